```python
import math
import jax, jax.numpy as jnp
from jax import lax
import numpy as np

D_MODEL = 1024
BATCH = 4
SEQ = 8192
DEPTH = 2

GROUP_WIDTH = 256
MIX_WIDTH = 4 * GROUP_WIDTH
NORM_EPS = 1e-6
SSM_HEADS = 4
SSM_HEAD_DIM = 64
SSM_GROUPS = 2
SSM_STATE = 64
SSM_CONV = 4
SSM_CHUNK = 128
SSM_XBC = SSM_HEADS * SSM_HEAD_DIM + 2 * SSM_GROUPS * SSM_STATE
SSM_IN = GROUP_WIDTH + SSM_XBC + SSM_HEADS
LRU_BLOCKS = 4
LRU_BLOCK_DIM = 64
LRU_CONV = 4
LRU_C = 8.0
LRU_IN = 2 * GROUP_WIDTH
RWKV_HEADS = 4
RWKV_HEAD_DIM = 64
RWKV_DECAY_LORA = 64
RWKV_ICLR_LORA = 64
RWKV_GATE_LORA = 128
RWKV_GN_EPS = 64e-5
RWKV_IN = 3 * GROUP_WIDTH + RWKV_DECAY_LORA + RWKV_ICLR_LORA + RWKV_GATE_LORA
NSA_HEADS = 4
NSA_HEAD_DIM = 64
NSA_CMP_STRIDE = 16
NSA_CMP_LEN = 2 * NSA_CMP_STRIDE
NSA_CMP_HIDDEN = 256
NSA_SEL_BLOCK = 64
NSA_SEL_TOPN = 16
NSA_WINDOW = 512
NSA_Q_BLOCK = 128
NSA_FORCE = 1e4
NSA_IN = NSA_HEADS * NSA_HEAD_DIM + 6 * NSA_HEAD_DIM + 3 * NSA_HEADS
N_IN = SSM_IN + LRU_IN + RWKV_IN + NSA_IN
PEER_HEADS = 8
PEER_NKEYS = 128
PEER_EXPERTS = PEER_NKEYS * PEER_NKEYS
PEER_KEY_DIM = 128
PEER_TOPK = 16
PEER_TOKEN_CHUNK = 128

kernel_name = "hymba_ssd_rglru_rwkv7_nsa_peer"


def rms_norm(x, g):
    xf = x.astype(jnp.float32)
    y = xf * lax.rsqrt(jnp.mean(xf * xf, axis=-1, keepdims=True) + NORM_EPS)
    return (y * g).astype(x.dtype)


def causal_dwconv(x, w, b):
    k, c = w.shape
    y = lax.conv_general_dilated(x, w[:, None, :].astype(x.dtype), window_strides=(1,),
                                 padding=[(k - 1, 0)], dimension_numbers=('NWC', 'WIO', 'NWC'),
                                 feature_group_count=c)
    return y + b


def shift_right(t):
    return jnp.pad(t, ((0, 0), (1, 0), (0, 0)))[:, :-1]


def masked_softmax(s, mask):
    s = jnp.where(mask, s.astype(jnp.float32), -jnp.inf)
    m = jnp.max(s, axis=-1, keepdims=True)
    m = jnp.where(jnp.isfinite(m), m, 0.0)
    e = jnp.where(mask, jnp.exp(s - m), 0.0)
    return e / jnp.maximum(jnp.sum(e, axis=-1, keepdims=True), 1e-30)


def alibi_slopes(n):
    return jnp.asarray(2.0 ** (-8.0 * (np.arange(n) + 1) / n), jnp.float32)


def segsum(a):
    t = a.shape[-1]
    ar = jnp.broadcast_to(a[..., :, None], a.shape + (t,))
    ar = jnp.where(jnp.tril(jnp.ones((t, t), bool), -1), ar, 0.0)
    ss = jnp.cumsum(ar, axis=-2)
    return jnp.where(jnp.tril(jnp.ones((t, t), bool), 0), ss, -jnp.inf)


def ssd_chunked(x, a_dt, bm, cm):
    b, s, h, p = x.shape
    n = bm.shape[-1]
    q = SSM_CHUNK
    c = s // q
    x = x.reshape(b, c, q, h, p)
    bm = bm.reshape(b, c, q, h, n)
    cm = cm.reshape(b, c, q, h, n)
    a = a_dt.reshape(b, c, q, h).transpose(0, 3, 1, 2)
    a_cs = jnp.cumsum(a, axis=-1)
    decay_in = jnp.exp(segsum(a))
    scores = jnp.einsum('bclhn,bcshn->bhcls', cm, bm) * decay_in
    y_diag = jnp.einsum('bhcls,bcshp->bclhp', scores, x)
    decay_to_end = jnp.exp(a_cs[..., -1:] - a_cs).transpose(0, 2, 3, 1)
    states = jnp.einsum('bclhn,bclhp->bchpn', bm * decay_to_end[..., None], x)
    states = jnp.concatenate([jnp.zeros_like(states[:, :1]), states], axis=1)
    chunk_decay = jnp.exp(segsum(jnp.pad(a_cs[..., -1], ((0, 0), (0, 0), (1, 0)))))
    states = jnp.einsum('bhzc,bchpn->bzhpn', chunk_decay, states)[:, :-1]
    decay_from_start = jnp.exp(a_cs).transpose(0, 2, 3, 1)
    y_off = jnp.einsum('bclhn,bchpn->bclhp', cm, states) * decay_from_start[..., None]
    return (y_diag + y_off).reshape(b, s, h, p)


def ssd_mixer(cols, conv_w, conv_b, dt_bias, a_log, d_skip, norm_w):
    b, s, _ = cols.shape
    di = SSM_HEADS * SSM_HEAD_DIM
    z, xbc, dt = jnp.split(cols, [di, di + SSM_XBC], axis=-1)
    xbc = jax.nn.silu(causal_dwconv(xbc, conv_w, conv_b)).astype(jnp.float32)
    xs, bm, cm = jnp.split(xbc, [di, di + SSM_GROUPS * SSM_STATE], axis=-1)
    rep = SSM_HEADS // SSM_GROUPS
    xs = xs.reshape(b, s, SSM_HEADS, SSM_HEAD_DIM)
    bm = jnp.repeat(bm.reshape(b, s, SSM_GROUPS, SSM_STATE), rep, axis=2)
    cm = jnp.repeat(cm.reshape(b, s, SSM_GROUPS, SSM_STATE), rep, axis=2)
    dt = jax.nn.softplus(dt.astype(jnp.float32) + dt_bias.astype(jnp.float32))
    a = -jnp.exp(a_log.astype(jnp.float32))
    y = ssd_chunked(xs * dt[..., None], dt * a, bm, cm)
    y = y + xs * d_skip.astype(jnp.float32)[:, None]
    y = y.reshape(b, s, di) * jax.nn.silu(z.astype(jnp.float32))
    yg = y.reshape(b, s, SSM_GROUPS, di // SSM_GROUPS)
    yg = yg * lax.rsqrt(jnp.mean(yg * yg, axis=-1, keepdims=True) + NORM_EPS)
    return (yg.reshape(b, s, di) * norm_w).astype(cols.dtype)


def linear_recurrence(a, u):
    def combine(l, r):
        return l[0] * r[0], r[0] * l[1] + r[1]
    _, h = lax.associative_scan(combine, (a, u), axis=1)
    return h


def rglru_mixer(cols, conv_w, conv_b, wa, ba, wi, bi, lam):
    b, s, _ = cols.shape
    gate, xr = jnp.split(cols, [GROUP_WIDTH], axis=-1)
    xr = causal_dwconv(xr, conv_w, conv_b).astype(jnp.float32)
    xb = xr.reshape(b, s, LRU_BLOCKS, LRU_BLOCK_DIM)
    r = jax.nn.sigmoid(jnp.einsum('bshi,hij->bshj', xb, wa.astype(jnp.float32)) + ba)
    i = jax.nn.sigmoid(jnp.einsum('bshi,hij->bshj', xb, wi.astype(jnp.float32)) + bi)
    log_sig_lam = -jax.nn.softplus(-lam.astype(jnp.float32).reshape(LRU_BLOCKS, LRU_BLOCK_DIM))
    log_a = LRU_C * r * log_sig_lam
    a = jnp.exp(log_a)
    u = jnp.sqrt(-jnp.expm1(2.0 * log_a)) * (i * xb)
    h = linear_recurrence(a, u)
    y = h.reshape(b, s, GROUP_WIDTH) * jax.nn.gelu(gate.astype(jnp.float32))
    return y.astype(cols.dtype)


def rwkv7_scan(r, w, k, v, kk, a):
    b, s, h, n = r.shape

    def step(state, inp):
        r_t, w_t, k_t, v_t, kk_t, a_t = inp
        sa = jnp.einsum('bhij,bhj->bhi', state, -kk_t)
        state = (state * w_t[:, :, None, :] + sa[..., None] * (kk_t * a_t)[:, :, None, :]
                 + v_t[..., None] * k_t[:, :, None, :])
        y_t = jnp.einsum('bhij,bhj->bhi', state, r_t)
        return state, y_t

    xs = tuple(jnp.moveaxis(t, 1, 0) for t in (r, w, k, v, kk, a))
    state0 = jnp.zeros((b, h, n, n), jnp.float32)
    _, y = lax.scan(step, state0, xs)
    return jnp.moveaxis(y, 0, 1)


def rwkv7_mixer(cols, mu, w0, w2, a0, a2, g2, k_k, k_a, r_k, ln_w, ln_b):
    b, s, _ = cols.shape
    c = cols.astype(jnp.float32)
    c = c + (shift_right(c) - c) * mu
    g_ = GROUP_WIDTH
    r, k, v, wd, ad, gd = jnp.split(
        c, [g_, 2 * g_, 3 * g_, 3 * g_ + RWKV_DECAY_LORA, 3 * g_ + RWKV_DECAY_LORA + RWKV_ICLR_LORA], axis=-1)
    w_log = -jax.nn.softplus(-(w0 + jnp.tanh(wd) @ w2)) - 0.5
    decay = jnp.exp(-jnp.exp(w_log))
    a = jax.nn.sigmoid(a0 + ad @ a2)
    g = jax.nn.sigmoid(gd) @ g2
    hs = lambda t: t.reshape(b, s, RWKV_HEADS, RWKV_HEAD_DIM)
    kk = hs(k * k_k)
    kk = kk / jnp.maximum(jnp.sqrt(jnp.sum(kk * kk, axis=-1, keepdims=True)), 1e-12)
    k = k * (1.0 + (a - 1.0) * k_a)
    r, decay, k, v, a = hs(r), hs(decay), hs(k), hs(v), hs(a)
    y = rwkv7_scan(r, decay, k, v, kk, a)
    mean = jnp.mean(y, axis=-1, keepdims=True)
    var = jnp.mean((y - mean) ** 2, axis=-1, keepdims=True)
    y = ((y - mean) * lax.rsqrt(var + RWKV_GN_EPS)).reshape(b, s, g_) * ln_w + ln_b
    y = y + (jnp.sum(r * k * r_k, axis=-1, keepdims=True) * v).reshape(b, s, g_)
    return (y * g).astype(cols.dtype)


def compress_blocks(t, pos_emb, w1, w2):
    b, s, d = t.shape
    ch = t.reshape(b, s // NSA_CMP_STRIDE, NSA_CMP_STRIDE, d)
    blocks = jnp.concatenate([ch[:, :-1], ch[:, 1:]], axis=2) + pos_emb
    hdn = jax.nn.gelu(blocks.reshape(b, blocks.shape[1], NSA_CMP_LEN * d) @ w1)
    return hdn @ w2


def nsa_overlap(s):
    nb = s // NSA_CMP_STRIDE - 1
    nsel = s // NSA_SEL_BLOCK
    cs = np.arange(nb) * NSA_CMP_STRIDE
    ss = np.arange(nsel) * NSA_SEL_BLOCK
    ov = np.minimum(cs[:, None] + NSA_CMP_LEN, ss[None, :] + NSA_SEL_BLOCK) - np.maximum(cs[:, None], ss[None, :])
    return jnp.asarray(np.clip(ov, 0, None) / NSA_CMP_LEN, jnp.float32)


def nsa_mixer(cols, cmp_pos, cmp_w1, cmp_w2, slopes):
    b, s, _ = cols.shape
    hd = NSA_HEAD_DIM
    qd = NSA_HEADS * hd
    q, kc, vc, ks, vs, kw, vw, gates = jnp.split(
        cols, [qd, qd + hd, qd + 2 * hd, qd + 3 * hd, qd + 4 * hd, qd + 5 * hd, qd + 6 * hd], axis=-1)
    q = q.reshape(b, s, NSA_HEADS, hd) * (hd ** -0.5)
    gates = jax.nn.sigmoid(gates.astype(jnp.float32).reshape(b, s, NSA_HEADS, 3))
    pos = jnp.arange(s)

    kcmp = compress_blocks(kc, cmp_pos[0], cmp_w1[0], cmp_w2[0])
    vcmp = compress_blocks(vc, cmp_pos[1], cmp_w1[1], cmp_w2[1])
    nb = kcmp.shape[1]
    blk_end = jnp.arange(nb) * NSA_CMP_STRIDE + NSA_CMP_LEN - 1
    dist = pos[:, None] - blk_end[None, :]
    sc = jnp.einsum('bshd,bnd->bhsn', q, kcmp).astype(jnp.float32) - slopes[:, None, None] * dist
    p_cmp = masked_softmax(sc, dist >= 0)
    o_cmp = jnp.einsum('bhsn,bnd->bshd', p_cmp.astype(vcmp.dtype), vcmp)

    nsel = s // NSA_SEL_BLOCK
    n_top = min(NSA_SEL_TOPN, nsel)
    imp = jnp.einsum('bhsn,nj->bsj', p_cmp, nsa_overlap(s))
    j = jnp.arange(nsel)
    cur = pos // NSA_SEL_BLOCK
    forced = (j[None, :] == 0) | (j[None, :] == cur[:, None]) | (j[None, :] == cur[:, None] - 1)
    imp = jnp.where(forced, NSA_FORCE, imp)
    imp = jnp.where(j[None, :] * NSA_SEL_BLOCK <= pos[:, None], imp, -jnp.inf)
    _, sel_idx = lax.top_k(imp, n_top)

    kw_pad = jnp.pad(kw, ((0, 0), (NSA_WINDOW, 0), (0, 0)))
    vw_pad = jnp.pad(vw, ((0, 0), (NSA_WINDOW, 0), (0, 0)))
    win_len = NSA_WINDOW + NSA_Q_BLOCK
    in_blk = jnp.arange(NSA_SEL_BLOCK)
    qb_len = NSA_Q_BLOCK

    def q_block(i):
        q0 = i * qb_len
        qb = lax.dynamic_slice_in_dim(q, q0, qb_len, axis=1)
        tq = q0 + jnp.arange(qb_len)
        idx = lax.dynamic_slice_in_dim(sel_idx, q0, qb_len, axis=1)
        tok = (idx[..., None] * NSA_SEL_BLOCK + in_blk).reshape(b, qb_len, n_top * NSA_SEL_BLOCK)
        kg = jax.vmap(lambda a_, i_: a_[i_])(ks, tok)
        vg = jax.vmap(lambda a_, i_: a_[i_])(vs, tok)
        d_sel = tq[None, :, None] - tok
        ssel = (jnp.einsum('bqhd,bqld->bhql', qb, kg).astype(jnp.float32)
                - slopes[None, :, None, None] * d_sel[:, None])
        p_sel = masked_softmax(ssel, (d_sel >= 0)[:, None])
        o_sel = jnp.einsum('bhql,bqld->bqhd', p_sel.astype(vg.dtype), vg)
        kwb = lax.dynamic_slice_in_dim(kw_pad, q0, win_len, axis=1)
        vwb = lax.dynamic_slice_in_dim(vw_pad, q0, win_len, axis=1)
        tk = q0 - NSA_WINDOW + jnp.arange(win_len)
        d_win = tq[:, None] - tk[None, :]
        m_win = (d_win >= 0) & (d_win < NSA_WINDOW) & (tk[None, :] >= 0)
        swin = jnp.einsum('bqhd,bkd->bhqk', qb, kwb).astype(jnp.float32) - slopes[:, None, None] * d_win
        p_win = masked_softmax(swin, m_win)
        o_win = jnp.einsum('bhqk,bkd->bqhd', p_win.astype(vwb.dtype), vwb)
        return o_sel, o_win

    o_sel, o_win = lax.map(q_block, jnp.arange(s // qb_len))
    o_sel = jnp.moveaxis(o_sel, 0, 1).reshape(b, s, NSA_HEADS, hd)
    o_win = jnp.moveaxis(o_win, 0, 1).reshape(b, s, NSA_HEADS, hd)
    out = gates[..., 0:1] * o_cmp + gates[..., 1:2] * o_sel + gates[..., 2:3] * o_win
    return out.reshape(b, s, qd).astype(cols.dtype)


def peer_ffn(x, wq, keys, u, v):
    b, s, d = x.shape
    t = b * s
    xt = x.reshape(t, d)
    q = (xt @ wq).reshape(t, PEER_HEADS, 2, PEER_KEY_DIM)
    sc = jnp.einsum('thcd,hcnd->thcn', q, keys).astype(jnp.float32)
    s1, i1 = lax.top_k(sc[:, :, 0], PEER_TOPK)
    s2, i2 = lax.top_k(sc[:, :, 1], PEER_TOPK)
    cand = (s1[..., :, None] + s2[..., None, :]).reshape(t, PEER_HEADS, PEER_TOPK * PEER_TOPK)
    cidx = (i1[..., :, None] * PEER_NKEYS + i2[..., None, :]).reshape(t, PEER_HEADS, PEER_TOPK * PEER_TOPK)
    top, pos = lax.top_k(cand, PEER_TOPK)
    eidx = jnp.take_along_axis(cidx, pos, axis=-1)
    gate = jax.nn.softmax(top, axis=-1)
    nc = t // PEER_TOKEN_CHUNK
    xc = xt.reshape(nc, PEER_TOKEN_CHUNK, d)
    ic = eidx.reshape(nc, PEER_TOKEN_CHUNK, PEER_HEADS * PEER_TOPK)
    gc = gate.reshape(nc, PEER_TOKEN_CHUNK, PEER_HEADS * PEER_TOPK).astype(x.dtype)

    def chunk(args):
        xb, ib, gb = args
        act = jax.nn.gelu(jnp.einsum('td,tkd->tk', xb, u[ib]))
        return jnp.einsum('tk,tkd->td', gb * act, v[ib])

    y = lax.map(chunk, (xc, ic, gc))
    return y.reshape(b, s, d)


def setup_inputs(seed: int = 0) -> dict:
    key = jax.random.key(seed)
    keys = jax.random.split(key, 48)
    counter = [0]

    def nk():
        counter[0] += 1
        return keys[counter[0] - 1]

    def nrm(shape, scale):
        return scale * jax.random.normal(nk(), shape, jnp.float32)

    def unif(shape, lo, hi):
        return jax.random.uniform(nk(), shape, jnp.float32, lo, hi)

    L = DEPTH
    x = nrm((BATCH, SEQ, D_MODEL), 1.0)
    mix_norm = 1.0 + nrm((L, D_MODEL), 0.02)
    w_in = nrm((L, D_MODEL, N_IN), D_MODEL ** -0.5)
    w_out = nrm((L, MIX_WIDTH, D_MODEL), 0.5 * MIX_WIDTH ** -0.5)
    ssm_conv_w = nrm((L, SSM_CONV, SSM_XBC), SSM_CONV ** -0.5)
    ssm_conv_b = nrm((L, SSM_XBC), 0.01)
    dt0 = jnp.exp(unif((L, SSM_HEADS), math.log(1e-3), math.log(1e-1)))
    ssm_dt_bias = dt0 + jnp.log(-jnp.expm1(-dt0))
    ssm_a_log = jnp.log(unif((L, SSM_HEADS), 1.0, 16.0))
    ssm_d = 1.0 + nrm((L, SSM_HEADS), 0.1)
    ssm_norm = 1.0 + nrm((L, GROUP_WIDTH), 0.02)
    lru_conv_w = nrm((L, LRU_CONV, GROUP_WIDTH), LRU_CONV ** -0.5)
    lru_conv_b = nrm((L, GROUP_WIDTH), 0.01)
    lru_wa = nrm((L, LRU_BLOCKS, LRU_BLOCK_DIM, LRU_BLOCK_DIM), LRU_BLOCK_DIM ** -0.5)
    lru_ba = nrm((L, LRU_BLOCKS, LRU_BLOCK_DIM), 0.01)
    lru_wi = nrm((L, LRU_BLOCKS, LRU_BLOCK_DIM, LRU_BLOCK_DIM), LRU_BLOCK_DIM ** -0.5)
    lru_bi = nrm((L, LRU_BLOCKS, LRU_BLOCK_DIM), 0.01)
    p_lam = unif((L, GROUP_WIDTH), 0.9, 0.999) ** (1.0 / LRU_C)
    lru_lambda = jnp.log(p_lam) - jnp.log1p(-p_lam)
    rwkv_mu = unif((L, RWKV_IN), 0.0, 1.0)
    rwkv_w0 = unif((L, GROUP_WIDTH), -5.0, -1.0)
    rwkv_w2 = nrm((L, RWKV_DECAY_LORA, GROUP_WIDTH), 0.5 * RWKV_DECAY_LORA ** -0.5)
    rwkv_a0 = nrm((L, GROUP_WIDTH), 0.1)
    rwkv_a2 = nrm((L, RWKV_ICLR_LORA, GROUP_WIDTH), 0.5 * RWKV_ICLR_LORA ** -0.5)
    rwkv_g2 = nrm((L, RWKV_GATE_LORA, GROUP_WIDTH), RWKV_GATE_LORA ** -0.5)
    rwkv_kk = 0.85 + nrm((L, GROUP_WIDTH), 0.05)
    rwkv_ka = 1.0 + nrm((L, GROUP_WIDTH), 0.05)
    rwkv_rk = nrm((L, RWKV_HEADS, RWKV_HEAD_DIM), 0.1)
    rwkv_ln_w = 1.0 + nrm((L, GROUP_WIDTH), 0.02)
    rwkv_ln_b = nrm((L, GROUP_WIDTH), 0.01)
    nsa_cmp_pos = nrm((L, 2, NSA_CMP_LEN, NSA_HEAD_DIM), 0.02)
    nsa_cmp_w1 = nrm((L, 2, NSA_CMP_LEN * NSA_HEAD_DIM, NSA_CMP_HIDDEN), (NSA_CMP_LEN * NSA_HEAD_DIM) ** -0.5)
    nsa_cmp_w2 = nrm((L, 2, NSA_CMP_HIDDEN, NSA_HEAD_DIM), NSA_CMP_HIDDEN ** -0.5)
    ffn_norm = 1.0 + nrm((L, D_MODEL), 0.02)
    peer_wq = nrm((L, D_MODEL, PEER_HEADS * 2 * PEER_KEY_DIM), D_MODEL ** -0.5)
    peer_keys = nrm((L, PEER_HEADS, 2, PEER_NKEYS, PEER_KEY_DIM), PEER_KEY_DIM ** -0.5)
    peer_u = nrm((L, PEER_EXPERTS, D_MODEL), D_MODEL ** -0.5)
    peer_v = nrm((L, PEER_EXPERTS, D_MODEL), (PEER_HEADS * PEER_TOPK) ** -0.5)
    final_norm = 1.0 + nrm((D_MODEL,), 0.02)
    return {"x": x, "mix_norm": mix_norm, "w_in": w_in, "w_out": w_out,
            "ssm_conv_w": ssm_conv_w, "ssm_conv_b": ssm_conv_b, "ssm_dt_bias": ssm_dt_bias,
            "ssm_a_log": ssm_a_log, "ssm_d": ssm_d, "ssm_norm": ssm_norm,
            "lru_conv_w": lru_conv_w, "lru_conv_b": lru_conv_b, "lru_wa": lru_wa, "lru_ba": lru_ba,
            "lru_wi": lru_wi, "lru_bi": lru_bi, "lru_lambda": lru_lambda,
            "rwkv_mu": rwkv_mu, "rwkv_w0": rwkv_w0, "rwkv_w2": rwkv_w2, "rwkv_a0": rwkv_a0,
            "rwkv_a2": rwkv_a2, "rwkv_g2": rwkv_g2, "rwkv_kk": rwkv_kk, "rwkv_ka": rwkv_ka,
            "rwkv_rk": rwkv_rk, "rwkv_ln_w": rwkv_ln_w, "rwkv_ln_b": rwkv_ln_b,
            "nsa_cmp_pos": nsa_cmp_pos, "nsa_cmp_w1": nsa_cmp_w1, "nsa_cmp_w2": nsa_cmp_w2,
            "ffn_norm": ffn_norm, "peer_wq": peer_wq, "peer_keys": peer_keys,
            "peer_u": peer_u, "peer_v": peer_v, "final_norm": final_norm}


def reference(x, mix_norm, w_in, w_out, ssm_conv_w, ssm_conv_b, ssm_dt_bias, ssm_a_log, ssm_d, ssm_norm,
              lru_conv_w, lru_conv_b, lru_wa, lru_ba, lru_wi, lru_bi, lru_lambda,
              rwkv_mu, rwkv_w0, rwkv_w2, rwkv_a0, rwkv_a2, rwkv_g2, rwkv_kk, rwkv_ka, rwkv_rk,
              rwkv_ln_w, rwkv_ln_b, nsa_cmp_pos, nsa_cmp_w1, nsa_cmp_w2,
              ffn_norm, peer_wq, peer_keys, peer_u, peer_v, final_norm):
    slopes = alibi_slopes(NSA_HEADS)
    splits = [SSM_IN, SSM_IN + LRU_IN, SSM_IN + LRU_IN + RWKV_IN]
    h = x
    for l in range(DEPTH):
        u = rms_norm(h, mix_norm[l])
        cols = u @ w_in[l]
        c_ssm, c_lru, c_rwkv, c_nsa = jnp.split(cols, splits, axis=-1)
        y_ssm = ssd_mixer(c_ssm, ssm_conv_w[l], ssm_conv_b[l], ssm_dt_bias[l], ssm_a_log[l], ssm_d[l], ssm_norm[l])
        y_lru = rglru_mixer(c_lru, lru_conv_w[l], lru_conv_b[l], lru_wa[l], lru_ba[l], lru_wi[l], lru_bi[l],
                            lru_lambda[l])
        y_rwkv = rwkv7_mixer(c_rwkv, rwkv_mu[l], rwkv_w0[l], rwkv_w2[l], rwkv_a0[l], rwkv_a2[l], rwkv_g2[l],
                             rwkv_kk[l], rwkv_ka[l], rwkv_rk[l], rwkv_ln_w[l], rwkv_ln_b[l])
        y_nsa = nsa_mixer(c_nsa, nsa_cmp_pos[l], nsa_cmp_w1[l], nsa_cmp_w2[l], slopes)
        mixed = jnp.concatenate([y_ssm, y_lru, y_rwkv, y_nsa], axis=-1)
        h = h + mixed @ w_out[l]
        h = h + peer_ffn(rms_norm(h, ffn_norm[l]), peer_wq[l], peer_keys[l], peer_u[l], peer_v[l])
    return rms_norm(h, final_norm)
```

```python
import functools
import math

import numpy as np
import jax
import jax.numpy as jnp
from jax import lax
from jax.experimental import pallas as pl
from jax.experimental.pallas import tpu as pltpu

F32 = jnp.float32
BF16 = jnp.bfloat16
I32 = jnp.int32

NORM_EPS = 1e-6
GROUP_WIDTH = 256
HEAD_DIM = 64
N_HEADS = 4
LRU_C = 8.0
NEG_BIG = -1e30


COL = dict(ssd_z=0, ssd_x=256, lru_gate=512, lru_x=768, rw_r=1024, rw_k=1280, rw_v=1536, rw_wa=1792, rw_g=1920,
           nsa_q=2048, nsa_kvc=2304, nsa_kvs=2432, nsa_kvw=2560, nsa_gate=2688, ssd_b=2816, ssd_c=2944, ssd_dt=3072)
NCOL = 3200


def _col_perm():
    src = np.full((NCOL,), -1, np.int64)

    def put(dst, s0, n):
        src[dst:dst + n] = np.arange(s0, s0 + n)

    put(COL["ssd_z"], 0, 256)
    put(COL["ssd_x"], 256, 256)
    put(COL["ssd_b"], 512, 128)
    put(COL["ssd_c"], 640, 128)
    put(COL["ssd_dt"], 768, 4)
    put(COL["lru_gate"], 772, 256)
    put(COL["lru_x"], 1028, 256)
    put(COL["rw_r"], 1284, 1024)
    put(COL["nsa_q"], 2308, 256)
    put(COL["nsa_kvc"], 2564, 128)
    put(COL["nsa_kvs"], 2692, 128)
    put(COL["nsa_kvw"], 2820, 128)
    put(COL["nsa_gate"], 2948, 12)
    return src


def _arrange_cols(a):
    src = _col_perm()
    out = jnp.take(a, jnp.asarray(np.maximum(src, 0)), axis=-1)
    return jnp.where(jnp.asarray(src >= 0), out, jnp.zeros((), a.dtype))


def _cparams(sem, vmem_mb=None):
    kw = dict(dimension_semantics=sem)
    if vmem_mb is not None:
        kw["vmem_limit_bytes"] = vmem_mb * 1024 * 1024
    return pltpu.CompilerParams(**kw)


def _dot(a, b):
    return jnp.dot(a.astype(BF16), b.astype(BF16), preferred_element_type=F32)


def _dot_nt(a, b):
    return lax.dot_general(a.astype(BF16), b.astype(BF16), (((1,), (1,)), ((), ())), preferred_element_type=F32)


def _dot_tn(a, b):
    return lax.dot_general(a.astype(BF16), b.astype(BF16), (((0,), (0,)), ((), ())), preferred_element_type=F32)


def _split3(x):
    h = x.astype(BF16)
    r = x - h.astype(F32)
    m = r.astype(BF16)
    l = (r - m.astype(F32)).astype(BF16)
    return h, m, l


def _split2(x):
    h = x.astype(BF16)
    return h, (x - h.astype(F32)).astype(BF16)


def _dot_exact_rhs(a, b_exact):
    h, m, l = _split3(a)
    d = lambda t: jnp.dot(t, b_exact, preferred_element_type=F32)
    return d(h) + d(m) + d(l)


def _dot_exact_lhs(a_exact, b):
    h, m, l = _split3(b)
    d = lambda t: jnp.dot(a_exact, t, preferred_element_type=F32)
    return d(h) + d(m) + d(l)


def _sigmoid(x):
    return 1.0 / (1.0 + jnp.exp(-x))


def _silu(x):
    return x * _sigmoid(x)


def _gelu_tanh(x):
    return 0.5 * x * (1.0 + jnp.tanh(math.sqrt(2.0 / math.pi) * (x + 0.044715 * (x * x * x))))


def _softplus(x):
    return jnp.maximum(x, 0.0) + jnp.log1p(jnp.exp(-jnp.abs(x)))


def _conv4(cur, tail, w, b):
    c = cur.shape[1]
    row8 = lax.broadcasted_iota(I32, (8, c), 0)
    acc = cur * w[3:4, :] + b
    for k in (1, 2, 3):
        cr = pltpu.roll(cur, k, axis=0)
        tr = pltpu.roll(tail, k, axis=0)
        head = jnp.where(row8 < k, tr, cr[:8])
        sh = jnp.concatenate([head, cr[8:]], axis=0)
        acc = acc + sh * w[3 - k:4 - k, :]
    return acc


def _norm_matmul_kernel(x_ref, g_ref, w_ref, o_ref, xn_ref):
    @pl.when(pl.program_id(1) == 0)
    def _():
        x = x_ref[...]
        y = x * lax.rsqrt(jnp.mean(x * x, axis=-1, keepdims=True) + NORM_EPS) * g_ref[...]
        xn_ref[...] = y.astype(BF16)

    o_ref[...] = jnp.dot(xn_ref[...], w_ref[...], preferred_element_type=F32)


def _norm_matmul(x, g, w, tm, tn):
    m, k = x.shape
    n = w.shape[1]
    return pl.pallas_call(
        _norm_matmul_kernel,
        grid=(m // tm, n // tn),
        in_specs=[pl.BlockSpec((tm, k), lambda i, j: (i, 0)),
                  pl.BlockSpec((1, k), lambda i, j: (0, 0)),
                  pl.BlockSpec((k, tn), lambda i, j: (0, j))],
        out_specs=pl.BlockSpec((tm, tn), lambda i, j: (i, j)),
        out_shape=jax.ShapeDtypeStruct((m, n), F32),
        scratch_shapes=[pltpu.VMEM((tm, k), BF16)],
        compiler_params=_cparams(("arbitrary", "arbitrary")),
    )(x, g, w)


SSD_CHUNK = 128


def _ssd_kernel(z_ref, x_ref, b_ref, c_ref, dt_ref, cwx_ref, cbx_ref, cwb_ref, cbb_ref, cwc_ref, cbc_ref,
                dtb_ref, alog_ref, d_ref, nw_ref, o_ref, tx_ref, tb_ref, tc_ref, st_ref):
    t = SSD_CHUNK

    @pl.when(pl.program_id(1) == 0)
    def _():
        tx_ref[...] = jnp.zeros_like(tx_ref)
        tb_ref[...] = jnp.zeros_like(tb_ref)
        tc_ref[...] = jnp.zeros_like(tc_ref)
        st_ref[...] = jnp.zeros_like(st_ref)

    xr, br, cr = x_ref[...], b_ref[...], c_ref[...]
    xs = _silu(_conv4(xr, tx_ref[...], cwx_ref[...], cbx_ref[...]))
    bm = _silu(_conv4(br, tb_ref[...], cwb_ref[...], cbb_ref[...]))
    cm = _silu(_conv4(cr, tc_ref[...], cwc_ref[...], cbc_ref[...]))
    tx_ref[...] = xr[t - 8:]
    tb_ref[...] = br[t - 8:]
    tc_ref[...] = cr[t - 8:]

    dt = _softplus(dt_ref[...] + dtb_ref[...])
    a = dt * (-jnp.exp(alog_ref[...]))
    ri = lax.broadcasted_iota(I32, (t, t), 0)
    ci = lax.broadcasted_iota(I32, (t, t), 1)
    tril = ri >= ci
    acs = _dot_exact_lhs(jnp.where(tril, 1.0, 0.0).astype(BF16), a)
    acs_t = acs.T
    d_row = d_ref[...]
    ys = []
    for h in range(N_HEADS):
        g = h // 2
        lo, hi = HEAD_DIM * h, HEAD_DIM * (h + 1)
        b_g = bm[:, HEAD_DIM * g:HEAD_DIM * (g + 1)]
        c_g = cm[:, HEAD_DIM * g:HEAD_DIM * (g + 1)]
        x_h = xs[:, lo:hi]
        acs_c = acs[:, h:h + 1]
        lmat = jnp.exp(jnp.where(tril, acs_c - acs_t[h:h + 1, :], NEG_BIG))
        scores = _dot_nt(c_g, b_g) * lmat
        xdt = x_h * dt[:, h:h + 1]
        st = st_ref[h]
        y = _dot(scores, xdt) + _dot(c_g, st) * jnp.exp(acs_c) + x_h * d_row[:, lo:hi]
        a_last = acs[t - 1:t, h:h + 1]
        bw = b_g * jnp.exp(a_last - acs_c)
        st_ref[h] = st * jnp.exp(a_last) + _dot_tn(bw, xdt)
        ys.append(y)
    y = jnp.concatenate(ys, axis=1) * _silu(z_ref[...])
    outs = []
    for g in range(2):
        yg = y[:, 128 * g:128 * (g + 1)]
        outs.append(yg * lax.rsqrt(jnp.mean(yg * yg, axis=-1, keepdims=True) + NORM_EPS))
    o_ref[...] = jnp.concatenate(outs, axis=1) * nw_ref[...]


def _ssd_mixer(cols, conv_w, conv_b, dt_bias, a_log, d_skip, norm_w):
    bsz, s, _ = cols.shape
    t = SSD_CHUNK
    row = lambda v, n: jnp.pad(v.reshape(1, -1), ((0, 0), (0, n - v.size)))
    full = lambda shp: pl.BlockSpec(shp, lambda b, c: (0,) * len(shp))
    colspec = lambda w, name: pl.BlockSpec((None, t, w), lambda b, c: (b, c, COL[name] // w))
    return pl.pallas_call(
        _ssd_kernel,
        grid=(bsz, s // t),
        in_specs=[colspec(256, "ssd_z"), colspec(256, "ssd_x"), colspec(128, "ssd_b"), colspec(128, "ssd_c"),
                  colspec(128, "ssd_dt"),
                  full((4, 256)), full((1, 256)), full((4, 128)), full((1, 128)), full((4, 128)), full((1, 128)),
                  full((1, 128)), full((1, 128)), full((1, 256)), full((1, 256))],
        out_specs=pl.BlockSpec((None, t, 256), lambda b, c: (b, c, 0)),
        out_shape=jax.ShapeDtypeStruct((bsz, s, 256), F32),
        scratch_shapes=[pltpu.VMEM((8, 256), F32), pltpu.VMEM((8, 128), F32), pltpu.VMEM((8, 128), F32),
                        pltpu.VMEM((N_HEADS, HEAD_DIM, HEAD_DIM), F32)],
        compiler_params=_cparams(("arbitrary", "arbitrary")),
    )(cols, cols, cols, cols, cols,
      conv_w[:, :256], conv_b[:256].reshape(1, 256), conv_w[:, 256:384], conv_b[256:384].reshape(1, 128),
      conv_w[:, 384:512], conv_b[384:512].reshape(1, 128),
      row(dt_bias, 128), row(a_log, 128), jnp.repeat(d_skip, HEAD_DIM).reshape(1, 256), norm_w.reshape(1, 256))


LRU_TILE = 256


def _lru_kernel(gate_ref, xr_ref, cw_ref, cb_ref, wa_ref, ba_ref, wi_ref, bi_ref, lam_ref, o_ref, tail_ref, h_ref):
    t = LRU_TILE

    @pl.when(pl.program_id(1) == 0)
    def _():
        tail_ref[...] = jnp.zeros_like(tail_ref)
        h_ref[...] = jnp.zeros_like(h_ref)

    xr = xr_ref[...]
    xc = _conv4(xr, tail_ref[...], cw_ref[...], cb_ref[...])
    tail_ref[...] = xr[t - 8:]
    r = _sigmoid(_dot(xc, wa_ref[...]) + ba_ref[...])
    i = _sigmoid(_dot(xc, wi_ref[...]) + bi_ref[...])
    log_a = LRU_C * r * (-_softplus(-lam_ref[...]))
    a = jnp.exp(log_a)
    u = jnp.sqrt(-jnp.tanh(log_a) * (a * a + 1.0)) * (i * xc)
    rows = lax.broadcasted_iota(I32, (t, GROUP_WIDTH), 0)
    d = 1
    while d < t:
        a_sh = pltpu.roll(a, d, axis=0)
        u_sh = pltpu.roll(u, d, axis=0)
        m = rows >= d
        u = jnp.where(m, a * u_sh + u, u)
        a = jnp.where(m, a * a_sh, a)
        d *= 2
    h = a * h_ref[7:8, :] + u
    h_ref[...] = h[t - 8:]
    o_ref[...] = h * _gelu_tanh(gate_ref[...])


def _lru_mixer(cols, conv_w, conv_b, wa, ba, wi, bi, lam):
    bsz, s, _ = cols.shape
    t = LRU_TILE
    cb = COL["lru_gate"] // 256
    bd = lambda w: jax.scipy.linalg.block_diag(*[w[k] for k in range(w.shape[0])]).astype(BF16)
    full = lambda shp: pl.BlockSpec(shp, lambda b, c: (0,) * len(shp))
    return pl.pallas_call(
        _lru_kernel,
        grid=(bsz, s // t),
        in_specs=[pl.BlockSpec((None, t, 256), lambda b, c: (b, c, cb)),
                  pl.BlockSpec((None, t, 256), lambda b, c: (b, c, cb + 1)),
                  full((4, 256)), full((1, 256)), full((256, 256)), full((1, 256)), full((256, 256)), full((1, 256)),
                  full((1, 256))],
        out_specs=pl.BlockSpec((None, t, 256), lambda b, c: (b, c, 0)),
        out_shape=jax.ShapeDtypeStruct((bsz, s, 256), F32),
        scratch_shapes=[pltpu.VMEM((8, 256), F32), pltpu.VMEM((8, 256), F32)],
        compiler_params=_cparams(("arbitrary", "arbitrary")),
    )(cols, cols, conv_w, conv_b.reshape(1, 256), bd(wa), ba.reshape(1, 256), bd(wi), bi.reshape(1, 256),
      lam.reshape(1, 256))


RW_TILE = 128
RWKV_GN_EPS = 64e-5


def _segsum_bc(x):
    r = x.shape[0]
    m = lax.broadcasted_iota(I32, (r, 128), 1) < HEAD_DIM
    outs = []
    for j in range(2):
        xj = x[:, 128 * j:128 * (j + 1)]
        lo = jnp.sum(jnp.where(m, xj, 0.0), axis=1, keepdims=True)
        hi = jnp.sum(jnp.where(m, 0.0, xj), axis=1, keepdims=True)
        outs.append(jnp.where(m, lo, hi))
    return jnp.concatenate(outs, axis=1)


def _rwkv_kernel(c_ref, mu_ref, w0_ref, w2_ref, a0_ref, a2_ref, g2_ref, kkp_ref, ka_ref, rk_ref, lnw_ref, lnb_ref,
                 o_ref, prev_ref, s_ref, r_s, w_s, k_s, v_s, kk_s, b_s, g_s, bon_s, y_s):
    nb = c_ref.shape[0]
    t = RW_TILE

    @pl.when(pl.program_id(0) == 0)
    def _():
        prev_ref[...] = jnp.zeros_like(prev_ref)
        s_ref[...] = jnp.zeros_like(s_ref)

    row = lax.broadcasted_iota(I32, (t, 1024), 0)
    for b in range(nb):
        c = c_ref[b]
        pr = jnp.where(row == 0, prev_ref[b, 7:8, :], pltpu.roll(c, 1, axis=0))
        prev_ref[b] = c[t - 8:]
        c = c + (pr - c) * mu_ref[...]
        r, k, v = c[:, 0:256], c[:, 256:512], c[:, 512:768]
        wd, ad, gd = c[:, 768:832], c[:, 832:896], c[:, 896:1024]
        w_log = -_softplus(-(w0_ref[...] + _dot(jnp.tanh(wd), w2_ref[...]))) - 0.5
        a = _sigmoid(a0_ref[...] + _dot(ad, a2_ref[...]))
        kk = k * kkp_ref[...]
        kk = kk / jnp.maximum(jnp.sqrt(_segsum_bc(kk * kk)), 1e-12)
        k2 = k * (1.0 + (a - 1.0) * ka_ref[...])
        r_s[b] = r
        w_s[b] = jnp.exp(-jnp.exp(w_log))
        k_s[b] = k2
        v_s[b] = v
        kk_s[b] = kk
        b_s[b] = kk * a
        g_s[b] = _dot(_sigmoid(gd), g2_ref[...])
        bon_s[b] = _segsum_bc(r * k2 * rk_ref[...]) * v

    eye4 = jnp.where(lax.broadcasted_iota(I32, (HEAD_DIM, 256), 0)
                     == lax.broadcasted_iota(I32, (HEAD_DIM, 256), 1) % HEAD_DIM, 1.0, 0.0)

    def step(tt, carry):
        for b in range(nb):
            s = s_ref[b]
            rowv = lambda ref: ref[b, pl.ds(tt, 1), :]
            sa = -_segsum_bc(s * rowv(kk_s))
            vbc = _segsum_bc(eye4 * rowv(v_s))
            s = s * rowv(w_s) + sa * rowv(b_s) + vbc * rowv(k_s)
            s_ref[b] = s
            ybc = _segsum_bc(s * rowv(r_s))
            y_s[b, pl.ds(tt, 1), :] = jnp.sum(ybc * eye4, axis=0, keepdims=True)
        return carry

    lax.fori_loop(0, t, step, 0)

    for b in range(nb):
        y = y_s[b]
        d = y - _segsum_bc(y) * (1.0 / HEAD_DIM)
        var = _segsum_bc(d * d) * (1.0 / HEAD_DIM)
        y = d * lax.rsqrt(var + RWKV_GN_EPS) * lnw_ref[...] + lnb_ref[...] + bon_s[b]
        o_ref[b] = y * g_s[b]


def _rwkv_mixer(cols, mu, w0, w2, a0, a2, g2, k_k, k_a, r_k, ln_w, ln_b):
    bsz, s, _ = cols.shape
    t = RW_TILE
    full = lambda shp: pl.BlockSpec(shp, lambda c: (0,) * len(shp))
    r1 = lambda v: v.reshape(1, -1)
    tile = lambda: pltpu.VMEM((bsz, t, 256), F32)
    return pl.pallas_call(
        _rwkv_kernel,
        grid=(s // t,),
        in_specs=[pl.BlockSpec((bsz, t, 1024), lambda c: (0, c, COL["rw_r"] // 1024)),
                  full((1, 1024)), full((1, 256)), full((64, 256)), full((1, 256)), full((64, 256)), full((128, 256)),
                  full((1, 256)), full((1, 256)), full((1, 256)), full((1, 256)), full((1, 256))],
        out_specs=pl.BlockSpec((bsz, t, 256), lambda c: (0, c, 0)),
        out_shape=jax.ShapeDtypeStruct((bsz, s, 256), F32),
        scratch_shapes=[pltpu.VMEM((bsz, 8, 1024), F32), pltpu.VMEM((bsz, HEAD_DIM, 256), F32)] + [tile() for _ in range(9)],
        compiler_params=_cparams(("arbitrary",)),
    )(cols, r1(mu), r1(w0), w2, r1(a0), a2, g2, r1(k_k), r1(k_a), r1(r_k), r1(ln_w), r1(ln_b))


NSA_STRIDE = 16
NSA_SEL_BLOCK = 64
NSA_TOPN = 16
NSA_WINDOW = 512
NSA_FORCE = 1e4
NSA_TQ = 256
NSA_TK = 512
NSA_SLOPES = tuple(2.0 ** (-8.0 * (h + 1) / N_HEADS) for h in range(N_HEADS))
SEL_OFF = -float(2 ** 30)


def _dot3_nt(a, b):
    ah, al = _split2(a)
    bh, bl = _split2(b)
    d = lambda x, y: lax.dot_general(x, y, (((1,), (1,)), ((), ())), preferred_element_type=F32)
    return d(ah, bh) + d(ah, bl) + d(al, bh)


def _nsa_compress_kernel(kv_ref, pos_ref, w1_ref, w2_ref, o_ref):
    nbp = o_ref.shape[0]
    outs = []
    for c in range(2):
        acc_a = jnp.zeros((nbp, 256), F32)
        acc_b = jnp.zeros((nbp, 256), F32)
        for r in range(NSA_STRIDE):
            xr = kv_ref[pl.ds(r, nbp, stride=NSA_STRIDE), :][:, HEAD_DIM * c:HEAD_DIM * (c + 1)]
            acc_a = acc_a + _dot(xr + pos_ref[c, r:r + 1, :], w1_ref[c, 64 * r:64 * (r + 1), :])
            acc_b = acc_b + _dot(xr + pos_ref[c, 16 + r:17 + r, :], w1_ref[c, 1024 + 64 * r:1024 + 64 * (r + 1), :])
        hdn = _gelu_tanh(acc_a + pltpu.roll(acc_b, nbp - 1, axis=0))
        outs.append(_dot(hdn, w2_ref[c]))
    o_ref[...] = jnp.concatenate(outs, axis=1)


def _nsa_cmp_kernel(q_ref, kv_ref, ov_ref, ocmp_ref, qaug_ref):
    tq = NSA_TQ
    nbp = kv_ref.shape[0]
    q0 = pl.program_id(1) * tq
    kv = kv_ref[...]
    kc, vc = kv[:, :HEAD_DIM], kv[:, HEAD_DIM:]
    pos = q0 + lax.broadcasted_iota(I32, (tq, nbp), 0)
    blk_end = lax.broadcasted_iota(I32, (tq, nbp), 1) * NSA_STRIDE + (2 * NSA_STRIDE - 1)
    dist = pos - blk_end
    mask = dist >= 0
    distf = dist.astype(F32)
    q = q_ref[...] * (HEAD_DIM ** -0.5)
    lane = lax.broadcasted_iota(I32, (tq, 128), 1)
    psum = jnp.zeros((tq, nbp), F32)
    for h in range(N_HEADS):
        qh = q[:, HEAD_DIM * h:HEAD_DIM * (h + 1)]
        s = jnp.where(mask, _dot3_nt(qh, kc) - NSA_SLOPES[h] * distf, -jnp.inf)
        m = jnp.max(s, axis=1, keepdims=True)
        m = jnp.where(m == -jnp.inf, 0.0, m)
        e = jnp.where(mask, jnp.exp(s - m), 0.0)
        p = e / jnp.maximum(jnp.sum(e, axis=1, keepdims=True), 1e-30)
        ocmp_ref[h] = _dot(p, vc)
        psum = psum + p
        qt = q[:, 128 * (h // 2):128 * (h // 2 + 1)]
        keep = (lane < HEAD_DIM) if h % 2 == 0 else (lane >= HEAD_DIM)
        qaug_ref[h, :, 0:128] = jnp.where(keep, qt, 0.0).astype(BF16)
    imp_t = _dot_exact_rhs(psum, ov_ref[...]).T
    j = lax.broadcasted_iota(I32, (128, tq), 0)
    pos_t = q0 + lax.broadcasted_iota(I32, (128, tq), 1)
    cur = pos_t // NSA_SEL_BLOCK
    forced = (j == 0) | (j == cur) | (j == cur - 1)
    valid = j * NSA_SEL_BLOCK <= pos_t
    x = jnp.where(valid, jnp.where(forced, NSA_FORCE, imp_t), -jnp.inf)
    jf = j.astype(F32)
    picked = jnp.zeros((128, tq), F32)
    for _ in range(NSA_TOPN):
        mx = jnp.max(x, axis=0, keepdims=True)
        first = jnp.min(jnp.where(x == mx, jf, 128.0), axis=0, keepdims=True)
        hit = jf == first
        picked = jnp.where(hit, 1.0, picked)
        x = jnp.where(hit, -jnp.inf, x)
    bias = jnp.where(valid, jnp.where(picked > 0.0, 0.0, SEL_OFF), SEL_OFF).T.astype(BF16)
    for h in range(N_HEADS):
        qaug_ref[h, :, 128:256] = bias


def _nsa_flash_kernel(qaug_ref, kv_ref, o_ref, kaug_s, v_s, *, window):
    tq, tk = NSA_TQ, NSA_TK
    s_len = kv_ref.shape[0]
    tk = min(tk, s_len)
    kdim = 128 if window else 256
    qi = pl.program_id(1)
    q0 = qi * tq

    @pl.when(qi == 0)
    def _():
        def build(i, c):
            r0 = pl.multiple_of(i * tk, tk)
            kv = kv_ref[pl.ds(r0, tk), :]
            k = kv[:, :HEAD_DIM]
            parts = [k, k]
            if not window:
                blk = (r0 + lax.broadcasted_iota(I32, (tk, 128), 0)) // NSA_SEL_BLOCK
                parts.append(jnp.where(blk == lax.broadcasted_iota(I32, (tk, 128), 1), 1.0, 0.0))
            kaug_s[pl.ds(r0, tk), :] = jnp.concatenate(parts, axis=1).astype(BF16)
            v_s[pl.ds(r0, tk), :] = kv[:, HEAD_DIM:].astype(BF16)
            return c
        lax.fori_loop(0, s_len // tk, build, 0)

    q = qaug_ref[...][:, :, :kdim].reshape(N_HEADS * tq, kdim)
    row = lax.broadcasted_iota(I32, (N_HEADS * tq, tk), 0)
    t_q = q0 + (row & (tq - 1))
    head = lax.broadcasted_iota(I32, (N_HEADS * tq, 1), 0) // tq
    slope = jnp.where(head == 0, NSA_SLOPES[0], jnp.where(head == 1, NSA_SLOPES[1],
                      jnp.where(head == 2, NSA_SLOPES[2], NSA_SLOPES[3])))
    col = lax.broadcasted_iota(I32, (N_HEADS * tq, tk), 1)

    def body(kt, carry):
        m, l, acc = carry
        k0 = pl.multiple_of(kt * tk, tk)
        s = lax.dot_general(q, kaug_s[pl.ds(k0, tk), :], (((1,), (1,)), ((), ())), preferred_element_type=F32)
        t_k = k0 + col
        s = s + slope * (t_k - q0).astype(F32)
        ok = t_k <= t_q
        if window:
            ok = ok & (t_q - t_k < NSA_WINDOW)
        m_new = jnp.maximum(m, jnp.max(jnp.where(ok, s, NEG_BIG), axis=1, keepdims=True))
        p = jnp.where(ok, jnp.exp(s - m_new), 0.0)
        alpha = jnp.exp(m - m_new)
        l = alpha * l + jnp.sum(p, axis=1, keepdims=True)
        acc = alpha * acc + jnp.dot(p.astype(BF16), v_s[pl.ds(k0, tk), :], preferred_element_type=F32)
        return m_new, l, acc

    lo = jnp.maximum(q0 - NSA_WINDOW, 0) // tk if window else 0
    hi = (q0 + tq - 1) // tk + 1
    init = (jnp.full((N_HEADS * tq, 1), NEG_BIG, F32), jnp.zeros((N_HEADS * tq, 1), F32),
            jnp.zeros((N_HEADS * tq, HEAD_DIM), F32))
    m, l, acc = lax.fori_loop(lo, hi, body, init)
    o_ref[...] = (acc / l).reshape(N_HEADS, tq, HEAD_DIM)


def _nsa_overlap(s):
    nbp, nsel = s // NSA_STRIDE, s // NSA_SEL_BLOCK
    cs = np.arange(nbp) * NSA_STRIDE
    ss = np.arange(128) * NSA_SEL_BLOCK
    ov = np.minimum(cs[:, None] + 2 * NSA_STRIDE, ss[None, :] + NSA_SEL_BLOCK) - np.maximum(cs[:, None], ss[None, :])
    ov = np.clip(ov, 0, None) / (2 * NSA_STRIDE)
    ov[nbp - 1:, :] = 0.0
    ov[:, nsel:] = 0.0
    return jnp.asarray(ov, BF16)


def _nsa_branches(cols, cmp_pos, cmp_w1, cmp_w2):
    bsz, s, _ = cols.shape
    nbp = s // NSA_STRIDE
    tq = NSA_TQ
    kvcmp = pl.pallas_call(
        _nsa_compress_kernel,
        grid=(bsz,),
        in_specs=[pl.BlockSpec((None, s, 128), lambda b: (b, 0, COL["nsa_kvc"] // 128)),
                  pl.BlockSpec((2, 32, 64), lambda b: (0, 0, 0)),
                  pl.BlockSpec((2, 2048, 256), lambda b: (0, 0, 0)),
                  pl.BlockSpec((2, 256, 64), lambda b: (0, 0, 0))],
        out_specs=pl.BlockSpec((None, nbp, 128), lambda b: (b, 0, 0)),
        out_shape=jax.ShapeDtypeStruct((bsz, nbp, 128), F32),
        compiler_params=_cparams(("arbitrary",), 48),
    )(cols, cmp_pos, cmp_w1.astype(BF16), cmp_w2)
    head_out = lambda dt, w: (pl.BlockSpec((None, N_HEADS, tq, w), lambda b, i: (b, 0, i, 0)),
                              jax.ShapeDtypeStruct((bsz, N_HEADS, s, w), dt))
    (oc_spec, oc_shape), (qa_spec, qa_shape) = head_out(F32, HEAD_DIM), head_out(BF16, 256)
    o_cmp, qaug = pl.pallas_call(
        _nsa_cmp_kernel,
        grid=(bsz, s // tq),
        in_specs=[pl.BlockSpec((None, tq, 256), lambda b, i: (b, i, COL["nsa_q"] // 256)),
                  pl.BlockSpec((None, nbp, 128), lambda b, i: (b, 0, 0)),
                  pl.BlockSpec((nbp, 128), lambda b, i: (0, 0))],
        out_specs=[oc_spec, qa_spec],
        out_shape=[oc_shape, qa_shape],
        compiler_params=_cparams(("arbitrary", "arbitrary"), 48),
    )(cols, kvcmp, _nsa_overlap(s))

    def flash(name, window):
        return pl.pallas_call(
            functools.partial(_nsa_flash_kernel, window=window),
            grid=(bsz, s // tq),
            in_specs=[qa_spec, pl.BlockSpec((None, s, 128), lambda b, i: (b, 0, COL[name] // 128))],
            out_specs=oc_spec,
            out_shape=oc_shape,
            scratch_shapes=[pltpu.VMEM((s, 128 if window else 256), BF16), pltpu.VMEM((s, HEAD_DIM), BF16)],
            compiler_params=_cparams(("arbitrary", "arbitrary"), 48),
        )(qaug, cols)

    return o_cmp, flash("nsa_kvs", False), flash("nsa_kvw", True)


def _nsa_mixer(cols, cmp_pos, cmp_w1, cmp_w2):
    o_cmp, o_sel, o_win = _nsa_branches(cols, cmp_pos, cmp_w1, cmp_w2)
    bsz, s, _ = cols.shape
    g = jax.nn.sigmoid(cols[..., COL["nsa_gate"]:COL["nsa_gate"] + 12]).reshape(bsz, s, N_HEADS, 3)
    tr = lambda o: jnp.moveaxis(o, 1, 2)
    out = g[..., 0:1] * tr(o_cmp) + g[..., 1:2] * tr(o_sel) + g[..., 2:3] * tr(o_win)
    return out.reshape(bsz, s, 256)


MIX_TILE = 256


def _mix_out_kernel(h_ref, ssm_ref, lru_ref, rw_ref, oc_ref, os_ref, ow_ref, gate_ref, w_ref, o_ref):
    acc = h_ref[...]
    acc = acc + _dot(ssm_ref[...], w_ref[0:256, :])
    acc = acc + _dot(lru_ref[...], w_ref[256:512, :])
    acc = acc + _dot(rw_ref[...], w_ref[512:768, :])
    g = _sigmoid(gate_ref[...])
    for h in range(N_HEADS):
        y = (g[:, 3 * h:3 * h + 1] * oc_ref[h] + g[:, 3 * h + 1:3 * h + 2] * os_ref[h]
             + g[:, 3 * h + 2:3 * h + 3] * ow_ref[h])
        acc = acc + _dot(y, w_ref[768 + HEAD_DIM * h:768 + HEAD_DIM * (h + 1), :])
    o_ref[...] = acc


def _mix_out(h, y_ssm, y_lru, y_rw, o_cmp, o_sel, o_win, cols, w_out):
    bsz, s, d = h.shape
    t = MIX_TILE
    tok = lambda w: pl.BlockSpec((None, t, w), lambda b, i: (b, i, 0))
    hd = pl.BlockSpec((None, N_HEADS, t, HEAD_DIM), lambda b, i: (b, 0, i, 0))
    return pl.pallas_call(
        _mix_out_kernel,
        grid=(bsz, s // t),
        in_specs=[tok(d), tok(256), tok(256), tok(256), hd, hd, hd,
                  pl.BlockSpec((None, t, 128), lambda b, i: (b, i, COL["nsa_gate"] // 128)),
                  pl.BlockSpec((4 * GROUP_WIDTH, d), lambda b, i: (0, 0))],
        out_specs=tok(d),
        out_shape=jax.ShapeDtypeStruct((bsz, s, d), F32),
        compiler_params=_cparams(("arbitrary", "arbitrary")),
    )(h, y_ssm, y_lru, y_rw, o_cmp, o_sel, o_win, cols, w_out.astype(BF16))


PEER_HEADS = 8
PEER_NKEYS = 128
PEER_TOPK = 16
PEER_TT = 128
PEER_TQ = 256


def _norm_matmul2_kernel(x_ref, g_ref, w_ref, o_ref, xn_out_ref, xn_ref):
    @pl.when(pl.program_id(1) == 0)
    def _():
        x = x_ref[...]
        y = x * lax.rsqrt(jnp.mean(x * x, axis=-1, keepdims=True) + NORM_EPS) * g_ref[...]
        xn_out_ref[...] = y
        xn_ref[...] = y.astype(BF16)

    o_ref[...] = jnp.dot(xn_ref[...], w_ref[...], preferred_element_type=F32)


def _norm_matmul2(x, g, w, tm, tn):
    m, k = x.shape
    n = w.shape[1]
    return pl.pallas_call(
        _norm_matmul2_kernel,
        grid=(m // tm, n // tn),
        in_specs=[pl.BlockSpec((tm, k), lambda i, j: (i, 0)),
                  pl.BlockSpec((1, k), lambda i, j: (0, 0)),
                  pl.BlockSpec((k, tn), lambda i, j: (0, j))],
        out_specs=[pl.BlockSpec((tm, tn), lambda i, j: (i, j)), pl.BlockSpec((tm, k), lambda i, j: (i, 0))],
        out_shape=[jax.ShapeDtypeStruct((m, n), F32), jax.ShapeDtypeStruct((m, k), F32)],
        scratch_shapes=[pltpu.VMEM((tm, k), BF16)],
        compiler_params=_cparams(("arbitrary", "arbitrary")),
    )(x, g, w)


def _top_rows(x, n, payload=None):
    r = x.shape[0]
    rid = lax.broadcasted_iota(I32, x.shape, 0).astype(F32)
    vals, ids, pays = [], [], []
    for _ in range(n):
        mx = jnp.max(x, axis=0, keepdims=True)
        first = jnp.min(jnp.where(x == mx, rid, float(r)), axis=0, keepdims=True)
        hit = rid == first
        vals.append(mx)
        ids.append(first)
        if payload is not None:
            pays.append(jnp.sum(jnp.where(hit, payload, 0.0), axis=0, keepdims=True))
        x = jnp.where(hit, -jnp.inf, x)
    cat = lambda v: jnp.concatenate(v, axis=0)
    return (cat(vals), cat(ids)) + ((cat(pays),) if payload is not None else ())


def _peer_route_kernel(q_ref, keys_ref, eidx_ref, gate_ref):
    k = PEER_TOPK
    for h in range(PEER_HEADS):
        s, i = [], []
        for c in range(2):
            qh = q_ref[:, 128 * (2 * h + c):128 * (2 * h + c + 1)]
            sv, si = _top_rows(_dot3_nt(keys_ref[h, c], qh), k)
            s.append(sv)
            i.append(si)
        cand = jnp.concatenate([s[0][a:a + 1, :] + s[1] for a in range(k)], axis=0)
        cidx = jnp.concatenate([i[0][a:a + 1, :] * float(PEER_NKEYS) + i[1] for a in range(k)], axis=0)
        top, _, eidx = _top_rows(cand, k, payload=cidx)
        e = jnp.exp(top - top[0:1, :])
        gate_ref[k * h:k * (h + 1), :] = e / jnp.sum(e, axis=0, keepdims=True)
        eidx_ref[k * h:k * (h + 1), :] = eidx.astype(I32)


def _peer_route(q, keys):
    t = q.shape[0]
    tq = min(PEER_TQ, t)
    nk = PEER_HEADS * PEER_TOPK
    return pl.pallas_call(
        _peer_route_kernel,
        grid=(t // tq,),
        in_specs=[pl.BlockSpec((tq, q.shape[1]), lambda i: (i, 0)),
                  pl.BlockSpec(keys.shape, lambda i: (0, 0, 0, 0))],
        out_specs=[pl.BlockSpec((nk, tq), lambda i: (0, i)), pl.BlockSpec((nk, tq), lambda i: (0, i))],
        out_shape=[jax.ShapeDtypeStruct((nk, t), I32), jax.ShapeDtypeStruct((nk, t), F32)],
        compiler_params=_cparams(("arbitrary",), 48),
    )(q, keys)


def _pack_table(tab):
    e, d = tab.shape
    bits = lax.bitcast_convert_type(tab.astype(BF16), jnp.uint16).astype(jnp.uint32).reshape(e // 2, 2, d)
    word = (bits[:, 1, :] << 16) | bits[:, 0, :]
    return lax.bitcast_convert_type(word, I32).reshape(e // 2, d // 128, 128)


def _expert_row(tbl_ref, e):
    word = tbl_ref[e >> 1]
    sh = (1 - (e & 1)) * 16
    return pltpu.bitcast((word << sh) & jnp.int32(-65536), F32)


def _peer_act_kernel(idx_ref, x_ref, gate_ref, tbl_ref, o_ref):
    nk, tt = gate_ref.shape
    sub = lax.broadcasted_iota(I32, (8, 128), 0)
    lane = lax.broadcasted_iota(I32, (nk, tt), 1)

    def tok(t, acc):
        x = x_ref[t]
        groups = []
        for g in range(nk // 8):
            q = jnp.zeros((8, 128), F32)
            for j in range(8):
                p = _expert_row(tbl_ref, idx_ref[8 * g + j, t]) * x
                p = p + pltpu.roll(p, 4, axis=0)
                p = p + pltpu.roll(p, 2, axis=0)
                p = p + pltpu.roll(p, 1, axis=0)
                q = jnp.where(sub == j, p, q)
            groups.append(q)
        s = jnp.sum(jnp.concatenate(groups, axis=0), axis=1, keepdims=True)
        return jnp.where(lane == t, s, acc)

    acc = lax.fori_loop(0, tt, tok, jnp.zeros((nk, tt), F32))
    o_ref[...] = gate_ref[...] * _gelu_tanh(acc)


def _peer_out_kernel(idx_ref, w_ref, tbl_ref, h_ref, o_ref):
    nk, tt = idx_ref.shape

    def tok(t, c):
        accs = [jnp.zeros((8, 128), F32) for _ in range(4)]
        for k in range(nk):
            accs[k % 4] = accs[k % 4] + _expert_row(tbl_ref, idx_ref[k, t]) * w_ref[k, t]
        o_ref[t] = h_ref[t] + ((accs[0] + accs[1]) + (accs[2] + accs[3]))
        return c

    lax.fori_loop(0, tt, tok, 0)


def _peer_experts(eidx_t, gate_t, xn3, h3, tbl_u, tbl_v):
    nk, t = eidx_t.shape
    tt = min(PEER_TT, t)
    smem = lambda: pl.BlockSpec((nk, tt), lambda i: (0, i), memory_space=pltpu.SMEM)
    vm = lambda: pl.BlockSpec((nk, tt), lambda i: (0, i))
    tok = lambda: pl.BlockSpec((tt, 8, 128), lambda i: (i, 0, 0))
    tbl = lambda: pl.BlockSpec(tbl_u.shape, lambda i: (0, 0, 0), pipeline_mode=pl.Buffered(1))
    w_t = pl.pallas_call(
        _peer_act_kernel,
        grid=(t // tt,),
        in_specs=[smem(), tok(), vm(), tbl()],
        out_specs=vm(),
        out_shape=jax.ShapeDtypeStruct((nk, t), F32),
        compiler_params=_cparams(("arbitrary",), 48),
    )(eidx_t, xn3, gate_t, tbl_u)
    return pl.pallas_call(
        _peer_out_kernel,
        grid=(t // tt,),
        in_specs=[smem(), smem(), tbl(), tok()],
        out_specs=tok(),
        out_shape=jax.ShapeDtypeStruct(h3.shape, F32),
        compiler_params=_cparams(("arbitrary",), 48),
    )(eidx_t, w_t, tbl_v, h3)


def _peer_block(h, g, wq, keys, u, v):
    t, d = h.shape
    q, xn = _norm_matmul2(h, g.reshape(1, d), wq.astype(BF16), min(512, t), 1024)
    eidx_t, gate_t = _peer_route(q, keys)
    out = _peer_experts(eidx_t, gate_t, xn.reshape(t, 8, 128), h.reshape(t, 8, 128), _pack_table(u), _pack_table(v))
    return out.reshape(t, d)


def _rmsnorm_kernel(x_ref, g_ref, o_ref):
    x = x_ref[...]
    o_ref[...] = x * lax.rsqrt(jnp.mean(x * x, axis=-1, keepdims=True) + NORM_EPS) * g_ref[...]


def _rmsnorm(x, g, tm=512):
    m, d = x.shape
    return pl.pallas_call(
        _rmsnorm_kernel,
        grid=(m // tm,),
        in_specs=[pl.BlockSpec((tm, d), lambda i: (i, 0)), pl.BlockSpec((1, d), lambda i: (0, 0))],
        out_specs=pl.BlockSpec((tm, d), lambda i: (i, 0)),
        out_shape=jax.ShapeDtypeStruct((m, d), F32),
        compiler_params=_cparams(("arbitrary",)),
    )(x, g.reshape(1, d))


def kernel(x, mix_norm, w_in, w_out, ssm_conv_w, ssm_conv_b, ssm_dt_bias, ssm_a_log, ssm_d, ssm_norm, lru_conv_w, lru_conv_b, lru_wa, lru_ba, lru_wi, lru_bi, lru_lambda, rwkv_mu, rwkv_w0, rwkv_w2, rwkv_a0, rwkv_a2, rwkv_g2, rwkv_kk, rwkv_ka, rwkv_rk, rwkv_ln_w, rwkv_ln_b, nsa_cmp_pos, nsa_cmp_w1, nsa_cmp_w2, ffn_norm, peer_wq, peer_keys, peer_u, peer_v, final_norm):
    bsz, s, d = x.shape
    t = bsz * s
    h = x
    for l in range(w_in.shape[0]):
        w_l = _arrange_cols(w_in[l]).astype(BF16)
        cols = _norm_matmul(h.reshape(t, d), mix_norm[l].reshape(1, d), w_l, 512, 640).reshape(bsz, s, NCOL)
        y_ssm = _ssd_mixer(cols, ssm_conv_w[l], ssm_conv_b[l], ssm_dt_bias[l], ssm_a_log[l], ssm_d[l], ssm_norm[l])
        y_lru = _lru_mixer(cols, lru_conv_w[l], lru_conv_b[l], lru_wa[l], lru_ba[l], lru_wi[l], lru_bi[l], lru_lambda[l])
        y_rw = _rwkv_mixer(cols, rwkv_mu[l], rwkv_w0[l], rwkv_w2[l], rwkv_a0[l], rwkv_a2[l], rwkv_g2[l],
                           rwkv_kk[l], rwkv_ka[l], rwkv_rk[l], rwkv_ln_w[l], rwkv_ln_b[l])
        o_cmp, o_sel, o_win = _nsa_branches(cols, nsa_cmp_pos[l], nsa_cmp_w1[l], nsa_cmp_w2[l])
        h = _mix_out(h, y_ssm, y_lru, y_rw, o_cmp, o_sel, o_win, cols, w_out[l])
        h = _peer_block(h.reshape(t, d), ffn_norm[l], peer_wq[l], peer_keys[l], peer_u[l], peer_v[l]).reshape(bsz, s, d)
    return _rmsnorm(h.reshape(t, d), final_norm).reshape(bsz, s, d)
```

```python
import functools
import math

import numpy as np
import jax
import jax.numpy as jnp
from jax import lax
from jax.experimental import pallas as pl
from jax.experimental.pallas import tpu as pltpu

F32 = jnp.float32
BF16 = jnp.bfloat16
I32 = jnp.int32

NORM_EPS = 1e-6
GROUP_WIDTH = 256
HEAD_DIM = 64
N_HEADS = 4
LRU_C = 8.0
NEG_BIG = -1e30


COL = dict(ssd_z=0, ssd_x=256, lru_gate=512, lru_x=768, rw_r=1024, rw_k=1280, rw_v=1536, rw_wa=1792, rw_g=1920,
           nsa_q=2048, nsa_kvc=2304, nsa_kvs=2432, nsa_kvw=2560, nsa_gate=2688, ssd_b=2816, ssd_c=2944, ssd_dt=3072)
NCOL = 3200


def _col_perm():
    src = np.full((NCOL,), -1, np.int64)

    def put(dst, s0, n):
        src[dst:dst + n] = np.arange(s0, s0 + n)

    put(COL["ssd_z"], 0, 256)
    put(COL["ssd_x"], 256, 256)
    put(COL["ssd_b"], 512, 128)
    put(COL["ssd_c"], 640, 128)
    put(COL["ssd_dt"], 768, 4)
    put(COL["lru_gate"], 772, 256)
    put(COL["lru_x"], 1028, 256)
    put(COL["rw_r"], 1284, 1024)
    put(COL["nsa_q"], 2308, 256)
    put(COL["nsa_kvc"], 2564, 128)
    put(COL["nsa_kvs"], 2692, 128)
    put(COL["nsa_kvw"], 2820, 128)
    put(COL["nsa_gate"], 2948, 12)
    return src


def _arrange_cols(a):
    src = _col_perm()
    out = jnp.take(a, jnp.asarray(np.maximum(src, 0)), axis=-1)
    return jnp.where(jnp.asarray(src >= 0), out, jnp.zeros((), a.dtype))


def _cparams(sem, vmem_mb=None):
    kw = dict(dimension_semantics=sem)
    if vmem_mb is not None:
        kw["vmem_limit_bytes"] = vmem_mb * 1024 * 1024
    return pltpu.CompilerParams(**kw)


def _dot(a, b):
    return jnp.dot(a.astype(BF16), b.astype(BF16), preferred_element_type=F32)


def _dot_nt(a, b):
    return lax.dot_general(a.astype(BF16), b.astype(BF16), (((1,), (1,)), ((), ())), preferred_element_type=F32)


def _dot_tn(a, b):
    return lax.dot_general(a.astype(BF16), b.astype(BF16), (((0,), (0,)), ((), ())), preferred_element_type=F32)


def _split3(x):
    h = x.astype(BF16)
    r = x - h.astype(F32)
    m = r.astype(BF16)
    l = (r - m.astype(F32)).astype(BF16)
    return h, m, l


def _split2(x):
    h = x.astype(BF16)
    return h, (x - h.astype(F32)).astype(BF16)


def _dot_exact_rhs(a, b_exact):
    h, m, l = _split3(a)
    d = lambda t: jnp.dot(t, b_exact, preferred_element_type=F32)
    return d(h) + d(m) + d(l)


def _dot_exact_lhs(a_exact, b):
    h, m, l = _split3(b)
    d = lambda t: jnp.dot(a_exact, t, preferred_element_type=F32)
    return d(h) + d(m) + d(l)


def _sigmoid(x):
    return 1.0 / (1.0 + jnp.exp(-x))


def _silu(x):
    return x * _sigmoid(x)


def _gelu_tanh(x):
    return 0.5 * x * (1.0 + jnp.tanh(math.sqrt(2.0 / math.pi) * (x + 0.044715 * (x * x * x))))


def _softplus(x):
    return jnp.maximum(x, 0.0) + jnp.log1p(jnp.exp(-jnp.abs(x)))


def _conv4(cur, tail, w, b):
    c = cur.shape[1]
    row8 = lax.broadcasted_iota(I32, (8, c), 0)
    acc = cur * w[3:4, :] + b
    for k in (1, 2, 3):
        cr = pltpu.roll(cur, k, axis=0)
        tr = pltpu.roll(tail, k, axis=0)
        head = jnp.where(row8 < k, tr, cr[:8])
        sh = jnp.concatenate([head, cr[8:]], axis=0)
        acc = acc + sh * w[3 - k:4 - k, :]
    return acc


def _norm_matmul_kernel(x_ref, g_ref, w_ref, o_ref, xn_ref):
    @pl.when(pl.program_id(1) == 0)
    def _():
        x = x_ref[...]
        y = x * lax.rsqrt(jnp.mean(x * x, axis=-1, keepdims=True) + NORM_EPS) * g_ref[...]
        xn_ref[...] = y.astype(BF16)

    o_ref[...] = jnp.dot(xn_ref[...], w_ref[...], preferred_element_type=F32)


def _norm_matmul(x, g, w, tm, tn):
    m, k = x.shape
    n = w.shape[1]
    return pl.pallas_call(
        _norm_matmul_kernel,
        grid=(m // tm, n // tn),
        in_specs=[pl.BlockSpec((tm, k), lambda i, j: (i, 0)),
                  pl.BlockSpec((1, k), lambda i, j: (0, 0)),
                  pl.BlockSpec((k, tn), lambda i, j: (0, j))],
        out_specs=pl.BlockSpec((tm, tn), lambda i, j: (i, j)),
        out_shape=jax.ShapeDtypeStruct((m, n), F32),
        scratch_shapes=[pltpu.VMEM((tm, k), BF16)],
        compiler_params=_cparams(("arbitrary", "arbitrary")),
    )(x, g, w)


SSD_CHUNK = 128


def _ssd_kernel(z_ref, x_ref, b_ref, c_ref, dt_ref, cwx_ref, cbx_ref, cwb_ref, cbb_ref, cwc_ref, cbc_ref,
                dtb_ref, alog_ref, d_ref, nw_ref, o_ref, tx_ref, tb_ref, tc_ref, st_ref):
    t = SSD_CHUNK

    @pl.when(pl.program_id(1) == 0)
    def _():
        tx_ref[...] = jnp.zeros_like(tx_ref)
        tb_ref[...] = jnp.zeros_like(tb_ref)
        tc_ref[...] = jnp.zeros_like(tc_ref)
        st_ref[...] = jnp.zeros_like(st_ref)

    xr, br, cr = x_ref[...], b_ref[...], c_ref[...]
    xs = _silu(_conv4(xr, tx_ref[...], cwx_ref[...], cbx_ref[...]))
    bm = _silu(_conv4(br, tb_ref[...], cwb_ref[...], cbb_ref[...]))
    cm = _silu(_conv4(cr, tc_ref[...], cwc_ref[...], cbc_ref[...]))
    tx_ref[...] = xr[t - 8:]
    tb_ref[...] = br[t - 8:]
    tc_ref[...] = cr[t - 8:]

    dt = _softplus(dt_ref[...] + dtb_ref[...])
    a = dt * (-jnp.exp(alog_ref[...]))
    ri = lax.broadcasted_iota(I32, (t, t), 0)
    ci = lax.broadcasted_iota(I32, (t, t), 1)
    tril = ri >= ci
    acs = _dot_exact_lhs(jnp.where(tril, 1.0, 0.0).astype(BF16), a)
    acs_t = acs.T
    d_row = d_ref[...]
    ys = []
    for h in range(N_HEADS):
        g = h // 2
        lo, hi = HEAD_DIM * h, HEAD_DIM * (h + 1)
        b_g = bm[:, HEAD_DIM * g:HEAD_DIM * (g + 1)]
        c_g = cm[:, HEAD_DIM * g:HEAD_DIM * (g + 1)]
        x_h = xs[:, lo:hi]
        acs_c = acs[:, h:h + 1]
        lmat = jnp.exp(jnp.where(tril, acs_c - acs_t[h:h + 1, :], NEG_BIG))
        scores = _dot_nt(c_g, b_g) * lmat
        xdt = x_h * dt[:, h:h + 1]
        st = st_ref[h]
        y = _dot(scores, xdt) + _dot(c_g, st) * jnp.exp(acs_c) + x_h * d_row[:, lo:hi]
        a_last = acs[t - 1:t, h:h + 1]
        bw = b_g * jnp.exp(a_last - acs_c)
        st_ref[h] = st * jnp.exp(a_last) + _dot_tn(bw, xdt)
        ys.append(y)
    y = jnp.concatenate(ys, axis=1) * _silu(z_ref[...])
    outs = []
    for g in range(2):
        yg = y[:, 128 * g:128 * (g + 1)]
        outs.append(yg * lax.rsqrt(jnp.mean(yg * yg, axis=-1, keepdims=True) + NORM_EPS))
    o_ref[...] = jnp.concatenate(outs, axis=1) * nw_ref[...]


def _ssd_mixer(cols, conv_w, conv_b, dt_bias, a_log, d_skip, norm_w):
    bsz, s, _ = cols.shape
    t = SSD_CHUNK
    row = lambda v, n: jnp.pad(v.reshape(1, -1), ((0, 0), (0, n - v.size)))
    full = lambda shp: pl.BlockSpec(shp, lambda b, c: (0,) * len(shp))
    colspec = lambda w, name: pl.BlockSpec((None, t, w), lambda b, c: (b, c, COL[name] // w))
    return pl.pallas_call(
        _ssd_kernel,
        grid=(bsz, s // t),
        in_specs=[colspec(256, "ssd_z"), colspec(256, "ssd_x"), colspec(128, "ssd_b"), colspec(128, "ssd_c"),
                  colspec(128, "ssd_dt"),
                  full((4, 256)), full((1, 256)), full((4, 128)), full((1, 128)), full((4, 128)), full((1, 128)),
                  full((1, 128)), full((1, 128)), full((1, 256)), full((1, 256))],
        out_specs=pl.BlockSpec((None, t, 256), lambda b, c: (b, c, 0)),
        out_shape=jax.ShapeDtypeStruct((bsz, s, 256), F32),
        scratch_shapes=[pltpu.VMEM((8, 256), F32), pltpu.VMEM((8, 128), F32), pltpu.VMEM((8, 128), F32),
                        pltpu.VMEM((N_HEADS, HEAD_DIM, HEAD_DIM), F32)],
        compiler_params=_cparams(("arbitrary", "arbitrary")),
    )(cols, cols, cols, cols, cols,
      conv_w[:, :256], conv_b[:256].reshape(1, 256), conv_w[:, 256:384], conv_b[256:384].reshape(1, 128),
      conv_w[:, 384:512], conv_b[384:512].reshape(1, 128),
      row(dt_bias, 128), row(a_log, 128), jnp.repeat(d_skip, HEAD_DIM).reshape(1, 256), norm_w.reshape(1, 256))


LRU_TILE = 256


def _lru_kernel(gate_ref, xr_ref, cw_ref, cb_ref, wa_ref, ba_ref, wi_ref, bi_ref, lam_ref, o_ref, tail_ref, h_ref):
    t = LRU_TILE

    @pl.when(pl.program_id(1) == 0)
    def _():
        tail_ref[...] = jnp.zeros_like(tail_ref)
        h_ref[...] = jnp.zeros_like(h_ref)

    xr = xr_ref[...]
    xc = _conv4(xr, tail_ref[...], cw_ref[...], cb_ref[...])
    tail_ref[...] = xr[t - 8:]
    r = _sigmoid(_dot(xc, wa_ref[...]) + ba_ref[...])
    i = _sigmoid(_dot(xc, wi_ref[...]) + bi_ref[...])
    log_a = LRU_C * r * (-_softplus(-lam_ref[...]))
    a = jnp.exp(log_a)
    u = jnp.sqrt(-jnp.tanh(log_a) * (a * a + 1.0)) * (i * xc)
    rows = lax.broadcasted_iota(I32, (t, GROUP_WIDTH), 0)
    d = 1
    while d < t:
        a_sh = pltpu.roll(a, d, axis=0)
        u_sh = pltpu.roll(u, d, axis=0)
        m = rows >= d
        u = jnp.where(m, a * u_sh + u, u)
        a = jnp.where(m, a * a_sh, a)
        d *= 2
    h = a * h_ref[7:8, :] + u
    h_ref[...] = h[t - 8:]
    o_ref[...] = h * _gelu_tanh(gate_ref[...])


def _lru_mixer(cols, conv_w, conv_b, wa, ba, wi, bi, lam):
    bsz, s, _ = cols.shape
    t = LRU_TILE
    cb = COL["lru_gate"] // 256
    bd = lambda w: jax.scipy.linalg.block_diag(*[w[k] for k in range(w.shape[0])]).astype(BF16)
    full = lambda shp: pl.BlockSpec(shp, lambda b, c: (0,) * len(shp))
    return pl.pallas_call(
        _lru_kernel,
        grid=(bsz, s // t),
        in_specs=[pl.BlockSpec((None, t, 256), lambda b, c: (b, c, cb)),
                  pl.BlockSpec((None, t, 256), lambda b, c: (b, c, cb + 1)),
                  full((4, 256)), full((1, 256)), full((256, 256)), full((1, 256)), full((256, 256)), full((1, 256)),
                  full((1, 256))],
        out_specs=pl.BlockSpec((None, t, 256), lambda b, c: (b, c, 0)),
        out_shape=jax.ShapeDtypeStruct((bsz, s, 256), F32),
        scratch_shapes=[pltpu.VMEM((8, 256), F32), pltpu.VMEM((8, 256), F32)],
        compiler_params=_cparams(("arbitrary", "arbitrary")),
    )(cols, cols, conv_w, conv_b.reshape(1, 256), bd(wa), ba.reshape(1, 256), bd(wi), bi.reshape(1, 256),
      lam.reshape(1, 256))


RW_TILE = 128
RWKV_GN_EPS = 64e-5


def _segsum_bc(x):
    r = x.shape[0]
    m = lax.broadcasted_iota(I32, (r, 128), 1) < HEAD_DIM
    outs = []
    for j in range(2):
        xj = x[:, 128 * j:128 * (j + 1)]
        lo = jnp.sum(jnp.where(m, xj, 0.0), axis=1, keepdims=True)
        hi = jnp.sum(jnp.where(m, 0.0, xj), axis=1, keepdims=True)
        outs.append(jnp.where(m, lo, hi))
    return jnp.concatenate(outs, axis=1)


def _rwkv_kernel(c_ref, mu_ref, w0_ref, w2_ref, a0_ref, a2_ref, g2_ref, kkp_ref, ka_ref, rk_ref, lnw_ref, lnb_ref, e_ref,
                 o_ref, prev_ref, s_ref, r_s, w_s, k_s, v_s, kk_s, b_s, g_s, bon_s, y_s):
    nb = c_ref.shape[0]
    t = RW_TILE

    @pl.when(pl.program_id(0) == 0)
    def _():
        prev_ref[...] = jnp.zeros_like(prev_ref)
        s_ref[...] = jnp.zeros_like(s_ref)
        r_s[...] = jnp.zeros_like(r_s)

    row = lax.broadcasted_iota(I32, (t, 1024), 0)
    for b in range(nb):
        c = c_ref[b]
        pr = jnp.where(row == 0, prev_ref[b, 7:8, :], pltpu.roll(c, 1, axis=0))
        prev_ref[b] = c[t - 8:]
        c = c + (pr - c) * mu_ref[...]
        r, k, v = c[:, 0:256], c[:, 256:512], c[:, 512:768]
        wd, ad, gd = c[:, 768:832], c[:, 832:896], c[:, 896:1024]
        w_log = -_softplus(-(w0_ref[...] + _dot(jnp.tanh(wd), w2_ref[...]))) - 0.5
        a = _sigmoid(a0_ref[...] + _dot(ad, a2_ref[...]))
        kk = k * kkp_ref[...]
        kk = kk / jnp.maximum(jnp.sqrt(_segsum_bc(kk * kk)), 1e-12)
        k2 = k * (1.0 + (a - 1.0) * ka_ref[...])
        r_s[b, 8:, :] = r
        w_s[b] = jnp.exp(-jnp.exp(w_log))
        k_s[b] = k2
        v_s[b] = v
        kk_s[b] = kk
        b_s[b] = kk * a
        g_s[b] = _dot(_sigmoid(gd), g2_ref[...])
        bon_s[b] = _segsum_bc(r * k2 * rk_ref[...]) * v

    nr = nb * HEAD_DIM
    eye4 = jnp.where(lax.broadcasted_iota(I32, (nr, 256), 0) % HEAD_DIM
                     == lax.broadcasted_iota(I32, (nr, 256), 1) % HEAD_DIM, 1.0, 0.0)

    def seg(*xs):
        parts = []
        for x in xs:
            parts.extend(_split2(x))
        out = jnp.dot(jnp.concatenate(parts, axis=0), e_ref[...], preferred_element_type=F32)
        return [out[2 * i * nr:(2 * i + 1) * nr] + out[(2 * i + 1) * nr:(2 * i + 2) * nr] for i in range(len(xs))]

    def rows(ref, i):
        return jnp.concatenate([jnp.broadcast_to(ref[b, pl.ds(i, 1), :], (HEAD_DIM, 256)) for b in range(nb)], axis=0)

    def put_y(ybc, i):
        yd = ybc * eye4
        for b in range(nb):
            y_s[b, pl.ds(i, 1), :] = jnp.sum(yd[b * HEAD_DIM:(b + 1) * HEAD_DIM], axis=0, keepdims=True)

    def step(tt, carry):
        s = s_ref[...]
        sa, vbc, ybc = seg(s * rows(kk_s, tt), eye4 * rows(v_s, tt), s * rows(r_s, tt + 7))
        put_y(ybc, tt + 7)
        s_ref[...] = s * rows(w_s, tt) - sa * rows(b_s, tt) + vbc * rows(k_s, tt)
        return carry

    lax.fori_loop(0, t, step, 0)
    ybc, = seg(s_ref[...] * rows(r_s, t + 7))
    put_y(ybc, t + 7)

    for b in range(nb):
        y = y_s[b, 8:, :]
        d = y - _segsum_bc(y) * (1.0 / HEAD_DIM)
        var = _segsum_bc(d * d) * (1.0 / HEAD_DIM)
        y = d * lax.rsqrt(var + RWKV_GN_EPS) * lnw_ref[...] + lnb_ref[...] + bon_s[b]
        o_ref[b] = y * g_s[b]


def _rwkv_mixer(cols, mu, w0, w2, a0, a2, g2, k_k, k_a, r_k, ln_w, ln_b):
    bsz, s, _ = cols.shape
    t = RW_TILE
    full = lambda shp: pl.BlockSpec(shp, lambda c: (0,) * len(shp))
    r1 = lambda v: v.reshape(1, -1)
    tile = lambda: pltpu.VMEM((bsz, t, 256), F32)
    return pl.pallas_call(
        _rwkv_kernel,
        grid=(s // t,),
        in_specs=[pl.BlockSpec((bsz, t, 1024), lambda c: (0, c, COL["rw_r"] // 1024)),
                  full((1, 1024)), full((1, 256)), full((64, 256)), full((1, 256)), full((64, 256)), full((128, 256)),
                  full((1, 256)), full((1, 256)), full((1, 256)), full((1, 256)), full((1, 256)), full((256, 256))],
        out_specs=pl.BlockSpec((bsz, t, 256), lambda c: (0, c, 0)),
        out_shape=jax.ShapeDtypeStruct((bsz, s, 256), F32),
        scratch_shapes=([pltpu.VMEM((bsz, 8, 1024), F32), pltpu.VMEM((bsz * HEAD_DIM, 256), F32),
                         pltpu.VMEM((bsz, t + 8, 256), F32)] + [tile() for _ in range(7)]
                        + [pltpu.VMEM((bsz, t + 8, 256), F32)]),
        compiler_params=_cparams(("arbitrary",)),
    )(cols, r1(mu), r1(w0), w2, r1(a0), a2, g2, r1(k_k), r1(k_a), r1(r_k), r1(ln_w), r1(ln_b),
      jnp.asarray(np.kron(np.eye(N_HEADS), np.ones((HEAD_DIM, HEAD_DIM))), BF16))


NSA_STRIDE = 16
NSA_SEL_BLOCK = 64
NSA_TOPN = 16
NSA_WINDOW = 512
NSA_FORCE = 1e4
NSA_TQ = 256
NSA_TK = 512
NSA_SLOPES = tuple(2.0 ** (-8.0 * (h + 1) / N_HEADS) for h in range(N_HEADS))
SEL_OFF = -float(2 ** 30)


def _dot3_nt(a, b):
    ah, al = _split2(a)
    bh, bl = _split2(b)
    d = lambda x, y: lax.dot_general(x, y, (((1,), (1,)), ((), ())), preferred_element_type=F32)
    return d(ah, bh) + d(ah, bl) + d(al, bh)


def _nsa_compress_kernel(kv_ref, pos_ref, w1_ref, w2_ref, o_ref):
    nbp = o_ref.shape[0]
    outs = []
    for c in range(2):
        acc_a = jnp.zeros((nbp, 256), F32)
        acc_b = jnp.zeros((nbp, 256), F32)
        for r in range(NSA_STRIDE):
            xr = kv_ref[pl.ds(r, nbp, stride=NSA_STRIDE), :][:, HEAD_DIM * c:HEAD_DIM * (c + 1)]
            acc_a = acc_a + _dot(xr + pos_ref[c, r:r + 1, :], w1_ref[c, 64 * r:64 * (r + 1), :])
            acc_b = acc_b + _dot(xr + pos_ref[c, 16 + r:17 + r, :], w1_ref[c, 1024 + 64 * r:1024 + 64 * (r + 1), :])
        hdn = _gelu_tanh(acc_a + pltpu.roll(acc_b, nbp - 1, axis=0))
        outs.append(_dot(hdn, w2_ref[c]))
    o_ref[...] = jnp.concatenate(outs, axis=1)


def _nsa_cmp_kernel(q_ref, kv_ref, ov_ref, ocmp_ref, qaug_ref):
    tq = NSA_TQ
    nbp = kv_ref.shape[0]
    q0 = pl.program_id(1) * tq
    kv = kv_ref[...]
    kc, vc = kv[:, :HEAD_DIM], kv[:, HEAD_DIM:]
    pos = q0 + lax.broadcasted_iota(I32, (tq, nbp), 0)
    blk_end = lax.broadcasted_iota(I32, (tq, nbp), 1) * NSA_STRIDE + (2 * NSA_STRIDE - 1)
    dist = pos - blk_end
    mask = dist >= 0
    distf = dist.astype(F32)
    q = q_ref[...] * (HEAD_DIM ** -0.5)
    lane = lax.broadcasted_iota(I32, (tq, 128), 1)
    psum = jnp.zeros((tq, nbp), F32)
    for h in range(N_HEADS):
        qh = q[:, HEAD_DIM * h:HEAD_DIM * (h + 1)]
        s = jnp.where(mask, _dot3_nt(qh, kc) - NSA_SLOPES[h] * distf, -jnp.inf)
        m = jnp.max(s, axis=1, keepdims=True)
        m = jnp.where(m == -jnp.inf, 0.0, m)
        e = jnp.where(mask, jnp.exp(s - m), 0.0)
        p = e / jnp.maximum(jnp.sum(e, axis=1, keepdims=True), 1e-30)
        ocmp_ref[h] = _dot(p, vc)
        psum = psum + p
        qt = q[:, 128 * (h // 2):128 * (h // 2 + 1)]
        keep = (lane < HEAD_DIM) if h % 2 == 0 else (lane >= HEAD_DIM)
        qaug_ref[h, :, 0:128] = jnp.where(keep, qt, 0.0).astype(BF16)
    imp_t = _dot_exact_rhs(psum, ov_ref[...]).T
    j = lax.broadcasted_iota(I32, (128, tq), 0)
    pos_t = q0 + lax.broadcasted_iota(I32, (128, tq), 1)
    cur = pos_t // NSA_SEL_BLOCK
    forced = (j == 0) | (j == cur) | (j == cur - 1)
    valid = j * NSA_SEL_BLOCK <= pos_t
    x = jnp.where(valid, jnp.where(forced, NSA_FORCE, imp_t), -jnp.inf)
    jf = j.astype(F32)
    picked = jnp.zeros((128, tq), F32)
    for _ in range(NSA_TOPN):
        mx = jnp.max(x, axis=0, keepdims=True)
        first = jnp.min(jnp.where(x == mx, jf, 128.0), axis=0, keepdims=True)
        hit = jf == first
        picked = jnp.where(hit, 1.0, picked)
        x = jnp.where(hit, -jnp.inf, x)
    bias = jnp.where(valid, jnp.where(picked > 0.0, 0.0, SEL_OFF), SEL_OFF).T.astype(BF16)
    for h in range(N_HEADS):
        qaug_ref[h, :, 128:256] = bias


def _nsa_flash_kernel(qaug_ref, kv_ref, o_ref, kaug_s, v_s, *, window):
    tq, tk = NSA_TQ, NSA_TK
    s_len = kv_ref.shape[0]
    tk = min(tk, s_len)
    kdim = 128 if window else 256
    qi = pl.program_id(1)
    q0 = qi * tq

    @pl.when(qi == 0)
    def _():
        def build(i, c):
            r0 = pl.multiple_of(i * tk, tk)
            kv = kv_ref[pl.ds(r0, tk), :]
            k = kv[:, :HEAD_DIM]
            parts = [k, k]
            if not window:
                blk = (r0 + lax.broadcasted_iota(I32, (tk, 128), 0)) // NSA_SEL_BLOCK
                parts.append(jnp.where(blk == lax.broadcasted_iota(I32, (tk, 128), 1), 1.0, 0.0))
            kaug_s[pl.ds(r0, tk), :] = jnp.concatenate(parts, axis=1).astype(BF16)
            v_s[pl.ds(r0, tk), :] = kv[:, HEAD_DIM:].astype(BF16)
            return c
        lax.fori_loop(0, s_len // tk, build, 0)

    q = qaug_ref[...][:, :, :kdim].reshape(N_HEADS * tq, kdim)
    row = lax.broadcasted_iota(I32, (N_HEADS * tq, tk), 0)
    t_q = q0 + (row & (tq - 1))
    head = lax.broadcasted_iota(I32, (N_HEADS * tq, 1), 0) // tq
    slope = jnp.where(head == 0, NSA_SLOPES[0], jnp.where(head == 1, NSA_SLOPES[1],
                      jnp.where(head == 2, NSA_SLOPES[2], NSA_SLOPES[3])))
    col = lax.broadcasted_iota(I32, (N_HEADS * tq, tk), 1)

    def body(kt, carry):
        m, l, acc = carry
        k0 = pl.multiple_of(kt * tk, tk)
        s = lax.dot_general(q, kaug_s[pl.ds(k0, tk), :], (((1,), (1,)), ((), ())), preferred_element_type=F32)
        t_k = k0 + col
        s = s + slope * (t_k - q0).astype(F32)
        ok = t_k <= t_q
        if window:
            ok = ok & (t_q - t_k < NSA_WINDOW)
        m_new = jnp.maximum(m, jnp.max(jnp.where(ok, s, NEG_BIG), axis=1, keepdims=True))
        p = jnp.where(ok, jnp.exp(s - m_new), 0.0)
        alpha = jnp.exp(m - m_new)
        l = alpha * l + jnp.sum(p, axis=1, keepdims=True)
        acc = alpha * acc + jnp.dot(p.astype(BF16), v_s[pl.ds(k0, tk), :], preferred_element_type=F32)
        return m_new, l, acc

    lo = jnp.maximum(q0 - NSA_WINDOW, 0) // tk if window else 0
    hi = (q0 + tq - 1) // tk + 1
    init = (jnp.full((N_HEADS * tq, 1), NEG_BIG, F32), jnp.zeros((N_HEADS * tq, 1), F32),
            jnp.zeros((N_HEADS * tq, HEAD_DIM), F32))
    m, l, acc = lax.fori_loop(lo, hi, body, init)
    o_ref[...] = (acc / l).reshape(N_HEADS, tq, HEAD_DIM)


def _nsa_overlap(s):
    nbp, nsel = s // NSA_STRIDE, s // NSA_SEL_BLOCK
    cs = np.arange(nbp) * NSA_STRIDE
    ss = np.arange(128) * NSA_SEL_BLOCK
    ov = np.minimum(cs[:, None] + 2 * NSA_STRIDE, ss[None, :] + NSA_SEL_BLOCK) - np.maximum(cs[:, None], ss[None, :])
    ov = np.clip(ov, 0, None) / (2 * NSA_STRIDE)
    ov[nbp - 1:, :] = 0.0
    ov[:, nsel:] = 0.0
    return jnp.asarray(ov, BF16)


def _nsa_branches(cols, cmp_pos, cmp_w1, cmp_w2):
    bsz, s, _ = cols.shape
    nbp = s // NSA_STRIDE
    tq = NSA_TQ
    kvcmp = pl.pallas_call(
        _nsa_compress_kernel,
        grid=(bsz,),
        in_specs=[pl.BlockSpec((None, s, 128), lambda b: (b, 0, COL["nsa_kvc"] // 128)),
                  pl.BlockSpec((2, 32, 64), lambda b: (0, 0, 0)),
                  pl.BlockSpec((2, 2048, 256), lambda b: (0, 0, 0)),
                  pl.BlockSpec((2, 256, 64), lambda b: (0, 0, 0))],
        out_specs=pl.BlockSpec((None, nbp, 128), lambda b: (b, 0, 0)),
        out_shape=jax.ShapeDtypeStruct((bsz, nbp, 128), F32),
        compiler_params=_cparams(("arbitrary",), 48),
    )(cols, cmp_pos, cmp_w1.astype(BF16), cmp_w2)
    head_out = lambda dt, w: (pl.BlockSpec((None, N_HEADS, tq, w), lambda b, i: (b, 0, i, 0)),
                              jax.ShapeDtypeStruct((bsz, N_HEADS, s, w), dt))
    (oc_spec, oc_shape), (qa_spec, qa_shape) = head_out(F32, HEAD_DIM), head_out(BF16, 256)
    o_cmp, qaug = pl.pallas_call(
        _nsa_cmp_kernel,
        grid=(bsz, s // tq),
        in_specs=[pl.BlockSpec((None, tq, 256), lambda b, i: (b, i, COL["nsa_q"] // 256)),
                  pl.BlockSpec((None, nbp, 128), lambda b, i: (b, 0, 0)),
                  pl.BlockSpec((nbp, 128), lambda b, i: (0, 0))],
        out_specs=[oc_spec, qa_spec],
        out_shape=[oc_shape, qa_shape],
        compiler_params=_cparams(("arbitrary", "arbitrary"), 48),
    )(cols, kvcmp, _nsa_overlap(s))

    def flash(name, window):
        return pl.pallas_call(
            functools.partial(_nsa_flash_kernel, window=window),
            grid=(bsz, s // tq),
            in_specs=[qa_spec, pl.BlockSpec((None, s, 128), lambda b, i: (b, 0, COL[name] // 128))],
            out_specs=oc_spec,
            out_shape=oc_shape,
            scratch_shapes=[pltpu.VMEM((s, 128 if window else 256), BF16), pltpu.VMEM((s, HEAD_DIM), BF16)],
            compiler_params=_cparams(("arbitrary", "arbitrary"), 48),
        )(qaug, cols)

    return o_cmp, flash("nsa_kvs", False), flash("nsa_kvw", True)


def _nsa_mixer(cols, cmp_pos, cmp_w1, cmp_w2):
    o_cmp, o_sel, o_win = _nsa_branches(cols, cmp_pos, cmp_w1, cmp_w2)
    bsz, s, _ = cols.shape
    g = jax.nn.sigmoid(cols[..., COL["nsa_gate"]:COL["nsa_gate"] + 12]).reshape(bsz, s, N_HEADS, 3)
    tr = lambda o: jnp.moveaxis(o, 1, 2)
    out = g[..., 0:1] * tr(o_cmp) + g[..., 1:2] * tr(o_sel) + g[..., 2:3] * tr(o_win)
    return out.reshape(bsz, s, 256)


MIX_TILE = 256


def _mix_out_kernel(h_ref, ssm_ref, lru_ref, rw_ref, oc_ref, os_ref, ow_ref, gate_ref, w_ref, o_ref):
    acc = h_ref[...]
    acc = acc + _dot(ssm_ref[...], w_ref[0:256, :])
    acc = acc + _dot(lru_ref[...], w_ref[256:512, :])
    acc = acc + _dot(rw_ref[...], w_ref[512:768, :])
    g = _sigmoid(gate_ref[...])
    for h in range(N_HEADS):
        y = (g[:, 3 * h:3 * h + 1] * oc_ref[h] + g[:, 3 * h + 1:3 * h + 2] * os_ref[h]
             + g[:, 3 * h + 2:3 * h + 3] * ow_ref[h])
        acc = acc + _dot(y, w_ref[768 + HEAD_DIM * h:768 + HEAD_DIM * (h + 1), :])
    o_ref[...] = acc


def _mix_out(h, y_ssm, y_lru, y_rw, o_cmp, o_sel, o_win, cols, w_out):
    bsz, s, d = h.shape
    t = MIX_TILE
    tok = lambda w: pl.BlockSpec((None, t, w), lambda b, i: (b, i, 0))
    hd = pl.BlockSpec((None, N_HEADS, t, HEAD_DIM), lambda b, i: (b, 0, i, 0))
    return pl.pallas_call(
        _mix_out_kernel,
        grid=(bsz, s // t),
        in_specs=[tok(d), tok(256), tok(256), tok(256), hd, hd, hd,
                  pl.BlockSpec((None, t, 128), lambda b, i: (b, i, COL["nsa_gate"] // 128)),
                  pl.BlockSpec((4 * GROUP_WIDTH, d), lambda b, i: (0, 0))],
        out_specs=tok(d),
        out_shape=jax.ShapeDtypeStruct((bsz, s, d), F32),
        compiler_params=_cparams(("arbitrary", "arbitrary")),
    )(h, y_ssm, y_lru, y_rw, o_cmp, o_sel, o_win, cols, w_out.astype(BF16))


PEER_HEADS = 8
PEER_NKEYS = 128
PEER_TOPK = 16
PEER_TT = 128
PEER_TQ = 256
PEER_ACT_UNROLL = 16
PEER_OUT_UNROLL = 128


def _norm_matmul2_kernel(x_ref, g_ref, w_ref, o_ref, xn_out_ref, xn_ref):
    @pl.when(pl.program_id(1) == 0)
    def _():
        x = x_ref[...]
        y = x * lax.rsqrt(jnp.mean(x * x, axis=-1, keepdims=True) + NORM_EPS) * g_ref[...]
        xn_out_ref[...] = y
        xn_ref[...] = y.astype(BF16)

    o_ref[...] = jnp.dot(xn_ref[...], w_ref[...], preferred_element_type=F32)


def _norm_matmul2(x, g, w, tm, tn):
    m, k = x.shape
    n = w.shape[1]
    return pl.pallas_call(
        _norm_matmul2_kernel,
        grid=(m // tm, n // tn),
        in_specs=[pl.BlockSpec((tm, k), lambda i, j: (i, 0)),
                  pl.BlockSpec((1, k), lambda i, j: (0, 0)),
                  pl.BlockSpec((k, tn), lambda i, j: (0, j))],
        out_specs=[pl.BlockSpec((tm, tn), lambda i, j: (i, j)), pl.BlockSpec((tm, k), lambda i, j: (i, 0))],
        out_shape=[jax.ShapeDtypeStruct((m, n), F32), jax.ShapeDtypeStruct((m, k), F32)],
        scratch_shapes=[pltpu.VMEM((tm, k), BF16)],
        compiler_params=_cparams(("arbitrary", "arbitrary")),
    )(x, g, w)


def _top_rows(x, n, payload=None):
    r = x.shape[0]
    rid = lax.broadcasted_iota(I32, x.shape, 0).astype(F32)
    vals, ids, pays = [], [], []
    for _ in range(n):
        mx = jnp.max(x, axis=0, keepdims=True)
        first = jnp.min(jnp.where(x == mx, rid, float(r)), axis=0, keepdims=True)
        hit = rid == first
        vals.append(mx)
        ids.append(first)
        if payload is not None:
            pays.append(jnp.sum(jnp.where(hit, payload, 0.0), axis=0, keepdims=True))
        x = jnp.where(hit, -jnp.inf, x)
    cat = lambda v: jnp.concatenate(v, axis=0)
    return (cat(vals), cat(ids)) + ((cat(pays),) if payload is not None else ())


def _peer_route_kernel(q_ref, keys_ref, eidx_ref, gate_ref):
    k = PEER_TOPK
    tq = q_ref.shape[0]
    gates, eids = [], []
    for h in range(PEER_HEADS):
        s, i = [], []
        for c in range(2):
            qh = q_ref[:, 128 * (2 * h + c):128 * (2 * h + c + 1)]
            sv, si = _top_rows(_dot3_nt(keys_ref[h, c], qh), k)
            s.append(sv)
            i.append(si)
        cand, cidx = [], []
        for a in range(k):
            nb = k // (a + 1)
            cand.append(s[0][a:a + 1, :] + s[1][:nb])
            cidx.append(i[0][a:a + 1, :] * float(PEER_NKEYS) + i[1][:nb])
        n_c = sum(c.shape[0] for c in cand)
        pad = (-n_c) % 8
        cand.append(jnp.full((pad, tq), -jnp.inf, F32))
        cidx.append(jnp.zeros((pad, tq), F32))
        top, _, eidx = _top_rows(jnp.concatenate(cand, axis=0), k, payload=jnp.concatenate(cidx, axis=0))
        e = jnp.exp(top - top[0:1, :])
        gates.append(e / jnp.sum(e, axis=0, keepdims=True))
        eids.append(eidx)
    gate_ref[...] = jnp.concatenate(gates, axis=0).T
    eidx_ref[...] = jnp.concatenate(eids, axis=0).T.astype(I32)


def _peer_route(q, keys):
    t = q.shape[0]
    tq = min(PEER_TQ, t)
    nk = PEER_HEADS * PEER_TOPK
    return pl.pallas_call(
        _peer_route_kernel,
        grid=(t // tq,),
        in_specs=[pl.BlockSpec((tq, q.shape[1]), lambda i: (i, 0)),
                  pl.BlockSpec(keys.shape, lambda i: (0, 0, 0, 0))],
        out_specs=[pl.BlockSpec((tq, nk), lambda i: (i, 0)), pl.BlockSpec((tq, nk), lambda i: (i, 0))],
        out_shape=[jax.ShapeDtypeStruct((t, nk), I32), jax.ShapeDtypeStruct((t, nk), F32)],
        compiler_params=_cparams(("arbitrary",), 48),
    )(q, keys)


def _pack_table(tab):
    e, d = tab.shape
    bits = lax.bitcast_convert_type(tab.astype(BF16), jnp.uint16).astype(jnp.uint32).reshape(e, 2, d // 2)
    word = (bits[:, 1, :] << 16) | bits[:, 0, :]
    return lax.bitcast_convert_type(word, I32).reshape(e, d // 256, 128)


def _expert_halves(tbl_ref, e):
    word = tbl_ref[e]
    return pltpu.bitcast(word << 16, F32), pltpu.bitcast(word & jnp.int32(-65536), F32)


_BITREV8 = (0, 4, 2, 6, 1, 5, 3, 7)


def _rowsum8(ps):
    sub = lax.broadcasted_iota(I32, (8, 128), 0)
    q = [ps[_BITREV8[p]] for p in range(8)]
    cur = [jnp.concatenate([q[2 * i], q[2 * i + 1]], axis=0) for i in range(4)]
    for h in (2, 1):
        m = (sub & h) == 0
        cur = [jnp.where(m, x, y) + jnp.where(m, pltpu.roll(x, 8 - h, axis=0), pltpu.roll(y, h, axis=0))
               for x, y in zip(cur[0::2], cur[1::2])]
    return cur[0]


def _peer_act_kernel(idx_ref, x_ref, gate_ref, tbl_ref, o_ref, q_ref):
    tt, nk = gate_ref.shape
    lane = lax.broadcasted_iota(I32, (nk, tt), 1)

    def tok(t, acc):
        x = x_ref[t]
        xlo, xhi = x[0:4], x[4:8]

        def grp(g, c):
            for half in range(PEER_ACT_UNROLL):
                k0 = pl.multiple_of(g * (8 * PEER_ACT_UNROLL) + 8 * half, 8)
                ps = []
                for j in range(8):
                    lo, hi = _expert_halves(tbl_ref, idx_ref[t * nk + k0 + j])
                    ps.append(lo * xlo + hi * xhi)
                q_ref[pl.ds(k0, 8), :] = _rowsum8(ps)
            return c

        lax.fori_loop(0, nk // (8 * PEER_ACT_UNROLL), grp, 0)
        s = jnp.sum(q_ref[...], axis=1, keepdims=True)
        return jnp.where(lane == t, s, acc)

    acc = lax.fori_loop(0, tt, tok, jnp.zeros((nk, tt), F32))
    o_ref[...] = gate_ref[...] * _gelu_tanh(acc.T)


def _peer_out_kernel(idx_ref, w_ref, tbl_ref, h_ref, o_ref):
    tt = h_ref.shape[0]
    nk = idx_ref.shape[0] // tt

    def tok(t, c):
        def grp(g, accs):
            accs = list(accs)
            k0 = t * nk + g * PEER_OUT_UNROLL
            for j in range(PEER_OUT_UNROLL):
                w = w_ref[k0 + j]
                vlo, vhi = _expert_halves(tbl_ref, idx_ref[k0 + j])
                a = 2 * (j % 2)
                accs[a] = accs[a] + vlo * w
                accs[a + 1] = accs[a + 1] + vhi * w
            return tuple(accs)

        z = jnp.zeros((4, 128), F32)
        l0, h0, l1, h1 = lax.fori_loop(0, nk // PEER_OUT_UNROLL, grp, (z, z, z, z))
        o_ref[t] = h_ref[t] + jnp.concatenate([l0 + l1, h0 + h1], axis=0)
        return c

    lax.fori_loop(0, tt, tok, 0)


def _peer_experts(eidx_t, gate_t, xn3, h3, tbl_u, tbl_v):
    t, nk = eidx_t.shape
    tt = min(PEER_TT, t)
    smem = lambda: pl.BlockSpec((tt * nk,), lambda i: (i,), memory_space=pltpu.SMEM)
    vm = lambda: pl.BlockSpec((tt, nk), lambda i: (i, 0))
    tok = lambda: pl.BlockSpec((tt, 8, 128), lambda i: (i, 0, 0))
    tbl = lambda: pl.BlockSpec(tbl_u.shape, lambda i: (0, 0, 0), pipeline_mode=pl.Buffered(1))
    eidx_flat = eidx_t.reshape(t * nk)
    w_t = pl.pallas_call(
        _peer_act_kernel,
        grid=(t // tt,),
        in_specs=[smem(), tok(), vm(), tbl()],
        out_specs=vm(),
        out_shape=jax.ShapeDtypeStruct((t, nk), F32),
        scratch_shapes=[pltpu.VMEM((nk, 128), F32)],
        compiler_params=_cparams(("arbitrary",), 48),
    )(eidx_flat, xn3, gate_t, tbl_u)
    return pl.pallas_call(
        _peer_out_kernel,
        grid=(t // tt,),
        in_specs=[smem(), smem(), tbl(), tok()],
        out_specs=tok(),
        out_shape=jax.ShapeDtypeStruct(h3.shape, F32),
        compiler_params=_cparams(("arbitrary",), 48),
    )(eidx_flat, w_t.reshape(t * nk), tbl_v, h3)


def _peer_block(h, g, wq, keys, u, v):
    t, d = h.shape
    q, xn = _norm_matmul2(h, g.reshape(1, d), wq.astype(BF16), min(512, t), 1024)
    eidx_t, gate_t = _peer_route(q, keys)
    out = _peer_experts(eidx_t, gate_t, xn.reshape(t, 8, 128), h.reshape(t, 8, 128), _pack_table(u), _pack_table(v))
    return out.reshape(t, d)


def _rmsnorm_kernel(x_ref, g_ref, o_ref):
    x = x_ref[...]
    o_ref[...] = x * lax.rsqrt(jnp.mean(x * x, axis=-1, keepdims=True) + NORM_EPS) * g_ref[...]


def _rmsnorm(x, g, tm=512):
    m, d = x.shape
    return pl.pallas_call(
        _rmsnorm_kernel,
        grid=(m // tm,),
        in_specs=[pl.BlockSpec((tm, d), lambda i: (i, 0)), pl.BlockSpec((1, d), lambda i: (0, 0))],
        out_specs=pl.BlockSpec((tm, d), lambda i: (i, 0)),
        out_shape=jax.ShapeDtypeStruct((m, d), F32),
        compiler_params=_cparams(("arbitrary",)),
    )(x, g.reshape(1, d))


def kernel(x, mix_norm, w_in, w_out, ssm_conv_w, ssm_conv_b, ssm_dt_bias, ssm_a_log, ssm_d, ssm_norm, lru_conv_w, lru_conv_b, lru_wa, lru_ba, lru_wi, lru_bi, lru_lambda, rwkv_mu, rwkv_w0, rwkv_w2, rwkv_a0, rwkv_a2, rwkv_g2, rwkv_kk, rwkv_ka, rwkv_rk, rwkv_ln_w, rwkv_ln_b, nsa_cmp_pos, nsa_cmp_w1, nsa_cmp_w2, ffn_norm, peer_wq, peer_keys, peer_u, peer_v, final_norm):
    bsz, s, d = x.shape
    t = bsz * s
    h = x
    for l in range(w_in.shape[0]):
        w_l = _arrange_cols(w_in[l]).astype(BF16)
        cols = _norm_matmul(h.reshape(t, d), mix_norm[l].reshape(1, d), w_l, 512, 640).reshape(bsz, s, NCOL)
        y_ssm = _ssd_mixer(cols, ssm_conv_w[l], ssm_conv_b[l], ssm_dt_bias[l], ssm_a_log[l], ssm_d[l], ssm_norm[l])
        y_lru = _lru_mixer(cols, lru_conv_w[l], lru_conv_b[l], lru_wa[l], lru_ba[l], lru_wi[l], lru_bi[l], lru_lambda[l])
        y_rw = _rwkv_mixer(cols, rwkv_mu[l], rwkv_w0[l], rwkv_w2[l], rwkv_a0[l], rwkv_a2[l], rwkv_g2[l],
                           rwkv_kk[l], rwkv_ka[l], rwkv_rk[l], rwkv_ln_w[l], rwkv_ln_b[l])
        o_cmp, o_sel, o_win = _nsa_branches(cols, nsa_cmp_pos[l], nsa_cmp_w1[l], nsa_cmp_w2[l])
        h = _mix_out(h, y_ssm, y_lru, y_rw, o_cmp, o_sel, o_win, cols, w_out[l])
        h = _peer_block(h.reshape(t, d), ffn_norm[l], peer_wq[l], peer_keys[l], peer_u[l], peer_v[l]).reshape(bsz, s, d)
    return _rmsnorm(h.reshape(t, d), final_norm).reshape(bsz, s, d)
```

```python
import functools
import math

import numpy as np
import jax
import jax.numpy as jnp
from jax import lax
from jax.experimental import pallas as pl
from jax.experimental.pallas import tpu as pltpu

F32 = jnp.float32
BF16 = jnp.bfloat16
I32 = jnp.int32

NORM_EPS = 1e-6
GROUP_WIDTH = 256
HEAD_DIM = 64
N_HEADS = 4
LRU_C = 8.0
NEG_BIG = -1e30


COL = dict(ssd_z=0, ssd_x=256, lru_gate=512, lru_x=768, rw_r=1024, rw_k=1280, rw_v=1536, rw_wa=1792, rw_g=1920,
           nsa_q=2048, nsa_kvc=2304, nsa_kvs=2432, nsa_kvw=2560, nsa_gate=2688, ssd_b=2816, ssd_c=2944, ssd_dt=3072)
NCOL = 3200


def _col_perm():
    src = np.full((NCOL,), -1, np.int64)

    def put(dst, s0, n):
        src[dst:dst + n] = np.arange(s0, s0 + n)

    put(COL["ssd_z"], 0, 256)
    put(COL["ssd_x"], 256, 256)
    put(COL["ssd_b"], 512, 128)
    put(COL["ssd_c"], 640, 128)
    put(COL["ssd_dt"], 768, 4)
    put(COL["lru_gate"], 772, 256)
    put(COL["lru_x"], 1028, 256)
    put(COL["rw_r"], 1284, 1024)
    put(COL["nsa_q"], 2308, 256)
    put(COL["nsa_kvc"], 2564, 128)
    put(COL["nsa_kvs"], 2692, 128)
    put(COL["nsa_kvw"], 2820, 128)
    put(COL["nsa_gate"], 2948, 12)
    return src


def _arrange_cols(a):
    src = _col_perm()
    out = jnp.take(a, jnp.asarray(np.maximum(src, 0)), axis=-1)
    return jnp.where(jnp.asarray(src >= 0), out, jnp.zeros((), a.dtype))


def _cparams(sem, vmem_mb=None):
    kw = dict(dimension_semantics=sem)
    if vmem_mb is not None:
        kw["vmem_limit_bytes"] = vmem_mb * 1024 * 1024
    return pltpu.CompilerParams(**kw)


def _dot(a, b):
    return jnp.dot(a.astype(BF16), b.astype(BF16), preferred_element_type=F32)


def _dot_nt(a, b):
    return lax.dot_general(a.astype(BF16), b.astype(BF16), (((1,), (1,)), ((), ())), preferred_element_type=F32)


def _dot_tn(a, b):
    return lax.dot_general(a.astype(BF16), b.astype(BF16), (((0,), (0,)), ((), ())), preferred_element_type=F32)


def _split3(x):
    h = x.astype(BF16)
    r = x - h.astype(F32)
    m = r.astype(BF16)
    l = (r - m.astype(F32)).astype(BF16)
    return h, m, l


def _split2(x):
    h = x.astype(BF16)
    return h, (x - h.astype(F32)).astype(BF16)


def _dot_exact_rhs(a, b_exact):
    h, m, l = _split3(a)
    d = lambda t: jnp.dot(t, b_exact, preferred_element_type=F32)
    return d(h) + d(m) + d(l)


def _dot_exact_lhs(a_exact, b):
    h, m, l = _split3(b)
    d = lambda t: jnp.dot(a_exact, t, preferred_element_type=F32)
    return d(h) + d(m) + d(l)


def _sigmoid(x):
    return 1.0 / (1.0 + jnp.exp(-x))


def _silu(x):
    return x * _sigmoid(x)


def _gelu_tanh(x):
    return 0.5 * x * (1.0 + jnp.tanh(math.sqrt(2.0 / math.pi) * (x + 0.044715 * (x * x * x))))


def _softplus(x):
    return jnp.maximum(x, 0.0) + jnp.log1p(jnp.exp(-jnp.abs(x)))


def _conv4(cur, tail, w, b):
    c = cur.shape[1]
    row8 = lax.broadcasted_iota(I32, (8, c), 0)
    acc = cur * w[3:4, :] + b
    for k in (1, 2, 3):
        cr = pltpu.roll(cur, k, axis=0)
        tr = pltpu.roll(tail, k, axis=0)
        head = jnp.where(row8 < k, tr, cr[:8])
        sh = jnp.concatenate([head, cr[8:]], axis=0)
        acc = acc + sh * w[3 - k:4 - k, :]
    return acc


def _norm_matmul_kernel(x_ref, g_ref, w_ref, o_ref, xn_ref):
    @pl.when(pl.program_id(1) == 0)
    def _():
        x = x_ref[...]
        y = x * lax.rsqrt(jnp.mean(x * x, axis=-1, keepdims=True) + NORM_EPS) * g_ref[...]
        xn_ref[...] = y.astype(BF16)

    o_ref[...] = jnp.dot(xn_ref[...], w_ref[...], preferred_element_type=F32)


def _norm_matmul(x, g, w, tm, tn):
    m, k = x.shape
    n = w.shape[1]
    return pl.pallas_call(
        _norm_matmul_kernel,
        grid=(m // tm, n // tn),
        in_specs=[pl.BlockSpec((tm, k), lambda i, j: (i, 0)),
                  pl.BlockSpec((1, k), lambda i, j: (0, 0)),
                  pl.BlockSpec((k, tn), lambda i, j: (0, j))],
        out_specs=pl.BlockSpec((tm, tn), lambda i, j: (i, j)),
        out_shape=jax.ShapeDtypeStruct((m, n), F32),
        scratch_shapes=[pltpu.VMEM((tm, k), BF16)],
        compiler_params=_cparams(("arbitrary", "arbitrary")),
    )(x, g, w)


SSD_CHUNK = 128


def _ssd_kernel(z_ref, x_ref, b_ref, c_ref, dt_ref, cwx_ref, cbx_ref, cwb_ref, cbb_ref, cwc_ref, cbc_ref,
                dtb_ref, alog_ref, d_ref, nw_ref, o_ref, tx_ref, tb_ref, tc_ref, st_ref):
    t = SSD_CHUNK

    @pl.when(pl.program_id(1) == 0)
    def _():
        tx_ref[...] = jnp.zeros_like(tx_ref)
        tb_ref[...] = jnp.zeros_like(tb_ref)
        tc_ref[...] = jnp.zeros_like(tc_ref)
        st_ref[...] = jnp.zeros_like(st_ref)

    xr, br, cr = x_ref[...], b_ref[...], c_ref[...]
    xs = _silu(_conv4(xr, tx_ref[...], cwx_ref[...], cbx_ref[...]))
    bm = _silu(_conv4(br, tb_ref[...], cwb_ref[...], cbb_ref[...]))
    cm = _silu(_conv4(cr, tc_ref[...], cwc_ref[...], cbc_ref[...]))
    tx_ref[...] = xr[t - 8:]
    tb_ref[...] = br[t - 8:]
    tc_ref[...] = cr[t - 8:]

    dt = _softplus(dt_ref[...] + dtb_ref[...])
    a = dt * (-jnp.exp(alog_ref[...]))
    ri = lax.broadcasted_iota(I32, (t, t), 0)
    ci = lax.broadcasted_iota(I32, (t, t), 1)
    tril = ri >= ci
    acs = _dot_exact_lhs(jnp.where(tril, 1.0, 0.0).astype(BF16), a)
    acs_t = acs.T
    d_row = d_ref[...]
    ys = []
    for h in range(N_HEADS):
        g = h // 2
        lo, hi = HEAD_DIM * h, HEAD_DIM * (h + 1)
        b_g = bm[:, HEAD_DIM * g:HEAD_DIM * (g + 1)]
        c_g = cm[:, HEAD_DIM * g:HEAD_DIM * (g + 1)]
        x_h = xs[:, lo:hi]
        acs_c = acs[:, h:h + 1]
        lmat = jnp.exp(jnp.where(tril, acs_c - acs_t[h:h + 1, :], NEG_BIG))
        scores = _dot_nt(c_g, b_g) * lmat
        xdt = x_h * dt[:, h:h + 1]
        st = st_ref[h]
        y = _dot(scores, xdt) + _dot(c_g, st) * jnp.exp(acs_c) + x_h * d_row[:, lo:hi]
        a_last = acs[t - 1:t, h:h + 1]
        bw = b_g * jnp.exp(a_last - acs_c)
        st_ref[h] = st * jnp.exp(a_last) + _dot_tn(bw, xdt)
        ys.append(y)
    y = jnp.concatenate(ys, axis=1) * _silu(z_ref[...])
    outs = []
    for g in range(2):
        yg = y[:, 128 * g:128 * (g + 1)]
        outs.append(yg * lax.rsqrt(jnp.mean(yg * yg, axis=-1, keepdims=True) + NORM_EPS))
    o_ref[...] = jnp.concatenate(outs, axis=1) * nw_ref[...]


def _ssd_mixer(cols, conv_w, conv_b, dt_bias, a_log, d_skip, norm_w):
    bsz, s, _ = cols.shape
    t = SSD_CHUNK
    row = lambda v, n: jnp.pad(v.reshape(1, -1), ((0, 0), (0, n - v.size)))
    full = lambda shp: pl.BlockSpec(shp, lambda b, c: (0,) * len(shp))
    colspec = lambda w, name: pl.BlockSpec((None, t, w), lambda b, c: (b, c, COL[name] // w))
    return pl.pallas_call(
        _ssd_kernel,
        grid=(bsz, s // t),
        in_specs=[colspec(256, "ssd_z"), colspec(256, "ssd_x"), colspec(128, "ssd_b"), colspec(128, "ssd_c"),
                  colspec(128, "ssd_dt"),
                  full((4, 256)), full((1, 256)), full((4, 128)), full((1, 128)), full((4, 128)), full((1, 128)),
                  full((1, 128)), full((1, 128)), full((1, 256)), full((1, 256))],
        out_specs=pl.BlockSpec((None, t, 256), lambda b, c: (b, c, 0)),
        out_shape=jax.ShapeDtypeStruct((bsz, s, 256), F32),
        scratch_shapes=[pltpu.VMEM((8, 256), F32), pltpu.VMEM((8, 128), F32), pltpu.VMEM((8, 128), F32),
                        pltpu.VMEM((N_HEADS, HEAD_DIM, HEAD_DIM), F32)],
        compiler_params=_cparams(("arbitrary", "arbitrary")),
    )(cols, cols, cols, cols, cols,
      conv_w[:, :256], conv_b[:256].reshape(1, 256), conv_w[:, 256:384], conv_b[256:384].reshape(1, 128),
      conv_w[:, 384:512], conv_b[384:512].reshape(1, 128),
      row(dt_bias, 128), row(a_log, 128), jnp.repeat(d_skip, HEAD_DIM).reshape(1, 256), norm_w.reshape(1, 256))


LRU_TILE = 256


def _lru_kernel(gate_ref, xr_ref, cw_ref, cb_ref, wa_ref, ba_ref, wi_ref, bi_ref, lam_ref, o_ref, tail_ref, h_ref):
    t = LRU_TILE

    @pl.when(pl.program_id(1) == 0)
    def _():
        tail_ref[...] = jnp.zeros_like(tail_ref)
        h_ref[...] = jnp.zeros_like(h_ref)

    xr = xr_ref[...]
    xc = _conv4(xr, tail_ref[...], cw_ref[...], cb_ref[...])
    tail_ref[...] = xr[t - 8:]
    r = _sigmoid(_dot(xc, wa_ref[...]) + ba_ref[...])
    i = _sigmoid(_dot(xc, wi_ref[...]) + bi_ref[...])
    log_a = LRU_C * r * (-_softplus(-lam_ref[...]))
    a = jnp.exp(log_a)
    u = jnp.sqrt(-jnp.tanh(log_a) * (a * a + 1.0)) * (i * xc)
    rows = lax.broadcasted_iota(I32, (t, GROUP_WIDTH), 0)
    d = 1
    while d < t:
        a_sh = pltpu.roll(a, d, axis=0)
        u_sh = pltpu.roll(u, d, axis=0)
        m = rows >= d
        u = jnp.where(m, a * u_sh + u, u)
        a = jnp.where(m, a * a_sh, a)
        d *= 2
    h = a * h_ref[7:8, :] + u
    h_ref[...] = h[t - 8:]
    o_ref[...] = h * _gelu_tanh(gate_ref[...])


def _lru_mixer(cols, conv_w, conv_b, wa, ba, wi, bi, lam):
    bsz, s, _ = cols.shape
    t = LRU_TILE
    cb = COL["lru_gate"] // 256
    bd = lambda w: jax.scipy.linalg.block_diag(*[w[k] for k in range(w.shape[0])]).astype(BF16)
    full = lambda shp: pl.BlockSpec(shp, lambda b, c: (0,) * len(shp))
    return pl.pallas_call(
        _lru_kernel,
        grid=(bsz, s // t),
        in_specs=[pl.BlockSpec((None, t, 256), lambda b, c: (b, c, cb)),
                  pl.BlockSpec((None, t, 256), lambda b, c: (b, c, cb + 1)),
                  full((4, 256)), full((1, 256)), full((256, 256)), full((1, 256)), full((256, 256)), full((1, 256)),
                  full((1, 256))],
        out_specs=pl.BlockSpec((None, t, 256), lambda b, c: (b, c, 0)),
        out_shape=jax.ShapeDtypeStruct((bsz, s, 256), F32),
        scratch_shapes=[pltpu.VMEM((8, 256), F32), pltpu.VMEM((8, 256), F32)],
        compiler_params=_cparams(("arbitrary", "arbitrary")),
    )(cols, cols, conv_w, conv_b.reshape(1, 256), bd(wa), ba.reshape(1, 256), bd(wi), bi.reshape(1, 256),
      lam.reshape(1, 256))


RW_TILE = 128
RW_GROUP = 4
RWKV_GN_EPS = 64e-5


def _segsum_bc(x):
    r = x.shape[0]
    m = lax.broadcasted_iota(I32, (r, 128), 1) < HEAD_DIM
    outs = []
    for j in range(2):
        xj = x[:, 128 * j:128 * (j + 1)]
        lo = jnp.sum(jnp.where(m, xj, 0.0), axis=1, keepdims=True)
        hi = jnp.sum(jnp.where(m, 0.0, xj), axis=1, keepdims=True)
        outs.append(jnp.where(m, lo, hi))
    return jnp.concatenate(outs, axis=1)


def _rwkv_kernel(c_ref, mu_ref, w0_ref, w2_ref, a0_ref, a2_ref, g2_ref, kkp_ref, ka_ref, rk_ref, lnw_ref, lnb_ref, e_ref,
                 o_ref, prev_ref, s_ref, r_s, w_s, k_s, v_s, kk_s, b_s, g_s, bon_s, y_s):
    nb = c_ref.shape[0]
    t = RW_TILE

    @pl.when(pl.program_id(0) == 0)
    def _():
        prev_ref[...] = jnp.zeros_like(prev_ref)
        s_ref[...] = jnp.zeros_like(s_ref)
        r_s[...] = jnp.zeros_like(r_s)

    row = lax.broadcasted_iota(I32, (t, 1024), 0)
    for b in range(nb):
        c = c_ref[b]
        pr = jnp.where(row == 0, prev_ref[b, 7:8, :], pltpu.roll(c, 1, axis=0))
        prev_ref[b] = c[t - 8:]
        c = c + (pr - c) * mu_ref[...]
        r, k, v = c[:, 0:256], c[:, 256:512], c[:, 512:768]
        wd, ad, gd = c[:, 768:832], c[:, 832:896], c[:, 896:1024]
        w_log = -_softplus(-(w0_ref[...] + _dot(jnp.tanh(wd), w2_ref[...]))) - 0.5
        a = _sigmoid(a0_ref[...] + _dot(ad, a2_ref[...]))
        kk = k * kkp_ref[...]
        kk = kk / jnp.maximum(jnp.sqrt(_segsum_bc(kk * kk)), 1e-12)
        k2 = k * (1.0 + (a - 1.0) * ka_ref[...])
        r_s[b, 8:, :] = r
        w_s[b] = jnp.exp(-jnp.exp(w_log))
        k_s[b] = k2
        v_s[b] = v
        kk_s[b] = kk
        b_s[b] = kk * a
        g_s[b] = _dot(_sigmoid(gd), g2_ref[...])
        bon_s[b] = _segsum_bc(r * k2 * rk_ref[...]) * v

    gb = RW_GROUP if nb % RW_GROUP == 0 else nb
    nr = gb * HEAD_DIM
    eye4 = jnp.where(lax.broadcasted_iota(I32, (nr, 256), 0) % HEAD_DIM
                     == lax.broadcasted_iota(I32, (nr, 256), 1) % HEAD_DIM, 1.0, 0.0)

    def seg(*xs):
        parts = []
        for x in xs:
            parts.extend(_split2(x))
        out = jnp.dot(jnp.concatenate(parts, axis=0), e_ref[...], preferred_element_type=F32)
        return [out[2 * i * nr:(2 * i + 1) * nr] + out[(2 * i + 1) * nr:(2 * i + 2) * nr] for i in range(len(xs))]

    def rows(ref, b0, i):
        return jnp.concatenate([jnp.broadcast_to(ref[b0 + b, pl.ds(i, 1), :], (HEAD_DIM, 256)) for b in range(gb)],
                               axis=0)

    def put_y(ybc, b0, i):
        yd = ybc * eye4
        for b in range(gb):
            y_s[b0 + b, pl.ds(i, 1), :] = jnp.sum(yd[b * HEAD_DIM:(b + 1) * HEAD_DIM], axis=0, keepdims=True)

    def step(tt, carry):
        for b0 in range(0, nb, gb):
            sl = slice(b0 * HEAD_DIM, (b0 + gb) * HEAD_DIM)
            s = s_ref[sl, :]
            vbc = _segsum_bc(eye4 * rows(v_s, b0, tt))
            sa, ybc = seg(s * rows(kk_s, b0, tt), s * rows(r_s, b0, tt + 7))
            put_y(ybc, b0, tt + 7)
            s_ref[sl, :] = s * rows(w_s, b0, tt) - sa * rows(b_s, b0, tt) + vbc * rows(k_s, b0, tt)
        return carry

    lax.fori_loop(0, t, step, 0)
    for b0 in range(0, nb, gb):
        ybc, = seg(s_ref[b0 * HEAD_DIM:(b0 + gb) * HEAD_DIM, :] * rows(r_s, b0, t + 7))
        put_y(ybc, b0, t + 7)

    for b in range(nb):
        y = y_s[b, 8:, :]
        d = y - _segsum_bc(y) * (1.0 / HEAD_DIM)
        var = _segsum_bc(d * d) * (1.0 / HEAD_DIM)
        y = d * lax.rsqrt(var + RWKV_GN_EPS) * lnw_ref[...] + lnb_ref[...] + bon_s[b]
        o_ref[b] = y * g_s[b]


def _rwkv_mixer(cols, mu, w0, w2, a0, a2, g2, k_k, k_a, r_k, ln_w, ln_b):
    bsz, s, _ = cols.shape
    t = RW_TILE
    full = lambda shp: pl.BlockSpec(shp, lambda c: (0,) * len(shp))
    r1 = lambda v: v.reshape(1, -1)
    tile = lambda: pltpu.VMEM((bsz, t, 256), F32)
    return pl.pallas_call(
        _rwkv_kernel,
        grid=(s // t,),
        in_specs=[pl.BlockSpec((bsz, t, 1024), lambda c: (0, c, COL["rw_r"] // 1024)),
                  full((1, 1024)), full((1, 256)), full((64, 256)), full((1, 256)), full((64, 256)), full((128, 256)),
                  full((1, 256)), full((1, 256)), full((1, 256)), full((1, 256)), full((1, 256)), full((256, 256))],
        out_specs=pl.BlockSpec((bsz, t, 256), lambda c: (0, c, 0)),
        out_shape=jax.ShapeDtypeStruct((bsz, s, 256), F32),
        scratch_shapes=([pltpu.VMEM((bsz, 8, 1024), F32), pltpu.VMEM((bsz * HEAD_DIM, 256), F32),
                         pltpu.VMEM((bsz, t + 8, 256), F32)] + [tile() for _ in range(7)]
                        + [pltpu.VMEM((bsz, t + 8, 256), F32)]),
        compiler_params=_cparams(("arbitrary",)),
    )(cols, r1(mu), r1(w0), w2, r1(a0), a2, g2, r1(k_k), r1(k_a), r1(r_k), r1(ln_w), r1(ln_b),
      jnp.asarray(np.kron(np.eye(N_HEADS), np.ones((HEAD_DIM, HEAD_DIM))), BF16))


NSA_STRIDE = 16
NSA_SEL_BLOCK = 64
NSA_TOPN = 16
NSA_WINDOW = 512
NSA_FORCE = 1e4
NSA_TQ = 256
NSA_TK = 512
NSA_SLOPES = tuple(2.0 ** (-8.0 * (h + 1) / N_HEADS) for h in range(N_HEADS))
SEL_OFF = -float(2 ** 30)


def _dot3_nt(a, b):
    ah, al = _split2(a)
    bh, bl = _split2(b)
    d = lambda x, y: lax.dot_general(x, y, (((1,), (1,)), ((), ())), preferred_element_type=F32)
    return d(ah, bh) + d(ah, bl) + d(al, bh)


def _nsa_compress_kernel(kv_ref, pos_ref, w1_ref, w2_ref, o_ref):
    nbp = o_ref.shape[0]
    outs = []
    for c in range(2):
        acc_a = jnp.zeros((nbp, 256), F32)
        acc_b = jnp.zeros((nbp, 256), F32)
        for r in range(NSA_STRIDE):
            xr = kv_ref[pl.ds(r, nbp, stride=NSA_STRIDE), :][:, HEAD_DIM * c:HEAD_DIM * (c + 1)]
            acc_a = acc_a + _dot(xr + pos_ref[c, r:r + 1, :], w1_ref[c, 64 * r:64 * (r + 1), :])
            acc_b = acc_b + _dot(xr + pos_ref[c, 16 + r:17 + r, :], w1_ref[c, 1024 + 64 * r:1024 + 64 * (r + 1), :])
        hdn = _gelu_tanh(acc_a + pltpu.roll(acc_b, nbp - 1, axis=0))
        outs.append(_dot(hdn, w2_ref[c]))
    o_ref[...] = jnp.concatenate(outs, axis=1)


def _nsa_cmp_kernel(q_ref, kv_ref, ov_ref, ocmp_ref, qaug_ref, used_ref):
    tq = NSA_TQ
    nbp = kv_ref.shape[0]
    q0 = pl.program_id(1) * tq
    kv = kv_ref[...]
    kc, vc = kv[:, :HEAD_DIM], kv[:, HEAD_DIM:]
    pos = q0 + lax.broadcasted_iota(I32, (tq, nbp), 0)
    blk_end = lax.broadcasted_iota(I32, (tq, nbp), 1) * NSA_STRIDE + (2 * NSA_STRIDE - 1)
    dist = pos - blk_end
    mask = dist >= 0
    distf = dist.astype(F32)
    q = q_ref[...] * (HEAD_DIM ** -0.5)
    lane = lax.broadcasted_iota(I32, (tq, 128), 1)
    psum = jnp.zeros((tq, nbp), F32)
    for h in range(N_HEADS):
        qh = q[:, HEAD_DIM * h:HEAD_DIM * (h + 1)]
        s = jnp.where(mask, _dot3_nt(qh, kc) - NSA_SLOPES[h] * distf, -jnp.inf)
        m = jnp.max(s, axis=1, keepdims=True)
        m = jnp.where(m == -jnp.inf, 0.0, m)
        e = jnp.where(mask, jnp.exp(s - m), 0.0)
        p = e / jnp.maximum(jnp.sum(e, axis=1, keepdims=True), 1e-30)
        ocmp_ref[h] = _dot(p, vc)
        psum = psum + p
        qt = q[:, 128 * (h // 2):128 * (h // 2 + 1)]
        keep = (lane < HEAD_DIM) if h % 2 == 0 else (lane >= HEAD_DIM)
        qaug_ref[h, :, 0:128] = jnp.where(keep, qt, 0.0).astype(BF16)
    imp_t = _dot_exact_rhs(psum, ov_ref[...]).T
    j = lax.broadcasted_iota(I32, (128, tq), 0)
    pos_t = q0 + lax.broadcasted_iota(I32, (128, tq), 1)
    cur = pos_t // NSA_SEL_BLOCK
    forced = (j == 0) | (j == cur) | (j == cur - 1)
    valid = j * NSA_SEL_BLOCK <= pos_t
    x = jnp.where(valid, jnp.where(forced, NSA_FORCE, imp_t), -jnp.inf)
    jf = j.astype(F32)
    picked = jnp.zeros((128, tq), F32)
    for _ in range(NSA_TOPN):
        mx = jnp.max(x, axis=0, keepdims=True)
        first = jnp.min(jnp.where(x == mx, jf, 128.0), axis=0, keepdims=True)
        hit = jf == first
        picked = jnp.where(hit, 1.0, picked)
        x = jnp.where(hit, -jnp.inf, x)
    used = jnp.where(valid & (picked > 0.0), 1.0, 0.0)
    bias = jnp.where(used > 0.0, 0.0, SEL_OFF).T
    for h in range(N_HEADS):
        qaug_ref[h, :, 128:256] = bias.astype(BF16)
    used_ref[...] = jnp.broadcast_to(jnp.max(used.T, axis=0, keepdims=True), (8, 128))


def _nsa_flash_kernel(need_ref, qaug_ref, kv_ref, o_ref, kaug_s, v_s, *, window):
    tq, tk = NSA_TQ, NSA_TK
    s_len = kv_ref.shape[0]
    tk = min(tk, s_len)
    kdim = 128 if window else 256
    qi = pl.program_id(1)
    q0 = qi * tq

    @pl.when(qi == 0)
    def _():
        def build(i, c):
            r0 = pl.multiple_of(i * tk, tk)
            kv = kv_ref[pl.ds(r0, tk), :]
            k = kv[:, :HEAD_DIM]
            parts = [k, k]
            if not window:
                blk = (r0 + lax.broadcasted_iota(I32, (tk, 128), 0)) // NSA_SEL_BLOCK
                parts.append(jnp.where(blk == lax.broadcasted_iota(I32, (tk, 128), 1), 1.0, 0.0))
            kaug_s[pl.ds(r0, tk), :] = jnp.concatenate(parts, axis=1).astype(BF16)
            v_s[pl.ds(r0, tk), :] = kv[:, HEAD_DIM:].astype(BF16)
            return c
        lax.fori_loop(0, s_len // tk, build, 0)

    q = qaug_ref[...][:, :, :kdim].reshape(N_HEADS * tq, kdim)
    row = lax.broadcasted_iota(I32, (N_HEADS * tq, tk), 0)
    t_q = q0 + (row & (tq - 1))
    head = lax.broadcasted_iota(I32, (N_HEADS * tq, 1), 0) // tq
    slope = jnp.where(head == 0, NSA_SLOPES[0], jnp.where(head == 1, NSA_SLOPES[1],
                      jnp.where(head == 2, NSA_SLOPES[2], NSA_SLOPES[3])))
    col = lax.broadcasted_iota(I32, (N_HEADS * tq, tk), 1)

    n_kt = s_len // tk
    need_base = (pl.program_id(0) * pl.num_programs(1) + qi) * n_kt

    def body(kt, carry):
        if window:
            return tile(kt, carry)
        return lax.cond(need_ref[need_base + kt] != 0, lambda c: tile(kt, c), lambda c: c, carry)

    def tile(kt, carry):
        m, l, acc = carry
        k0 = pl.multiple_of(kt * tk, tk)
        s = lax.dot_general(q, kaug_s[pl.ds(k0, tk), :], (((1,), (1,)), ((), ())), preferred_element_type=F32)
        t_k = k0 + col
        s = s + slope * (t_k - q0).astype(F32)
        ok = t_k <= t_q
        if window:
            ok = ok & (t_q - t_k < NSA_WINDOW)
        m_new = jnp.maximum(m, jnp.max(jnp.where(ok, s, NEG_BIG), axis=1, keepdims=True))
        p = jnp.where(ok, jnp.exp(s - m_new), 0.0)
        alpha = jnp.exp(m - m_new)
        l = alpha * l + jnp.sum(p, axis=1, keepdims=True)
        acc = alpha * acc + jnp.dot(p.astype(BF16), v_s[pl.ds(k0, tk), :], preferred_element_type=F32)
        return m_new, l, acc

    lo = jnp.maximum(q0 - NSA_WINDOW, 0) // tk if window else 0
    hi = (q0 + tq - 1) // tk + 1
    init = (jnp.full((N_HEADS * tq, 1), NEG_BIG, F32), jnp.zeros((N_HEADS * tq, 1), F32),
            jnp.zeros((N_HEADS * tq, HEAD_DIM), F32))
    m, l, acc = lax.fori_loop(lo, hi, body, init)
    o_ref[...] = (acc / l).reshape(N_HEADS, tq, HEAD_DIM)


def _nsa_overlap(s):
    nbp, nsel = s // NSA_STRIDE, s // NSA_SEL_BLOCK
    cs = np.arange(nbp) * NSA_STRIDE
    ss = np.arange(128) * NSA_SEL_BLOCK
    ov = np.minimum(cs[:, None] + 2 * NSA_STRIDE, ss[None, :] + NSA_SEL_BLOCK) - np.maximum(cs[:, None], ss[None, :])
    ov = np.clip(ov, 0, None) / (2 * NSA_STRIDE)
    ov[nbp - 1:, :] = 0.0
    ov[:, nsel:] = 0.0
    return jnp.asarray(ov, BF16)


def _nsa_branches(cols, cmp_pos, cmp_w1, cmp_w2):
    bsz, s, _ = cols.shape
    nbp = s // NSA_STRIDE
    tq = NSA_TQ
    kvcmp = pl.pallas_call(
        _nsa_compress_kernel,
        grid=(bsz,),
        in_specs=[pl.BlockSpec((None, s, 128), lambda b: (b, 0, COL["nsa_kvc"] // 128)),
                  pl.BlockSpec((2, 32, 64), lambda b: (0, 0, 0)),
                  pl.BlockSpec((2, 2048, 256), lambda b: (0, 0, 0)),
                  pl.BlockSpec((2, 256, 64), lambda b: (0, 0, 0))],
        out_specs=pl.BlockSpec((None, nbp, 128), lambda b: (b, 0, 0)),
        out_shape=jax.ShapeDtypeStruct((bsz, nbp, 128), F32),
        compiler_params=_cparams(("arbitrary",), 48),
    )(cols, cmp_pos, cmp_w1.astype(BF16), cmp_w2)
    head_out = lambda dt, w: (pl.BlockSpec((None, N_HEADS, tq, w), lambda b, i: (b, 0, i, 0)),
                              jax.ShapeDtypeStruct((bsz, N_HEADS, s, w), dt))
    (oc_spec, oc_shape), (qa_spec, qa_shape) = head_out(F32, HEAD_DIM), head_out(BF16, 256)
    nqt = s // tq
    o_cmp, qaug, used = pl.pallas_call(
        _nsa_cmp_kernel,
        grid=(bsz, nqt),
        in_specs=[pl.BlockSpec((None, tq, 256), lambda b, i: (b, i, COL["nsa_q"] // 256)),
                  pl.BlockSpec((None, nbp, 128), lambda b, i: (b, 0, 0)),
                  pl.BlockSpec((nbp, 128), lambda b, i: (0, 0))],
        out_specs=[oc_spec, qa_spec, pl.BlockSpec((None, None, 8, 128), lambda b, i: (b, i, 0, 0))],
        out_shape=[oc_shape, qa_shape, jax.ShapeDtypeStruct((bsz, nqt, 8, 128), F32)],
        compiler_params=_cparams(("arbitrary", "arbitrary"), 48),
    )(cols, kvcmp, _nsa_overlap(s))
    tk = min(NSA_TK, s)
    bpt = tk // NSA_SEL_BLOCK
    need = (used[:, :, 0, :(s // tk) * bpt].reshape(bsz, nqt, s // tk, bpt).max(axis=-1) > 0).astype(I32).reshape(-1)

    def flash(name, window):
        hmap = lambda b, i, need_ref: (b, 0, i, 0)
        return pl.pallas_call(
            functools.partial(_nsa_flash_kernel, window=window),
            grid_spec=pltpu.PrefetchScalarGridSpec(
                num_scalar_prefetch=1,
                grid=(bsz, nqt),
                in_specs=[pl.BlockSpec((None, N_HEADS, tq, 256), hmap),
                          pl.BlockSpec((None, s, 128), lambda b, i, need_ref: (b, 0, COL[name] // 128))],
                out_specs=pl.BlockSpec((None, N_HEADS, tq, HEAD_DIM), hmap),
                scratch_shapes=[pltpu.VMEM((s, 128 if window else 256), BF16), pltpu.VMEM((s, HEAD_DIM), BF16)]),
            out_shape=oc_shape,
            compiler_params=_cparams(("arbitrary", "arbitrary"), 48),
        )(need, qaug, cols)

    return o_cmp, flash("nsa_kvs", False), flash("nsa_kvw", True)


def _nsa_mixer(cols, cmp_pos, cmp_w1, cmp_w2):
    o_cmp, o_sel, o_win = _nsa_branches(cols, cmp_pos, cmp_w1, cmp_w2)
    bsz, s, _ = cols.shape
    g = jax.nn.sigmoid(cols[..., COL["nsa_gate"]:COL["nsa_gate"] + 12]).reshape(bsz, s, N_HEADS, 3)
    tr = lambda o: jnp.moveaxis(o, 1, 2)
    out = g[..., 0:1] * tr(o_cmp) + g[..., 1:2] * tr(o_sel) + g[..., 2:3] * tr(o_win)
    return out.reshape(bsz, s, 256)


MIX_TILE = 256


def _mix_out_kernel(h_ref, ssm_ref, lru_ref, rw_ref, oc_ref, os_ref, ow_ref, gate_ref, w_ref, o_ref):
    acc = h_ref[...]
    acc = acc + _dot(ssm_ref[...], w_ref[0:256, :])
    acc = acc + _dot(lru_ref[...], w_ref[256:512, :])
    acc = acc + _dot(rw_ref[...], w_ref[512:768, :])
    g = _sigmoid(gate_ref[...])
    for h in range(N_HEADS):
        y = (g[:, 3 * h:3 * h + 1] * oc_ref[h] + g[:, 3 * h + 1:3 * h + 2] * os_ref[h]
             + g[:, 3 * h + 2:3 * h + 3] * ow_ref[h])
        acc = acc + _dot(y, w_ref[768 + HEAD_DIM * h:768 + HEAD_DIM * (h + 1), :])
    o_ref[...] = acc


def _mix_out(h, y_ssm, y_lru, y_rw, o_cmp, o_sel, o_win, cols, w_out):
    bsz, s, d = h.shape
    t = MIX_TILE
    tok = lambda w: pl.BlockSpec((None, t, w), lambda b, i: (b, i, 0))
    hd = pl.BlockSpec((None, N_HEADS, t, HEAD_DIM), lambda b, i: (b, 0, i, 0))
    return pl.pallas_call(
        _mix_out_kernel,
        grid=(bsz, s // t),
        in_specs=[tok(d), tok(256), tok(256), tok(256), hd, hd, hd,
                  pl.BlockSpec((None, t, 128), lambda b, i: (b, i, COL["nsa_gate"] // 128)),
                  pl.BlockSpec((4 * GROUP_WIDTH, d), lambda b, i: (0, 0))],
        out_specs=tok(d),
        out_shape=jax.ShapeDtypeStruct((bsz, s, d), F32),
        compiler_params=_cparams(("arbitrary", "arbitrary")),
    )(h, y_ssm, y_lru, y_rw, o_cmp, o_sel, o_win, cols, w_out.astype(BF16))


PEER_HEADS = 8
PEER_NKEYS = 128
PEER_TOPK = 16
PEER_TT = 128
PEER_TQ = 256
PEER_ACT_UNROLL = 16
PEER_OUT_UNROLL = 128


def _norm_matmul2_kernel(x_ref, g_ref, w_ref, o_ref, xn_out_ref, xn_ref):
    @pl.when(pl.program_id(1) == 0)
    def _():
        x = x_ref[...]
        y = x * lax.rsqrt(jnp.mean(x * x, axis=-1, keepdims=True) + NORM_EPS) * g_ref[...]
        xn_out_ref[...] = y
        xn_ref[...] = y.astype(BF16)

    o_ref[...] = jnp.dot(xn_ref[...], w_ref[...], preferred_element_type=F32)


def _norm_matmul2(x, g, w, tm, tn):
    m, k = x.shape
    n = w.shape[1]
    return pl.pallas_call(
        _norm_matmul2_kernel,
        grid=(m // tm, n // tn),
        in_specs=[pl.BlockSpec((tm, k), lambda i, j: (i, 0)),
                  pl.BlockSpec((1, k), lambda i, j: (0, 0)),
                  pl.BlockSpec((k, tn), lambda i, j: (0, j))],
        out_specs=[pl.BlockSpec((tm, tn), lambda i, j: (i, j)), pl.BlockSpec((tm, k), lambda i, j: (i, 0))],
        out_shape=[jax.ShapeDtypeStruct((m, n), F32), jax.ShapeDtypeStruct((m, k), F32)],
        scratch_shapes=[pltpu.VMEM((tm, k), BF16)],
        compiler_params=_cparams(("arbitrary", "arbitrary")),
    )(x, g, w)


def _top_rows(x, n, payload=None):
    r = x.shape[0]
    rid = lax.broadcasted_iota(I32, x.shape, 0).astype(F32)
    vals, ids, pays = [], [], []
    for _ in range(n):
        mx = jnp.max(x, axis=0, keepdims=True)
        first = jnp.min(jnp.where(x == mx, rid, float(r)), axis=0, keepdims=True)
        hit = rid == first
        vals.append(mx)
        ids.append(first)
        if payload is not None:
            pays.append(jnp.sum(jnp.where(hit, payload, 0.0), axis=0, keepdims=True))
        x = jnp.where(hit, -jnp.inf, x)
    cat = lambda v: jnp.concatenate(v, axis=0)
    return (cat(vals), cat(ids)) + ((cat(pays),) if payload is not None else ())


def _peer_route_kernel(q_ref, keys_ref, eidx_ref, gate_ref):
    k = PEER_TOPK
    tq = q_ref.shape[0]
    gates, eids = [], []
    for h in range(PEER_HEADS):
        s, i = [], []
        for c in range(2):
            qh = q_ref[:, 128 * (2 * h + c):128 * (2 * h + c + 1)]
            sv, si = _top_rows(_dot3_nt(keys_ref[h, c], qh), k)
            s.append(sv)
            i.append(si)
        cand, cidx = [], []
        for a in range(k):
            nb = k // (a + 1)
            cand.append(s[0][a:a + 1, :] + s[1][:nb])
            cidx.append(i[0][a:a + 1, :] * float(PEER_NKEYS) + i[1][:nb])
        n_c = sum(c.shape[0] for c in cand)
        pad = (-n_c) % 8
        cand.append(jnp.full((pad, tq), -jnp.inf, F32))
        cidx.append(jnp.zeros((pad, tq), F32))
        top, _, eidx = _top_rows(jnp.concatenate(cand, axis=0), k, payload=jnp.concatenate(cidx, axis=0))
        e = jnp.exp(top - top[0:1, :])
        gates.append(e / jnp.sum(e, axis=0, keepdims=True))
        eids.append(eidx)
    gate_ref[...] = jnp.concatenate(gates, axis=0).T
    eidx_ref[...] = (jnp.concatenate(eids, axis=0).T * float(PEER_ROW)).astype(I32)


def _peer_route(q, keys):
    t = q.shape[0]
    tq = min(PEER_TQ, t)
    nk = PEER_HEADS * PEER_TOPK
    return pl.pallas_call(
        _peer_route_kernel,
        grid=(t // tq,),
        in_specs=[pl.BlockSpec((tq, q.shape[1]), lambda i: (i, 0)),
                  pl.BlockSpec(keys.shape, lambda i: (0, 0, 0, 0))],
        out_specs=[pl.BlockSpec((tq, nk), lambda i: (i, 0)), pl.BlockSpec((tq, nk), lambda i: (i, 0))],
        out_shape=[jax.ShapeDtypeStruct((t, nk), I32), jax.ShapeDtypeStruct((t, nk), F32)],
        compiler_params=_cparams(("arbitrary",), 48),
    )(q, keys)


PEER_ROW = 4


def _pack_table(tab):
    e, d = tab.shape
    bits = lax.bitcast_convert_type(tab.astype(BF16), jnp.uint16).astype(jnp.uint32).reshape(e, 2, d // 2)
    word = (bits[:, 1, :] << 16) | bits[:, 0, :]
    return lax.bitcast_convert_type(word, I32).reshape(e * PEER_ROW, 128)


def _expert_halves(tbl_ref, r):
    word = tbl_ref[pl.ds(pl.multiple_of(r, PEER_ROW), PEER_ROW), :]
    return pltpu.bitcast(word << 16, F32), pltpu.bitcast(word & jnp.int32(-65536), F32)


_BITREV8 = (0, 4, 2, 6, 1, 5, 3, 7)


def _rowsum8(ps):
    sub = lax.broadcasted_iota(I32, (8, 128), 0)
    q = [ps[_BITREV8[p]] for p in range(8)]
    cur = [jnp.concatenate([q[2 * i], q[2 * i + 1]], axis=0) for i in range(4)]
    for h in (2, 1):
        m = (sub & h) == 0
        cur = [jnp.where(m, x, y) + jnp.where(m, pltpu.roll(x, 8 - h, axis=0), pltpu.roll(y, h, axis=0))
               for x, y in zip(cur[0::2], cur[1::2])]
    return cur[0]


def _peer_act_kernel(idx_ref, x_ref, gate_ref, tbl_ref, o_ref, q_ref, s_ref):
    tt, nk = gate_ref.shape

    def tok(t, c):
        x = x_ref[t]
        xlo, xhi = x[0:4], x[4:8]
        for g in range(nk // 8):
            ps = []
            for j in range(8):
                lo, hi = _expert_halves(tbl_ref, idx_ref[t * nk + 8 * g + j])
                ps.append(lo * xlo + hi * xhi)
            q_ref[t, 8 * g:8 * (g + 1), :] = _rowsum8(ps)
        return c

    lax.fori_loop(0, tt, tok, 0)

    def fin(i, c):
        t0 = pl.multiple_of(i * 8, 8)
        s_ref[pl.ds(t0, 8), :] = jnp.concatenate(
            [jnp.sum(q_ref[t0 + j].T, axis=0, keepdims=True) for j in range(8)], axis=0)
        return c

    lax.fori_loop(0, tt // 8, fin, 0)
    o_ref[...] = gate_ref[...] * _gelu_tanh(s_ref[...])


def _peer_out_kernel(idx_ref, w_ref, tbl_ref, h_ref, o_ref):
    tt = h_ref.shape[0]
    nk = idx_ref.shape[0] // tt

    def tok(t, c):
        def grp(g, accs):
            accs = list(accs)
            k0 = t * nk + g * PEER_OUT_UNROLL
            for j in range(PEER_OUT_UNROLL):
                w = w_ref[k0 + j]
                vlo, vhi = _expert_halves(tbl_ref, idx_ref[k0 + j])
                a = 2 * (j % 2)
                accs[a] = accs[a] + vlo * w
                accs[a + 1] = accs[a + 1] + vhi * w
            return tuple(accs)

        z = jnp.zeros((4, 128), F32)
        l0, h0, l1, h1 = lax.fori_loop(0, nk // PEER_OUT_UNROLL, grp, (z, z, z, z))
        o_ref[t] = h_ref[t] + jnp.concatenate([l0 + l1, h0 + h1], axis=0)
        return c

    lax.fori_loop(0, tt, tok, 0)


def _peer_experts(eidx_t, gate_t, xn3, h3, tbl_u, tbl_v):
    t, nk = eidx_t.shape
    tt = min(PEER_TT, t)
    smem = lambda: pl.BlockSpec((tt * nk,), lambda i: (i,), memory_space=pltpu.SMEM)
    vm = lambda: pl.BlockSpec((tt, nk), lambda i: (i, 0))
    tok = lambda: pl.BlockSpec((tt, 8, 128), lambda i: (i, 0, 0))
    tbl = lambda: pl.BlockSpec(tbl_u.shape, lambda i: (0, 0), pipeline_mode=pl.Buffered(1))
    eidx_flat = eidx_t.reshape(t * nk)
    w_t = pl.pallas_call(
        _peer_act_kernel,
        grid=(t // tt,),
        in_specs=[smem(), tok(), vm(), tbl()],
        out_specs=vm(),
        out_shape=jax.ShapeDtypeStruct((t, nk), F32),
        scratch_shapes=[pltpu.VMEM((tt, nk, 128), F32), pltpu.VMEM((tt, nk), F32)],
        compiler_params=_cparams(("arbitrary",), 56),
    )(eidx_flat, xn3, gate_t, tbl_u)
    return pl.pallas_call(
        _peer_out_kernel,
        grid=(t // tt,),
        in_specs=[smem(), smem(), tbl(), tok()],
        out_specs=tok(),
        out_shape=jax.ShapeDtypeStruct(h3.shape, F32),
        compiler_params=_cparams(("arbitrary",), 48),
    )(eidx_flat, w_t.reshape(t * nk), tbl_v, h3)


def _peer_block(h, g, wq, keys, u, v):
    t, d = h.shape
    q, xn = _norm_matmul2(h, g.reshape(1, d), wq.astype(BF16), min(512, t), 1024)
    eidx_t, gate_t = _peer_route(q, keys)
    out = _peer_experts(eidx_t, gate_t, xn.reshape(t, 8, 128), h.reshape(t, 8, 128), _pack_table(u), _pack_table(v))
    return out.reshape(t, d)


def _rmsnorm_kernel(x_ref, g_ref, o_ref):
    x = x_ref[...]
    o_ref[...] = x * lax.rsqrt(jnp.mean(x * x, axis=-1, keepdims=True) + NORM_EPS) * g_ref[...]


def _rmsnorm(x, g, tm=512):
    m, d = x.shape
    return pl.pallas_call(
        _rmsnorm_kernel,
        grid=(m // tm,),
        in_specs=[pl.BlockSpec((tm, d), lambda i: (i, 0)), pl.BlockSpec((1, d), lambda i: (0, 0))],
        out_specs=pl.BlockSpec((tm, d), lambda i: (i, 0)),
        out_shape=jax.ShapeDtypeStruct((m, d), F32),
        compiler_params=_cparams(("arbitrary",)),
    )(x, g.reshape(1, d))


def kernel(x, mix_norm, w_in, w_out, ssm_conv_w, ssm_conv_b, ssm_dt_bias, ssm_a_log, ssm_d, ssm_norm, lru_conv_w, lru_conv_b, lru_wa, lru_ba, lru_wi, lru_bi, lru_lambda, rwkv_mu, rwkv_w0, rwkv_w2, rwkv_a0, rwkv_a2, rwkv_g2, rwkv_kk, rwkv_ka, rwkv_rk, rwkv_ln_w, rwkv_ln_b, nsa_cmp_pos, nsa_cmp_w1, nsa_cmp_w2, ffn_norm, peer_wq, peer_keys, peer_u, peer_v, final_norm):
    bsz, s, d = x.shape
    t = bsz * s
    h = x
    for l in range(w_in.shape[0]):
        w_l = _arrange_cols(w_in[l]).astype(BF16)
        cols = _norm_matmul(h.reshape(t, d), mix_norm[l].reshape(1, d), w_l, 512, 640).reshape(bsz, s, NCOL)
        y_ssm = _ssd_mixer(cols, ssm_conv_w[l], ssm_conv_b[l], ssm_dt_bias[l], ssm_a_log[l], ssm_d[l], ssm_norm[l])
        y_lru = _lru_mixer(cols, lru_conv_w[l], lru_conv_b[l], lru_wa[l], lru_ba[l], lru_wi[l], lru_bi[l], lru_lambda[l])
        y_rw = _rwkv_mixer(cols, rwkv_mu[l], rwkv_w0[l], rwkv_w2[l], rwkv_a0[l], rwkv_a2[l], rwkv_g2[l],
                           rwkv_kk[l], rwkv_ka[l], rwkv_rk[l], rwkv_ln_w[l], rwkv_ln_b[l])
        o_cmp, o_sel, o_win = _nsa_branches(cols, nsa_cmp_pos[l], nsa_cmp_w1[l], nsa_cmp_w2[l])
        h = _mix_out(h, y_ssm, y_lru, y_rw, o_cmp, o_sel, o_win, cols, w_out[l])
        h = _peer_block(h.reshape(t, d), ffn_norm[l], peer_wq[l], peer_keys[l], peer_u[l], peer_v[l]).reshape(bsz, s, d)
    return _rmsnorm(h.reshape(t, d), final_norm).reshape(bsz, s, d)
```

```python
import functools
import math

import numpy as np
import jax
import jax.numpy as jnp
from jax import lax
from jax.experimental import pallas as pl
from jax.experimental.pallas import tpu as pltpu

F32 = jnp.float32
BF16 = jnp.bfloat16
I32 = jnp.int32

NORM_EPS = 1e-6
GROUP_WIDTH = 256
HEAD_DIM = 64
N_HEADS = 4
LRU_C = 8.0
NEG_BIG = -1e30


COL = dict(ssd_z=0, ssd_x=256, lru_gate=512, lru_x=768, rw_r=1024, rw_k=1280, rw_v=1536, rw_wa=1792, rw_g=1920,
           nsa_q=2048, nsa_kvc=2304, nsa_kvs=2432, nsa_kvw=2560, nsa_gate=2688, ssd_b=2816, ssd_c=2944, ssd_dt=3072)
NCOL = 3200


def _col_perm():
    src = np.full((NCOL,), -1, np.int64)

    def put(dst, s0, n):
        src[dst:dst + n] = np.arange(s0, s0 + n)

    put(COL["ssd_z"], 0, 256)
    put(COL["ssd_x"], 256, 256)
    put(COL["ssd_b"], 512, 128)
    put(COL["ssd_c"], 640, 128)
    put(COL["ssd_dt"], 768, 4)
    put(COL["lru_gate"], 772, 256)
    put(COL["lru_x"], 1028, 256)
    put(COL["rw_r"], 1284, 1024)
    put(COL["nsa_q"], 2308, 256)
    put(COL["nsa_kvc"], 2564, 128)
    put(COL["nsa_kvs"], 2692, 128)
    put(COL["nsa_kvw"], 2820, 128)
    put(COL["nsa_gate"], 2948, 12)
    return src


def _arrange_cols(a):
    src = _col_perm()
    out = jnp.take(a, jnp.asarray(np.maximum(src, 0)), axis=-1)
    return jnp.where(jnp.asarray(src >= 0), out, jnp.zeros((), a.dtype))


def _cparams(sem, vmem_mb=None):
    kw = dict(dimension_semantics=sem)
    if vmem_mb is not None:
        kw["vmem_limit_bytes"] = vmem_mb * 1024 * 1024
    return pltpu.CompilerParams(**kw)


def _dot(a, b):
    return jnp.dot(a.astype(BF16), b.astype(BF16), preferred_element_type=F32)


def _dot_nt(a, b):
    return lax.dot_general(a.astype(BF16), b.astype(BF16), (((1,), (1,)), ((), ())), preferred_element_type=F32)


def _dot_tn(a, b):
    return lax.dot_general(a.astype(BF16), b.astype(BF16), (((0,), (0,)), ((), ())), preferred_element_type=F32)


def _split3(x):
    h = x.astype(BF16)
    r = x - h.astype(F32)
    m = r.astype(BF16)
    l = (r - m.astype(F32)).astype(BF16)
    return h, m, l


def _split2(x):
    h = x.astype(BF16)
    return h, (x - h.astype(F32)).astype(BF16)


def _dot_exact_rhs(a, b_exact):
    h, m, l = _split3(a)
    d = lambda t: jnp.dot(t, b_exact, preferred_element_type=F32)
    return d(h) + d(m) + d(l)


def _dot_exact_lhs(a_exact, b):
    h, m, l = _split3(b)
    d = lambda t: jnp.dot(a_exact, t, preferred_element_type=F32)
    return d(h) + d(m) + d(l)


def _sigmoid(x):
    return 1.0 / (1.0 + jnp.exp(-x))


def _silu(x):
    return x * _sigmoid(x)


def _gelu_tanh(x):
    return 0.5 * x * (1.0 + jnp.tanh(math.sqrt(2.0 / math.pi) * (x + 0.044715 * (x * x * x))))


def _softplus(x):
    return jnp.maximum(x, 0.0) + jnp.log1p(jnp.exp(-jnp.abs(x)))


def _conv4(cur, tail, w, b):
    c = cur.shape[1]
    row8 = lax.broadcasted_iota(I32, (8, c), 0)
    acc = cur * w[3:4, :] + b
    for k in (1, 2, 3):
        cr = pltpu.roll(cur, k, axis=0)
        tr = pltpu.roll(tail, k, axis=0)
        head = jnp.where(row8 < k, tr, cr[:8])
        sh = jnp.concatenate([head, cr[8:]], axis=0)
        acc = acc + sh * w[3 - k:4 - k, :]
    return acc


def _norm_matmul_kernel(x_ref, g_ref, w_ref, o_ref, xn_ref):
    @pl.when(pl.program_id(1) == 0)
    def _():
        x = x_ref[...]
        y = x * lax.rsqrt(jnp.mean(x * x, axis=-1, keepdims=True) + NORM_EPS) * g_ref[...]
        xn_ref[...] = y.astype(BF16)

    o_ref[...] = jnp.dot(xn_ref[...], w_ref[...], preferred_element_type=F32)


def _norm_matmul(x, g, w, tm, tn):
    m, k = x.shape
    n = w.shape[1]
    return pl.pallas_call(
        _norm_matmul_kernel,
        grid=(m // tm, n // tn),
        in_specs=[pl.BlockSpec((tm, k), lambda i, j: (i, 0)),
                  pl.BlockSpec((1, k), lambda i, j: (0, 0)),
                  pl.BlockSpec((k, tn), lambda i, j: (0, j))],
        out_specs=pl.BlockSpec((tm, tn), lambda i, j: (i, j)),
        out_shape=jax.ShapeDtypeStruct((m, n), F32),
        scratch_shapes=[pltpu.VMEM((tm, k), BF16)],
        compiler_params=_cparams(("arbitrary", "arbitrary")),
    )(x, g, w)


SSD_CHUNK = 128


def _ssd_kernel(z_ref, x_ref, b_ref, c_ref, dt_ref, cwx_ref, cbx_ref, cwb_ref, cbb_ref, cwc_ref, cbc_ref,
                dtb_ref, alog_ref, d_ref, nw_ref, o_ref, tx_ref, tb_ref, tc_ref, st_ref):
    t = SSD_CHUNK

    @pl.when(pl.program_id(1) == 0)
    def _():
        tx_ref[...] = jnp.zeros_like(tx_ref)
        tb_ref[...] = jnp.zeros_like(tb_ref)
        tc_ref[...] = jnp.zeros_like(tc_ref)
        st_ref[...] = jnp.zeros_like(st_ref)

    xr, br, cr = x_ref[...], b_ref[...], c_ref[...]
    xs = _silu(_conv4(xr, tx_ref[...], cwx_ref[...], cbx_ref[...]))
    bm = _silu(_conv4(br, tb_ref[...], cwb_ref[...], cbb_ref[...]))
    cm = _silu(_conv4(cr, tc_ref[...], cwc_ref[...], cbc_ref[...]))
    tx_ref[...] = xr[t - 8:]
    tb_ref[...] = br[t - 8:]
    tc_ref[...] = cr[t - 8:]

    dt = _softplus(dt_ref[...] + dtb_ref[...])
    a = dt * (-jnp.exp(alog_ref[...]))
    ri = lax.broadcasted_iota(I32, (t, t), 0)
    ci = lax.broadcasted_iota(I32, (t, t), 1)
    tril = ri >= ci
    acs = _dot_exact_lhs(jnp.where(tril, 1.0, 0.0).astype(BF16), a)
    acs_t = acs.T
    d_row = d_ref[...]
    ys = []
    for h in range(N_HEADS):
        g = h // 2
        lo, hi = HEAD_DIM * h, HEAD_DIM * (h + 1)
        b_g = bm[:, HEAD_DIM * g:HEAD_DIM * (g + 1)]
        c_g = cm[:, HEAD_DIM * g:HEAD_DIM * (g + 1)]
        x_h = xs[:, lo:hi]
        acs_c = acs[:, h:h + 1]
        lmat = jnp.exp(jnp.where(tril, acs_c - acs_t[h:h + 1, :], NEG_BIG))
        scores = _dot_nt(c_g, b_g) * lmat
        xdt = x_h * dt[:, h:h + 1]
        st = st_ref[h]
        y = _dot(scores, xdt) + _dot(c_g, st) * jnp.exp(acs_c) + x_h * d_row[:, lo:hi]
        a_last = acs[t - 1:t, h:h + 1]
        bw = b_g * jnp.exp(a_last - acs_c)
        st_ref[h] = st * jnp.exp(a_last) + _dot_tn(bw, xdt)
        ys.append(y)
    y = jnp.concatenate(ys, axis=1) * _silu(z_ref[...])
    outs = []
    for g in range(2):
        yg = y[:, 128 * g:128 * (g + 1)]
        outs.append(yg * lax.rsqrt(jnp.mean(yg * yg, axis=-1, keepdims=True) + NORM_EPS))
    o_ref[...] = jnp.concatenate(outs, axis=1) * nw_ref[...]


def _ssd_mixer(cols, conv_w, conv_b, dt_bias, a_log, d_skip, norm_w):
    bsz, s, _ = cols.shape
    t = SSD_CHUNK
    row = lambda v, n: jnp.pad(v.reshape(1, -1), ((0, 0), (0, n - v.size)))
    full = lambda shp: pl.BlockSpec(shp, lambda b, c: (0,) * len(shp))
    colspec = lambda w, name: pl.BlockSpec((None, t, w), lambda b, c: (b, c, COL[name] // w))
    return pl.pallas_call(
        _ssd_kernel,
        grid=(bsz, s // t),
        in_specs=[colspec(256, "ssd_z"), colspec(256, "ssd_x"), colspec(128, "ssd_b"), colspec(128, "ssd_c"),
                  colspec(128, "ssd_dt"),
                  full((4, 256)), full((1, 256)), full((4, 128)), full((1, 128)), full((4, 128)), full((1, 128)),
                  full((1, 128)), full((1, 128)), full((1, 256)), full((1, 256))],
        out_specs=pl.BlockSpec((None, t, 256), lambda b, c: (b, c, 0)),
        out_shape=jax.ShapeDtypeStruct((bsz, s, 256), F32),
        scratch_shapes=[pltpu.VMEM((8, 256), F32), pltpu.VMEM((8, 128), F32), pltpu.VMEM((8, 128), F32),
                        pltpu.VMEM((N_HEADS, HEAD_DIM, HEAD_DIM), F32)],
        compiler_params=_cparams(("arbitrary", "arbitrary")),
    )(cols, cols, cols, cols, cols,
      conv_w[:, :256], conv_b[:256].reshape(1, 256), conv_w[:, 256:384], conv_b[256:384].reshape(1, 128),
      conv_w[:, 384:512], conv_b[384:512].reshape(1, 128),
      row(dt_bias, 128), row(a_log, 128), jnp.repeat(d_skip, HEAD_DIM).reshape(1, 256), norm_w.reshape(1, 256))


LRU_TILE = 256


def _lru_kernel(gate_ref, xr_ref, cw_ref, cb_ref, wa_ref, ba_ref, wi_ref, bi_ref, lam_ref, o_ref, tail_ref, h_ref):
    t = LRU_TILE

    @pl.when(pl.program_id(1) == 0)
    def _():
        tail_ref[...] = jnp.zeros_like(tail_ref)
        h_ref[...] = jnp.zeros_like(h_ref)

    xr = xr_ref[...]
    xc = _conv4(xr, tail_ref[...], cw_ref[...], cb_ref[...])
    tail_ref[...] = xr[t - 8:]
    r = _sigmoid(_dot(xc, wa_ref[...]) + ba_ref[...])
    i = _sigmoid(_dot(xc, wi_ref[...]) + bi_ref[...])
    log_a = LRU_C * r * (-_softplus(-lam_ref[...]))
    a = jnp.exp(log_a)
    u = jnp.sqrt(-jnp.tanh(log_a) * (a * a + 1.0)) * (i * xc)
    rows = lax.broadcasted_iota(I32, (t, GROUP_WIDTH), 0)
    d = 1
    while d < t:
        a_sh = pltpu.roll(a, d, axis=0)
        u_sh = pltpu.roll(u, d, axis=0)
        m = rows >= d
        u = jnp.where(m, a * u_sh + u, u)
        a = jnp.where(m, a * a_sh, a)
        d *= 2
    h = a * h_ref[7:8, :] + u
    h_ref[...] = h[t - 8:]
    o_ref[...] = h * _gelu_tanh(gate_ref[...])


def _lru_mixer(cols, conv_w, conv_b, wa, ba, wi, bi, lam):
    bsz, s, _ = cols.shape
    t = LRU_TILE
    cb = COL["lru_gate"] // 256
    bd = lambda w: jax.scipy.linalg.block_diag(*[w[k] for k in range(w.shape[0])]).astype(BF16)
    full = lambda shp: pl.BlockSpec(shp, lambda b, c: (0,) * len(shp))
    return pl.pallas_call(
        _lru_kernel,
        grid=(bsz, s // t),
        in_specs=[pl.BlockSpec((None, t, 256), lambda b, c: (b, c, cb)),
                  pl.BlockSpec((None, t, 256), lambda b, c: (b, c, cb + 1)),
                  full((4, 256)), full((1, 256)), full((256, 256)), full((1, 256)), full((256, 256)), full((1, 256)),
                  full((1, 256))],
        out_specs=pl.BlockSpec((None, t, 256), lambda b, c: (b, c, 0)),
        out_shape=jax.ShapeDtypeStruct((bsz, s, 256), F32),
        scratch_shapes=[pltpu.VMEM((8, 256), F32), pltpu.VMEM((8, 256), F32)],
        compiler_params=_cparams(("arbitrary", "arbitrary")),
    )(cols, cols, conv_w, conv_b.reshape(1, 256), bd(wa), ba.reshape(1, 256), bd(wi), bi.reshape(1, 256),
      lam.reshape(1, 256))


RW_TILE = 128
RW_GROUP = 4
RWKV_GN_EPS = 64e-5


def _segsum_bc(x):
    r = x.shape[0]
    m = lax.broadcasted_iota(I32, (r, 128), 1) < HEAD_DIM
    outs = []
    for j in range(2):
        xj = x[:, 128 * j:128 * (j + 1)]
        lo = jnp.sum(jnp.where(m, xj, 0.0), axis=1, keepdims=True)
        hi = jnp.sum(jnp.where(m, 0.0, xj), axis=1, keepdims=True)
        outs.append(jnp.where(m, lo, hi))
    return jnp.concatenate(outs, axis=1)


def _rwkv_kernel(c_ref, mu_ref, w0_ref, w2_ref, a0_ref, a2_ref, g2_ref, kkp_ref, ka_ref, rk_ref, lnw_ref, lnb_ref, e_ref,
                 o_ref, prev_ref, s_ref, r_s, w_s, k_s, v_s, kk_s, b_s, g_s, bon_s, y_s):
    nb = c_ref.shape[0]
    t = RW_TILE

    @pl.when(pl.program_id(0) == 0)
    def _():
        prev_ref[...] = jnp.zeros_like(prev_ref)
        s_ref[...] = jnp.zeros_like(s_ref)
        r_s[...] = jnp.zeros_like(r_s)

    row = lax.broadcasted_iota(I32, (t, 1024), 0)
    for b in range(nb):
        c = c_ref[b]
        pr = jnp.where(row == 0, prev_ref[b, 7:8, :], pltpu.roll(c, 1, axis=0))
        prev_ref[b] = c[t - 8:]
        c = c + (pr - c) * mu_ref[...]
        r, k, v = c[:, 0:256], c[:, 256:512], c[:, 512:768]
        wd, ad, gd = c[:, 768:832], c[:, 832:896], c[:, 896:1024]
        w_log = -_softplus(-(w0_ref[...] + _dot(jnp.tanh(wd), w2_ref[...]))) - 0.5
        a = _sigmoid(a0_ref[...] + _dot(ad, a2_ref[...]))
        kk = k * kkp_ref[...]
        kk = kk / jnp.maximum(jnp.sqrt(_segsum_bc(kk * kk)), 1e-12)
        k2 = k * (1.0 + (a - 1.0) * ka_ref[...])
        r_s[b, 8:, :] = r
        w_s[b] = jnp.exp(-jnp.exp(w_log))
        k_s[b] = k2
        v_s[b] = v
        kk_s[b] = kk
        b_s[b] = kk * a
        g_s[b] = _dot(_sigmoid(gd), g2_ref[...])
        bon_s[b] = _segsum_bc(r * k2 * rk_ref[...]) * v

    gb = RW_GROUP if nb % RW_GROUP == 0 else nb
    nr = gb * HEAD_DIM
    eye4 = jnp.where(lax.broadcasted_iota(I32, (nr, 256), 0) % HEAD_DIM
                     == lax.broadcasted_iota(I32, (nr, 256), 1) % HEAD_DIM, 1.0, 0.0)

    def seg(*xs):
        parts = []
        for x in xs:
            parts.extend(_split2(x))
        out = jnp.dot(jnp.concatenate(parts, axis=0), e_ref[...], preferred_element_type=F32)
        return [out[2 * i * nr:(2 * i + 1) * nr] + out[(2 * i + 1) * nr:(2 * i + 2) * nr] for i in range(len(xs))]

    def rows(ref, b0, i):
        return jnp.concatenate([jnp.broadcast_to(ref[b0 + b, pl.ds(i, 1), :], (HEAD_DIM, 256)) for b in range(gb)],
                               axis=0)

    def put_y(ybc, b0, i):
        yd = ybc * eye4
        for b in range(gb):
            y_s[b0 + b, pl.ds(i, 1), :] = jnp.sum(yd[b * HEAD_DIM:(b + 1) * HEAD_DIM], axis=0, keepdims=True)

    def step(tt, carry):
        for b0 in range(0, nb, gb):
            sl = slice(b0 * HEAD_DIM, (b0 + gb) * HEAD_DIM)
            s = s_ref[sl, :]
            vbc = _segsum_bc(eye4 * rows(v_s, b0, tt))
            sa, ybc = seg(s * rows(kk_s, b0, tt), s * rows(r_s, b0, tt + 7))
            put_y(ybc, b0, tt + 7)
            s_ref[sl, :] = s * rows(w_s, b0, tt) - sa * rows(b_s, b0, tt) + vbc * rows(k_s, b0, tt)
        return carry

    lax.fori_loop(0, t, step, 0)
    for b0 in range(0, nb, gb):
        ybc, = seg(s_ref[b0 * HEAD_DIM:(b0 + gb) * HEAD_DIM, :] * rows(r_s, b0, t + 7))
        put_y(ybc, b0, t + 7)

    for b in range(nb):
        y = y_s[b, 8:, :]
        d = y - _segsum_bc(y) * (1.0 / HEAD_DIM)
        var = _segsum_bc(d * d) * (1.0 / HEAD_DIM)
        y = d * lax.rsqrt(var + RWKV_GN_EPS) * lnw_ref[...] + lnb_ref[...] + bon_s[b]
        o_ref[b] = y * g_s[b]


def _rwkv_mixer(cols, mu, w0, w2, a0, a2, g2, k_k, k_a, r_k, ln_w, ln_b):
    bsz, s, _ = cols.shape
    t = RW_TILE
    full = lambda shp: pl.BlockSpec(shp, lambda c: (0,) * len(shp))
    r1 = lambda v: v.reshape(1, -1)
    tile = lambda: pltpu.VMEM((bsz, t, 256), F32)
    return pl.pallas_call(
        _rwkv_kernel,
        grid=(s // t,),
        in_specs=[pl.BlockSpec((bsz, t, 1024), lambda c: (0, c, COL["rw_r"] // 1024)),
                  full((1, 1024)), full((1, 256)), full((64, 256)), full((1, 256)), full((64, 256)), full((128, 256)),
                  full((1, 256)), full((1, 256)), full((1, 256)), full((1, 256)), full((1, 256)), full((256, 256))],
        out_specs=pl.BlockSpec((bsz, t, 256), lambda c: (0, c, 0)),
        out_shape=jax.ShapeDtypeStruct((bsz, s, 256), F32),
        scratch_shapes=([pltpu.VMEM((bsz, 8, 1024), F32), pltpu.VMEM((bsz * HEAD_DIM, 256), F32),
                         pltpu.VMEM((bsz, t + 8, 256), F32)] + [tile() for _ in range(7)]
                        + [pltpu.VMEM((bsz, t + 8, 256), F32)]),
        compiler_params=_cparams(("arbitrary",)),
    )(cols, r1(mu), r1(w0), w2, r1(a0), a2, g2, r1(k_k), r1(k_a), r1(r_k), r1(ln_w), r1(ln_b),
      jnp.asarray(np.kron(np.eye(N_HEADS), np.ones((HEAD_DIM, HEAD_DIM))), BF16))


NSA_STRIDE = 16
NSA_SEL_BLOCK = 64
NSA_TOPN = 16
NSA_WINDOW = 512
NSA_FORCE = 1e4
NSA_TQ = 256
NSA_TK = 512
NSA_SLOPES = tuple(2.0 ** (-8.0 * (h + 1) / N_HEADS) for h in range(N_HEADS))
SEL_OFF = -float(2 ** 30)


def _dot3_nt(a, b):
    ah, al = _split2(a)
    bh, bl = _split2(b)
    d = lambda x, y: lax.dot_general(x, y, (((1,), (1,)), ((), ())), preferred_element_type=F32)
    return d(ah, bh) + d(ah, bl) + d(al, bh)


def _nsa_compress_kernel(kv_ref, pos_ref, w1_ref, w2_ref, o_ref):
    nbp = o_ref.shape[0]
    outs = []
    for c in range(2):
        acc_a = jnp.zeros((nbp, 256), F32)
        acc_b = jnp.zeros((nbp, 256), F32)
        for r in range(NSA_STRIDE):
            xr = kv_ref[pl.ds(r, nbp, stride=NSA_STRIDE), :][:, HEAD_DIM * c:HEAD_DIM * (c + 1)]
            acc_a = acc_a + _dot(xr + pos_ref[c, r:r + 1, :], w1_ref[c, 64 * r:64 * (r + 1), :])
            acc_b = acc_b + _dot(xr + pos_ref[c, 16 + r:17 + r, :], w1_ref[c, 1024 + 64 * r:1024 + 64 * (r + 1), :])
        hdn = _gelu_tanh(acc_a + pltpu.roll(acc_b, nbp - 1, axis=0))
        outs.append(_dot(hdn, w2_ref[c]))
    o_ref[...] = jnp.concatenate(outs, axis=1)


def _nsa_cmp_kernel(q_ref, kv_ref, ov_ref, ocmp_ref, qaug_ref, used_ref):
    tq = NSA_TQ
    nbp = kv_ref.shape[0]
    q0 = pl.program_id(1) * tq
    kv = kv_ref[...]
    kc, vc = kv[:, :HEAD_DIM], kv[:, HEAD_DIM:]
    pos = q0 + lax.broadcasted_iota(I32, (tq, nbp), 0)
    blk_end = lax.broadcasted_iota(I32, (tq, nbp), 1) * NSA_STRIDE + (2 * NSA_STRIDE - 1)
    dist = pos - blk_end
    mask = dist >= 0
    distf = dist.astype(F32)
    q = q_ref[...] * (HEAD_DIM ** -0.5)
    lane = lax.broadcasted_iota(I32, (tq, 128), 1)
    psum = jnp.zeros((tq, nbp), F32)
    for h in range(N_HEADS):
        qh = q[:, HEAD_DIM * h:HEAD_DIM * (h + 1)]
        s = jnp.where(mask, _dot3_nt(qh, kc) - NSA_SLOPES[h] * distf, -jnp.inf)
        m = jnp.max(s, axis=1, keepdims=True)
        m = jnp.where(m == -jnp.inf, 0.0, m)
        e = jnp.where(mask, jnp.exp(s - m), 0.0)
        p = e / jnp.maximum(jnp.sum(e, axis=1, keepdims=True), 1e-30)
        ocmp_ref[h] = _dot(p, vc)
        psum = psum + p
        qt = q[:, 128 * (h // 2):128 * (h // 2 + 1)]
        keep = (lane < HEAD_DIM) if h % 2 == 0 else (lane >= HEAD_DIM)
        qaug_ref[h, :, 0:128] = jnp.where(keep, qt, 0.0).astype(BF16)
    imp_t = _dot_exact_rhs(psum, ov_ref[...]).T
    j = lax.broadcasted_iota(I32, (128, tq), 0)
    pos_t = q0 + lax.broadcasted_iota(I32, (128, tq), 1)
    cur = pos_t // NSA_SEL_BLOCK
    forced = (j == 0) | (j == cur) | (j == cur - 1)
    valid = j * NSA_SEL_BLOCK <= pos_t
    x = jnp.where(valid, jnp.where(forced, NSA_FORCE, imp_t), -jnp.inf)
    jf = j.astype(F32)
    picked = jnp.zeros((128, tq), F32)
    for _ in range(NSA_TOPN):
        mx = jnp.max(x, axis=0, keepdims=True)
        first = jnp.min(jnp.where(x == mx, jf, 128.0), axis=0, keepdims=True)
        hit = jf == first
        picked = jnp.where(hit, 1.0, picked)
        x = jnp.where(hit, -jnp.inf, x)
    used = jnp.where(valid & (picked > 0.0), 1.0, 0.0)
    bias = jnp.where(used > 0.0, 0.0, SEL_OFF).T
    for h in range(N_HEADS):
        qaug_ref[h, :, 128:256] = bias.astype(BF16)
    used_ref[...] = jnp.broadcast_to(jnp.max(used.T, axis=0, keepdims=True), (8, 128))


def _nsa_flash_kernel(need_ref, qaug_ref, kv_ref, o_ref, kaug_s, v_s, *, window):
    tq, tk = NSA_TQ, NSA_TK
    s_len = kv_ref.shape[0]
    tk = min(tk, s_len)
    kdim = 128 if window else 256
    qi = pl.program_id(1)
    q0 = qi * tq

    @pl.when(qi == 0)
    def _():
        def build(i, c):
            r0 = pl.multiple_of(i * tk, tk)
            kv = kv_ref[pl.ds(r0, tk), :]
            k = kv[:, :HEAD_DIM]
            parts = [k, k]
            if not window:
                blk = (r0 + lax.broadcasted_iota(I32, (tk, 128), 0)) // NSA_SEL_BLOCK
                parts.append(jnp.where(blk == lax.broadcasted_iota(I32, (tk, 128), 1), 1.0, 0.0))
            kaug_s[pl.ds(r0, tk), :] = jnp.concatenate(parts, axis=1).astype(BF16)
            v_s[pl.ds(r0, tk), :] = kv[:, HEAD_DIM:].astype(BF16)
            return c
        lax.fori_loop(0, s_len // tk, build, 0)

    q = qaug_ref[...][:, :, :kdim].reshape(N_HEADS * tq, kdim)
    row = lax.broadcasted_iota(I32, (N_HEADS * tq, tk), 0)
    t_q = q0 + (row & (tq - 1))
    head = lax.broadcasted_iota(I32, (N_HEADS * tq, 1), 0) // tq
    slope = jnp.where(head == 0, NSA_SLOPES[0], jnp.where(head == 1, NSA_SLOPES[1],
                      jnp.where(head == 2, NSA_SLOPES[2], NSA_SLOPES[3])))
    col = lax.broadcasted_iota(I32, (N_HEADS * tq, tk), 1)

    n_kt = s_len // tk
    need_base = (pl.program_id(0) * pl.num_programs(1) + qi) * n_kt

    def body(kt, carry):
        if window:
            return tile(kt, carry)
        return lax.cond(need_ref[need_base + kt] != 0, lambda c: tile(kt, c), lambda c: c, carry)

    def tile(kt, carry):
        m, l, acc = carry
        k0 = pl.multiple_of(kt * tk, tk)
        s = lax.dot_general(q, kaug_s[pl.ds(k0, tk), :], (((1,), (1,)), ((), ())), preferred_element_type=F32)
        t_k = k0 + col
        s = s + slope * (t_k - q0).astype(F32)
        ok = t_k <= t_q
        if window:
            ok = ok & (t_q - t_k < NSA_WINDOW)
        m_new = jnp.maximum(m, jnp.max(jnp.where(ok, s, NEG_BIG), axis=1, keepdims=True))
        p = jnp.where(ok, jnp.exp(s - m_new), 0.0)
        alpha = jnp.exp(m - m_new)
        l = alpha * l + jnp.sum(p, axis=1, keepdims=True)
        acc = alpha * acc + jnp.dot(p.astype(BF16), v_s[pl.ds(k0, tk), :], preferred_element_type=F32)
        return m_new, l, acc

    lo = jnp.maximum(q0 - NSA_WINDOW, 0) // tk if window else 0
    hi = (q0 + tq - 1) // tk + 1
    init = (jnp.full((N_HEADS * tq, 1), NEG_BIG, F32), jnp.zeros((N_HEADS * tq, 1), F32),
            jnp.zeros((N_HEADS * tq, HEAD_DIM), F32))
    m, l, acc = lax.fori_loop(lo, hi, body, init)
    o_ref[...] = (acc / l).reshape(N_HEADS, tq, HEAD_DIM)


def _nsa_overlap(s):
    nbp, nsel = s // NSA_STRIDE, s // NSA_SEL_BLOCK
    cs = np.arange(nbp) * NSA_STRIDE
    ss = np.arange(128) * NSA_SEL_BLOCK
    ov = np.minimum(cs[:, None] + 2 * NSA_STRIDE, ss[None, :] + NSA_SEL_BLOCK) - np.maximum(cs[:, None], ss[None, :])
    ov = np.clip(ov, 0, None) / (2 * NSA_STRIDE)
    ov[nbp - 1:, :] = 0.0
    ov[:, nsel:] = 0.0
    return jnp.asarray(ov, BF16)


def _nsa_branches(cols, cmp_pos, cmp_w1, cmp_w2):
    bsz, s, _ = cols.shape
    nbp = s // NSA_STRIDE
    tq = NSA_TQ
    kvcmp = pl.pallas_call(
        _nsa_compress_kernel,
        grid=(bsz,),
        in_specs=[pl.BlockSpec((None, s, 128), lambda b: (b, 0, COL["nsa_kvc"] // 128)),
                  pl.BlockSpec((2, 32, 64), lambda b: (0, 0, 0)),
                  pl.BlockSpec((2, 2048, 256), lambda b: (0, 0, 0)),
                  pl.BlockSpec((2, 256, 64), lambda b: (0, 0, 0))],
        out_specs=pl.BlockSpec((None, nbp, 128), lambda b: (b, 0, 0)),
        out_shape=jax.ShapeDtypeStruct((bsz, nbp, 128), F32),
        compiler_params=_cparams(("arbitrary",), 48),
    )(cols, cmp_pos, cmp_w1.astype(BF16), cmp_w2)
    head_out = lambda dt, w: (pl.BlockSpec((None, N_HEADS, tq, w), lambda b, i: (b, 0, i, 0)),
                              jax.ShapeDtypeStruct((bsz, N_HEADS, s, w), dt))
    (oc_spec, oc_shape), (qa_spec, qa_shape) = head_out(F32, HEAD_DIM), head_out(BF16, 256)
    nqt = s // tq
    o_cmp, qaug, used = pl.pallas_call(
        _nsa_cmp_kernel,
        grid=(bsz, nqt),
        in_specs=[pl.BlockSpec((None, tq, 256), lambda b, i: (b, i, COL["nsa_q"] // 256)),
                  pl.BlockSpec((None, nbp, 128), lambda b, i: (b, 0, 0)),
                  pl.BlockSpec((nbp, 128), lambda b, i: (0, 0))],
        out_specs=[oc_spec, qa_spec, pl.BlockSpec((None, None, 8, 128), lambda b, i: (b, i, 0, 0))],
        out_shape=[oc_shape, qa_shape, jax.ShapeDtypeStruct((bsz, nqt, 8, 128), F32)],
        compiler_params=_cparams(("arbitrary", "arbitrary"), 48),
    )(cols, kvcmp, _nsa_overlap(s))
    tk = min(NSA_TK, s)
    bpt = tk // NSA_SEL_BLOCK
    need = (used[:, :, 0, :(s // tk) * bpt].reshape(bsz, nqt, s // tk, bpt).max(axis=-1) > 0).astype(I32).reshape(-1)

    def flash(name, window):
        hmap = lambda b, i, need_ref: (b, 0, i, 0)
        return pl.pallas_call(
            functools.partial(_nsa_flash_kernel, window=window),
            grid_spec=pltpu.PrefetchScalarGridSpec(
                num_scalar_prefetch=1,
                grid=(bsz, nqt),
                in_specs=[pl.BlockSpec((None, N_HEADS, tq, 256), hmap),
                          pl.BlockSpec((None, s, 128), lambda b, i, need_ref: (b, 0, COL[name] // 128))],
                out_specs=pl.BlockSpec((None, N_HEADS, tq, HEAD_DIM), hmap),
                scratch_shapes=[pltpu.VMEM((s, 128 if window else 256), BF16), pltpu.VMEM((s, HEAD_DIM), BF16)]),
            out_shape=oc_shape,
            compiler_params=_cparams(("arbitrary", "arbitrary"), 48),
        )(need, qaug, cols)

    return o_cmp, flash("nsa_kvs", False), flash("nsa_kvw", True)


def _nsa_mixer(cols, cmp_pos, cmp_w1, cmp_w2):
    o_cmp, o_sel, o_win = _nsa_branches(cols, cmp_pos, cmp_w1, cmp_w2)
    bsz, s, _ = cols.shape
    g = jax.nn.sigmoid(cols[..., COL["nsa_gate"]:COL["nsa_gate"] + 12]).reshape(bsz, s, N_HEADS, 3)
    tr = lambda o: jnp.moveaxis(o, 1, 2)
    out = g[..., 0:1] * tr(o_cmp) + g[..., 1:2] * tr(o_sel) + g[..., 2:3] * tr(o_win)
    return out.reshape(bsz, s, 256)


MIX_TILE = 256


def _mix_out_kernel(h_ref, ssm_ref, lru_ref, rw_ref, oc_ref, os_ref, ow_ref, gate_ref, w_ref, o_ref):
    acc = h_ref[...]
    acc = acc + _dot(ssm_ref[...], w_ref[0:256, :])
    acc = acc + _dot(lru_ref[...], w_ref[256:512, :])
    acc = acc + _dot(rw_ref[...], w_ref[512:768, :])
    g = _sigmoid(gate_ref[...])
    for h in range(N_HEADS):
        y = (g[:, 3 * h:3 * h + 1] * oc_ref[h] + g[:, 3 * h + 1:3 * h + 2] * os_ref[h]
             + g[:, 3 * h + 2:3 * h + 3] * ow_ref[h])
        acc = acc + _dot(y, w_ref[768 + HEAD_DIM * h:768 + HEAD_DIM * (h + 1), :])
    o_ref[...] = acc


def _mix_out(h, y_ssm, y_lru, y_rw, o_cmp, o_sel, o_win, cols, w_out):
    bsz, s, d = h.shape
    t = MIX_TILE
    tok = lambda w: pl.BlockSpec((None, t, w), lambda b, i: (b, i, 0))
    hd = pl.BlockSpec((None, N_HEADS, t, HEAD_DIM), lambda b, i: (b, 0, i, 0))
    return pl.pallas_call(
        _mix_out_kernel,
        grid=(bsz, s // t),
        in_specs=[tok(d), tok(256), tok(256), tok(256), hd, hd, hd,
                  pl.BlockSpec((None, t, 128), lambda b, i: (b, i, COL["nsa_gate"] // 128)),
                  pl.BlockSpec((4 * GROUP_WIDTH, d), lambda b, i: (0, 0))],
        out_specs=tok(d),
        out_shape=jax.ShapeDtypeStruct((bsz, s, d), F32),
        compiler_params=_cparams(("arbitrary", "arbitrary")),
    )(h, y_ssm, y_lru, y_rw, o_cmp, o_sel, o_win, cols, w_out.astype(BF16))


PEER_HEADS = 8
PEER_NKEYS = 128
PEER_TOPK = 16
PEER_TT = 128
PEER_TQ = 256
PEER_ACT_UNROLL = 16
PEER_OUT_UNROLL = 128


PEER_ROW = 4


def _store_packed(x, o_ref):
    rows, d = x.shape
    b = pltpu.bitcast(x, I32)
    r = b + 0x7FFF + ((b >> 16) & 1)
    word = (r[:, d // 2:] & jnp.int32(-65536)) | lax.shift_right_logical(r[:, :d // 2], 16)
    for s in range(PEER_ROW):
        o_ref[pl.ds(s, rows, stride=PEER_ROW), :] = word[:, 128 * s:128 * (s + 1)]


def _norm_matmul2_kernel(x_ref, g_ref, w_ref, o_ref, xpk_ref, xn_ref):
    @pl.when(pl.program_id(1) == 0)
    def _():
        x = x_ref[...]
        y = x * lax.rsqrt(jnp.mean(x * x, axis=-1, keepdims=True) + NORM_EPS) * g_ref[...]
        _store_packed(y, xpk_ref)
        xn_ref[...] = y.astype(BF16)

    o_ref[...] = jnp.dot(xn_ref[...], w_ref[...], preferred_element_type=F32)


def _norm_matmul2(x, g, w, tm, tn):
    m, k = x.shape
    n = w.shape[1]
    return pl.pallas_call(
        _norm_matmul2_kernel,
        grid=(m // tm, n // tn),
        in_specs=[pl.BlockSpec((tm, k), lambda i, j: (i, 0)),
                  pl.BlockSpec((1, k), lambda i, j: (0, 0)),
                  pl.BlockSpec((k, tn), lambda i, j: (0, j))],
        out_specs=[pl.BlockSpec((tm, tn), lambda i, j: (i, j)),
                   pl.BlockSpec((tm * PEER_ROW, 128), lambda i, j: (i, 0))],
        out_shape=[jax.ShapeDtypeStruct((m, n), F32), jax.ShapeDtypeStruct((m * PEER_ROW, 128), I32)],
        scratch_shapes=[pltpu.VMEM((tm, k), BF16)],
        compiler_params=_cparams(("arbitrary", "arbitrary")),
    )(x, g, w)


def _top_rows(x, n, payload=None):
    r = x.shape[0]
    rid = lax.broadcasted_iota(I32, x.shape, 0).astype(F32)
    vals, ids, pays = [], [], []
    for _ in range(n):
        mx = jnp.max(x, axis=0, keepdims=True)
        first = jnp.min(jnp.where(x == mx, rid, float(r)), axis=0, keepdims=True)
        hit = rid == first
        vals.append(mx)
        ids.append(first)
        if payload is not None:
            pays.append(jnp.sum(jnp.where(hit, payload, 0.0), axis=0, keepdims=True))
        x = jnp.where(hit, -jnp.inf, x)
    cat = lambda v: jnp.concatenate(v, axis=0)
    return (cat(vals), cat(ids)) + ((cat(pays),) if payload is not None else ())


def _peer_route_kernel(q_ref, keys_ref, eidx_ref, gate_ref):
    k = PEER_TOPK
    tq = q_ref.shape[0]
    gates, eids = [], []
    for h in range(PEER_HEADS):
        s, i = [], []
        for c in range(2):
            qh = q_ref[:, 128 * (2 * h + c):128 * (2 * h + c + 1)]
            sv, si = _top_rows(_dot3_nt(keys_ref[h, c], qh), k)
            s.append(sv)
            i.append(si)
        cand, cidx = [], []
        for a in range(k):
            nb = k // (a + 1)
            cand.append(s[0][a:a + 1, :] + s[1][:nb])
            cidx.append(i[0][a:a + 1, :] * float(PEER_NKEYS) + i[1][:nb])
        n_c = sum(c.shape[0] for c in cand)
        pad = (-n_c) % 8
        cand.append(jnp.full((pad, tq), -jnp.inf, F32))
        cidx.append(jnp.zeros((pad, tq), F32))
        top, _, eidx = _top_rows(jnp.concatenate(cand, axis=0), k, payload=jnp.concatenate(cidx, axis=0))
        e = jnp.exp(top - top[0:1, :])
        gates.append(e / jnp.sum(e, axis=0, keepdims=True))
        eids.append(eidx)
    gate_ref[...] = jnp.concatenate(gates, axis=0).T
    eidx_ref[...] = (jnp.concatenate(eids, axis=0).T * float(PEER_ROW)).astype(I32)


def _peer_route(q, keys):
    t = q.shape[0]
    tq = min(PEER_TQ, t)
    nk = PEER_HEADS * PEER_TOPK
    return pl.pallas_call(
        _peer_route_kernel,
        grid=(t // tq,),
        in_specs=[pl.BlockSpec((tq, q.shape[1]), lambda i: (i, 0)),
                  pl.BlockSpec(keys.shape, lambda i: (0, 0, 0, 0))],
        out_specs=[pl.BlockSpec((tq, nk), lambda i: (i, 0)), pl.BlockSpec((tq, nk), lambda i: (i, 0))],
        out_shape=[jax.ShapeDtypeStruct((t, nk), I32), jax.ShapeDtypeStruct((t, nk), F32)],
        compiler_params=_cparams(("arbitrary",), 48),
    )(q, keys)


PACK_TILE = 512


def _pack_kernel(t_ref, o_ref):
    _store_packed(t_ref[...], o_ref)


def _pack_table(tab):
    e, d = tab.shape
    te = PACK_TILE
    return pl.pallas_call(
        _pack_kernel,
        grid=(e // te,),
        in_specs=[pl.BlockSpec((te, d), lambda i: (i, 0))],
        out_specs=pl.BlockSpec((te * PEER_ROW, 128), lambda i: (i, 0)),
        out_shape=jax.ShapeDtypeStruct((e * PEER_ROW, 128), I32),
        compiler_params=_cparams(("arbitrary",)),
    )(tab)


def _expert_halves(tbl_ref, r):
    word = tbl_ref[pl.ds(pl.multiple_of(r, PEER_ROW), PEER_ROW), :]
    return pltpu.bitcast(word << 16, F32), pltpu.bitcast(word & jnp.int32(-65536), F32)


_BITREV8 = (0, 4, 2, 6, 1, 5, 3, 7)


def _rowsum8(ps):
    sub = lax.broadcasted_iota(I32, (8, 128), 0)
    q = [ps[_BITREV8[p]] for p in range(8)]
    cur = [jnp.concatenate([q[2 * i], q[2 * i + 1]], axis=0) for i in range(4)]
    for h in (2, 1):
        m = (sub & h) == 0
        cur = [jnp.where(m, x, y) + jnp.where(m, pltpu.roll(x, 8 - h, axis=0), pltpu.roll(y, h, axis=0))
               for x, y in zip(cur[0::2], cur[1::2])]
    return cur[0]


def _peer_act_kernel(idx_ref, x_ref, gate_ref, tbl_ref, o_ref, q_ref, s_ref):
    tt, nk = gate_ref.shape

    def tok(t, c):
        xb = pltpu.bitcast(x_ref[pl.ds(pl.multiple_of(t * PEER_ROW, PEER_ROW), PEER_ROW), :], BF16)
        for g in range(nk // 8):
            ps = []
            for j in range(8):
                r = pl.multiple_of(idx_ref[t * nk + 8 * g + j], PEER_ROW)
                prod = pltpu.bitcast(pltpu.bitcast(tbl_ref[pl.ds(r, PEER_ROW), :], BF16) * xb, I32)
                ps.append(pltpu.bitcast(prod << 16, F32) + pltpu.bitcast(prod & jnp.int32(-65536), F32))
            q_ref[t, 8 * g:8 * (g + 1), :] = _rowsum8(ps)
        return c

    lax.fori_loop(0, tt, tok, 0)

    def fin(i, c):
        t0 = pl.multiple_of(i * 8, 8)
        s_ref[pl.ds(t0, 8), :] = jnp.concatenate(
            [jnp.sum(q_ref[t0 + j].T, axis=0, keepdims=True) for j in range(8)], axis=0)
        return c

    lax.fori_loop(0, tt // 8, fin, 0)
    o_ref[...] = gate_ref[...] * _gelu_tanh(s_ref[...])


def _peer_out_kernel(idx_ref, w_ref, tbl_ref, h_ref, o_ref):
    tt = h_ref.shape[0]
    nk = idx_ref.shape[0] // tt

    def tok(t, c):
        def grp(g, accs):
            accs = list(accs)
            k0 = t * nk + g * PEER_OUT_UNROLL
            for j in range(PEER_OUT_UNROLL):
                w = w_ref[k0 + j]
                vlo, vhi = _expert_halves(tbl_ref, idx_ref[k0 + j])
                a = 2 * (j % 2)
                accs[a] = accs[a] + vlo * w
                accs[a + 1] = accs[a + 1] + vhi * w
            return tuple(accs)

        z = jnp.zeros((PEER_ROW, 128), F32)
        l0, h0, l1, h1 = lax.fori_loop(0, nk // PEER_OUT_UNROLL, grp, (z, z, z, z))
        o_ref[t] = h_ref[t] + jnp.concatenate([l0 + l1, h0 + h1], axis=0)
        return c

    lax.fori_loop(0, tt, tok, 0)


def _peer_experts(eidx_t, gate_t, xn3, h3, tbl_u, tbl_v):
    t, nk = eidx_t.shape
    tt = min(PEER_TT, t)
    smem = lambda: pl.BlockSpec((tt * nk,), lambda i: (i,), memory_space=pltpu.SMEM)
    vm = lambda: pl.BlockSpec((tt, nk), lambda i: (i, 0))
    tok = lambda: pl.BlockSpec((tt, 8, 128), lambda i: (i, 0, 0))
    tbl = lambda: pl.BlockSpec(tbl_u.shape, lambda i: (0, 0), pipeline_mode=pl.Buffered(1))
    eidx_flat = eidx_t.reshape(t * nk)
    w_t = pl.pallas_call(
        _peer_act_kernel,
        grid=(t // tt,),
        in_specs=[smem(), pl.BlockSpec((tt * PEER_ROW, 128), lambda i: (i, 0)), vm(), tbl()],
        out_specs=vm(),
        out_shape=jax.ShapeDtypeStruct((t, nk), F32),
        scratch_shapes=[pltpu.VMEM((tt, nk, 128), F32), pltpu.VMEM((tt, nk), F32)],
        compiler_params=_cparams(("arbitrary",), 56),
    )(eidx_flat, xn3, gate_t, tbl_u)
    return pl.pallas_call(
        _peer_out_kernel,
        grid=(t // tt,),
        in_specs=[smem(), smem(), tbl(), tok()],
        out_specs=tok(),
        out_shape=jax.ShapeDtypeStruct(h3.shape, F32),
        compiler_params=_cparams(("arbitrary",), 48),
    )(eidx_flat, w_t.reshape(t * nk), tbl_v, h3)


def _peer_block(h, g, wq, keys, u, v):
    t, d = h.shape
    q, xpk = _norm_matmul2(h, g.reshape(1, d), wq.astype(BF16), min(512, t), 1024)
    eidx_t, gate_t = _peer_route(q, keys)
    out = _peer_experts(eidx_t, gate_t, xpk, h.reshape(t, 8, 128), _pack_table(u), _pack_table(v))
    return out.reshape(t, d)


def _rmsnorm_kernel(x_ref, g_ref, o_ref):
    x = x_ref[...]
    o_ref[...] = x * lax.rsqrt(jnp.mean(x * x, axis=-1, keepdims=True) + NORM_EPS) * g_ref[...]


def _rmsnorm(x, g, tm=512):
    m, d = x.shape
    return pl.pallas_call(
        _rmsnorm_kernel,
        grid=(m // tm,),
        in_specs=[pl.BlockSpec((tm, d), lambda i: (i, 0)), pl.BlockSpec((1, d), lambda i: (0, 0))],
        out_specs=pl.BlockSpec((tm, d), lambda i: (i, 0)),
        out_shape=jax.ShapeDtypeStruct((m, d), F32),
        compiler_params=_cparams(("arbitrary",)),
    )(x, g.reshape(1, d))


def kernel(x, mix_norm, w_in, w_out, ssm_conv_w, ssm_conv_b, ssm_dt_bias, ssm_a_log, ssm_d, ssm_norm, lru_conv_w, lru_conv_b, lru_wa, lru_ba, lru_wi, lru_bi, lru_lambda, rwkv_mu, rwkv_w0, rwkv_w2, rwkv_a0, rwkv_a2, rwkv_g2, rwkv_kk, rwkv_ka, rwkv_rk, rwkv_ln_w, rwkv_ln_b, nsa_cmp_pos, nsa_cmp_w1, nsa_cmp_w2, ffn_norm, peer_wq, peer_keys, peer_u, peer_v, final_norm):
    bsz, s, d = x.shape
    t = bsz * s
    h = x
    for l in range(w_in.shape[0]):
        w_l = _arrange_cols(w_in[l]).astype(BF16)
        cols = _norm_matmul(h.reshape(t, d), mix_norm[l].reshape(1, d), w_l, 512, 640).reshape(bsz, s, NCOL)
        y_ssm = _ssd_mixer(cols, ssm_conv_w[l], ssm_conv_b[l], ssm_dt_bias[l], ssm_a_log[l], ssm_d[l], ssm_norm[l])
        y_lru = _lru_mixer(cols, lru_conv_w[l], lru_conv_b[l], lru_wa[l], lru_ba[l], lru_wi[l], lru_bi[l], lru_lambda[l])
        y_rw = _rwkv_mixer(cols, rwkv_mu[l], rwkv_w0[l], rwkv_w2[l], rwkv_a0[l], rwkv_a2[l], rwkv_g2[l],
                           rwkv_kk[l], rwkv_ka[l], rwkv_rk[l], rwkv_ln_w[l], rwkv_ln_b[l])
        o_cmp, o_sel, o_win = _nsa_branches(cols, nsa_cmp_pos[l], nsa_cmp_w1[l], nsa_cmp_w2[l])
        h = _mix_out(h, y_ssm, y_lru, y_rw, o_cmp, o_sel, o_win, cols, w_out[l])
        h = _peer_block(h.reshape(t, d), ffn_norm[l], peer_wq[l], peer_keys[l], peer_u[l], peer_v[l]).reshape(bsz, s, d)
    return _rmsnorm(h.reshape(t, d), final_norm).reshape(bsz, s, d)
```

```python
import functools
import math

import numpy as np
import jax
import jax.numpy as jnp
from jax import lax
from jax.experimental import pallas as pl
from jax.experimental.pallas import tpu as pltpu

F32 = jnp.float32
BF16 = jnp.bfloat16
I32 = jnp.int32

NORM_EPS = 1e-6
GROUP_WIDTH = 256
HEAD_DIM = 64
N_HEADS = 4
LRU_C = 8.0
NEG_BIG = -1e30


COL = dict(ssd_z=0, ssd_x=256, lru_gate=512, lru_x=768, rw_r=1024, rw_k=1280, rw_v=1536, rw_wa=1792, rw_g=1920,
           nsa_q=2048, nsa_kvc=2304, nsa_kvs=2432, nsa_kvw=2560, nsa_gate=2688, ssd_b=2816, ssd_c=2944, ssd_dt=3072)
NCOL = 3200


def _col_perm():
    src = np.full((NCOL,), -1, np.int64)

    def put(dst, s0, n):
        src[dst:dst + n] = np.arange(s0, s0 + n)

    put(COL["ssd_z"], 0, 256)
    put(COL["ssd_x"], 256, 256)
    put(COL["ssd_b"], 512, 128)
    put(COL["ssd_c"], 640, 128)
    put(COL["ssd_dt"], 768, 4)
    put(COL["lru_gate"], 772, 256)
    put(COL["lru_x"], 1028, 256)
    put(COL["rw_r"], 1284, 1024)
    put(COL["nsa_q"], 2308, 256)
    put(COL["nsa_kvc"], 2564, 128)
    put(COL["nsa_kvs"], 2692, 128)
    put(COL["nsa_kvw"], 2820, 128)
    put(COL["nsa_gate"], 2948, 12)
    return src


def _arrange_cols(a):
    src = _col_perm()
    out = jnp.take(a, jnp.asarray(np.maximum(src, 0)), axis=-1)
    return jnp.where(jnp.asarray(src >= 0), out, jnp.zeros((), a.dtype))


def _cparams(sem, vmem_mb=None):
    kw = dict(dimension_semantics=sem)
    if vmem_mb is not None:
        kw["vmem_limit_bytes"] = vmem_mb * 1024 * 1024
    return pltpu.CompilerParams(**kw)


def _dot(a, b):
    return jnp.dot(a.astype(BF16), b.astype(BF16), preferred_element_type=F32)


def _dot_nt(a, b):
    return lax.dot_general(a.astype(BF16), b.astype(BF16), (((1,), (1,)), ((), ())), preferred_element_type=F32)


def _dot_tn(a, b):
    return lax.dot_general(a.astype(BF16), b.astype(BF16), (((0,), (0,)), ((), ())), preferred_element_type=F32)


def _split3(x):
    h = x.astype(BF16)
    r = x - h.astype(F32)
    m = r.astype(BF16)
    l = (r - m.astype(F32)).astype(BF16)
    return h, m, l


def _split2(x):
    h = x.astype(BF16)
    return h, (x - h.astype(F32)).astype(BF16)


def _dot_exact_rhs(a, b_exact):
    h, m, l = _split3(a)
    d = lambda t: jnp.dot(t, b_exact, preferred_element_type=F32)
    return d(h) + d(m) + d(l)


def _dot_exact_lhs(a_exact, b):
    h, m, l = _split3(b)
    d = lambda t: jnp.dot(a_exact, t, preferred_element_type=F32)
    return d(h) + d(m) + d(l)


def _sigmoid(x):
    return 1.0 / (1.0 + jnp.exp(-x))


def _silu(x):
    return x * _sigmoid(x)


def _gelu_tanh(x):
    return 0.5 * x * (1.0 + jnp.tanh(math.sqrt(2.0 / math.pi) * (x + 0.044715 * (x * x * x))))


def _softplus(x):
    return jnp.maximum(x, 0.0) + jnp.log1p(jnp.exp(-jnp.abs(x)))


def _conv4(cur, tail, w, b):
    c = cur.shape[1]
    row8 = lax.broadcasted_iota(I32, (8, c), 0)
    acc = cur * w[3:4, :] + b
    for k in (1, 2, 3):
        cr = pltpu.roll(cur, k, axis=0)
        tr = pltpu.roll(tail, k, axis=0)
        head = jnp.where(row8 < k, tr, cr[:8])
        sh = jnp.concatenate([head, cr[8:]], axis=0)
        acc = acc + sh * w[3 - k:4 - k, :]
    return acc


def _norm_matmul_kernel(x_ref, g_ref, w_ref, o_ref, xn_ref):
    @pl.when(pl.program_id(1) == 0)
    def _():
        x = x_ref[...]
        y = x * lax.rsqrt(jnp.mean(x * x, axis=-1, keepdims=True) + NORM_EPS) * g_ref[...]
        xn_ref[...] = y.astype(BF16)

    o_ref[...] = jnp.dot(xn_ref[...], w_ref[...], preferred_element_type=F32)


def _norm_matmul(x, g, w, tm, tn):
    m, k = x.shape
    n = w.shape[1]
    return pl.pallas_call(
        _norm_matmul_kernel,
        grid=(m // tm, n // tn),
        in_specs=[pl.BlockSpec((tm, k), lambda i, j: (i, 0)),
                  pl.BlockSpec((1, k), lambda i, j: (0, 0)),
                  pl.BlockSpec((k, tn), lambda i, j: (0, j))],
        out_specs=pl.BlockSpec((tm, tn), lambda i, j: (i, j)),
        out_shape=jax.ShapeDtypeStruct((m, n), F32),
        scratch_shapes=[pltpu.VMEM((tm, k), BF16)],
        compiler_params=_cparams(("arbitrary", "arbitrary")),
    )(x, g, w)


SSD_CHUNK = 128


def _ssd_kernel(z_ref, x_ref, b_ref, c_ref, dt_ref, cwx_ref, cbx_ref, cwb_ref, cbb_ref, cwc_ref, cbc_ref,
                dtb_ref, alog_ref, d_ref, nw_ref, o_ref, tx_ref, tb_ref, tc_ref, st_ref):
    t = SSD_CHUNK

    @pl.when(pl.program_id(1) == 0)
    def _():
        tx_ref[...] = jnp.zeros_like(tx_ref)
        tb_ref[...] = jnp.zeros_like(tb_ref)
        tc_ref[...] = jnp.zeros_like(tc_ref)
        st_ref[...] = jnp.zeros_like(st_ref)

    xr, br, cr = x_ref[...], b_ref[...], c_ref[...]
    xs = _silu(_conv4(xr, tx_ref[...], cwx_ref[...], cbx_ref[...]))
    bm = _silu(_conv4(br, tb_ref[...], cwb_ref[...], cbb_ref[...]))
    cm = _silu(_conv4(cr, tc_ref[...], cwc_ref[...], cbc_ref[...]))
    tx_ref[...] = xr[t - 8:]
    tb_ref[...] = br[t - 8:]
    tc_ref[...] = cr[t - 8:]

    dt = _softplus(dt_ref[...] + dtb_ref[...])
    a = dt * (-jnp.exp(alog_ref[...]))
    ri = lax.broadcasted_iota(I32, (t, t), 0)
    ci = lax.broadcasted_iota(I32, (t, t), 1)
    tril = ri >= ci
    acs = _dot_exact_lhs(jnp.where(tril, 1.0, 0.0).astype(BF16), a)
    acs_t = acs.T
    d_row = d_ref[...]
    ys = []
    for h in range(N_HEADS):
        g = h // 2
        lo, hi = HEAD_DIM * h, HEAD_DIM * (h + 1)
        b_g = bm[:, HEAD_DIM * g:HEAD_DIM * (g + 1)]
        c_g = cm[:, HEAD_DIM * g:HEAD_DIM * (g + 1)]
        x_h = xs[:, lo:hi]
        acs_c = acs[:, h:h + 1]
        lmat = jnp.exp(jnp.where(tril, acs_c - acs_t[h:h + 1, :], NEG_BIG))
        scores = _dot_nt(c_g, b_g) * lmat
        xdt = x_h * dt[:, h:h + 1]
        st = st_ref[h]
        y = _dot(scores, xdt) + _dot(c_g, st) * jnp.exp(acs_c) + x_h * d_row[:, lo:hi]
        a_last = acs[t - 1:t, h:h + 1]
        bw = b_g * jnp.exp(a_last - acs_c)
        st_ref[h] = st * jnp.exp(a_last) + _dot_tn(bw, xdt)
        ys.append(y)
    y = jnp.concatenate(ys, axis=1) * _silu(z_ref[...])
    outs = []
    for g in range(2):
        yg = y[:, 128 * g:128 * (g + 1)]
        outs.append(yg * lax.rsqrt(jnp.mean(yg * yg, axis=-1, keepdims=True) + NORM_EPS))
    o_ref[...] = jnp.concatenate(outs, axis=1) * nw_ref[...]


def _ssd_mixer(cols, conv_w, conv_b, dt_bias, a_log, d_skip, norm_w):
    bsz, s, _ = cols.shape
    t = SSD_CHUNK
    row = lambda v, n: jnp.pad(v.reshape(1, -1), ((0, 0), (0, n - v.size)))
    full = lambda shp: pl.BlockSpec(shp, lambda b, c: (0,) * len(shp))
    colspec = lambda w, name: pl.BlockSpec((None, t, w), lambda b, c: (b, c, COL[name] // w))
    return pl.pallas_call(
        _ssd_kernel,
        grid=(bsz, s // t),
        in_specs=[colspec(256, "ssd_z"), colspec(256, "ssd_x"), colspec(128, "ssd_b"), colspec(128, "ssd_c"),
                  colspec(128, "ssd_dt"),
                  full((4, 256)), full((1, 256)), full((4, 128)), full((1, 128)), full((4, 128)), full((1, 128)),
                  full((1, 128)), full((1, 128)), full((1, 256)), full((1, 256))],
        out_specs=pl.BlockSpec((None, t, 256), lambda b, c: (b, c, 0)),
        out_shape=jax.ShapeDtypeStruct((bsz, s, 256), F32),
        scratch_shapes=[pltpu.VMEM((8, 256), F32), pltpu.VMEM((8, 128), F32), pltpu.VMEM((8, 128), F32),
                        pltpu.VMEM((N_HEADS, HEAD_DIM, HEAD_DIM), F32)],
        compiler_params=_cparams(("arbitrary", "arbitrary")),
    )(cols, cols, cols, cols, cols,
      conv_w[:, :256], conv_b[:256].reshape(1, 256), conv_w[:, 256:384], conv_b[256:384].reshape(1, 128),
      conv_w[:, 384:512], conv_b[384:512].reshape(1, 128),
      row(dt_bias, 128), row(a_log, 128), jnp.repeat(d_skip, HEAD_DIM).reshape(1, 256), norm_w.reshape(1, 256))


LRU_TILE = 256


def _lru_kernel(gate_ref, xr_ref, cw_ref, cb_ref, wa_ref, ba_ref, wi_ref, bi_ref, lam_ref, o_ref, tail_ref, h_ref):
    t = LRU_TILE

    @pl.when(pl.program_id(1) == 0)
    def _():
        tail_ref[...] = jnp.zeros_like(tail_ref)
        h_ref[...] = jnp.zeros_like(h_ref)

    xr = xr_ref[...]
    xc = _conv4(xr, tail_ref[...], cw_ref[...], cb_ref[...])
    tail_ref[...] = xr[t - 8:]
    r = _sigmoid(_dot(xc, wa_ref[...]) + ba_ref[...])
    i = _sigmoid(_dot(xc, wi_ref[...]) + bi_ref[...])
    log_a = LRU_C * r * (-_softplus(-lam_ref[...]))
    a = jnp.exp(log_a)
    u = jnp.sqrt(-jnp.tanh(log_a) * (a * a + 1.0)) * (i * xc)
    rows = lax.broadcasted_iota(I32, (t, GROUP_WIDTH), 0)
    d = 1
    while d < t:
        a_sh = pltpu.roll(a, d, axis=0)
        u_sh = pltpu.roll(u, d, axis=0)
        m = rows >= d
        u = jnp.where(m, a * u_sh + u, u)
        a = jnp.where(m, a * a_sh, a)
        d *= 2
    h = a * h_ref[7:8, :] + u
    h_ref[...] = h[t - 8:]
    o_ref[...] = h * _gelu_tanh(gate_ref[...])


def _lru_mixer(cols, conv_w, conv_b, wa, ba, wi, bi, lam):
    bsz, s, _ = cols.shape
    t = LRU_TILE
    cb = COL["lru_gate"] // 256
    bd = lambda w: jax.scipy.linalg.block_diag(*[w[k] for k in range(w.shape[0])]).astype(BF16)
    full = lambda shp: pl.BlockSpec(shp, lambda b, c: (0,) * len(shp))
    return pl.pallas_call(
        _lru_kernel,
        grid=(bsz, s // t),
        in_specs=[pl.BlockSpec((None, t, 256), lambda b, c: (b, c, cb)),
                  pl.BlockSpec((None, t, 256), lambda b, c: (b, c, cb + 1)),
                  full((4, 256)), full((1, 256)), full((256, 256)), full((1, 256)), full((256, 256)), full((1, 256)),
                  full((1, 256))],
        out_specs=pl.BlockSpec((None, t, 256), lambda b, c: (b, c, 0)),
        out_shape=jax.ShapeDtypeStruct((bsz, s, 256), F32),
        scratch_shapes=[pltpu.VMEM((8, 256), F32), pltpu.VMEM((8, 256), F32)],
        compiler_params=_cparams(("arbitrary", "arbitrary")),
    )(cols, cols, conv_w, conv_b.reshape(1, 256), bd(wa), ba.reshape(1, 256), bd(wi), bi.reshape(1, 256),
      lam.reshape(1, 256))


RW_TILE = 128
RW_GROUP = 4
RW_READOUT_PASSES = 1
RWKV_GN_EPS = 64e-5


def _segsum_bc(x):
    r = x.shape[0]
    m = lax.broadcasted_iota(I32, (r, 128), 1) < HEAD_DIM
    outs = []
    for j in range(2):
        xj = x[:, 128 * j:128 * (j + 1)]
        lo = jnp.sum(jnp.where(m, xj, 0.0), axis=1, keepdims=True)
        hi = jnp.sum(jnp.where(m, 0.0, xj), axis=1, keepdims=True)
        outs.append(jnp.where(m, lo, hi))
    return jnp.concatenate(outs, axis=1)


def _rwkv_kernel(c_ref, mu_ref, w0_ref, w2_ref, a0_ref, a2_ref, g2_ref, kkp_ref, ka_ref, rk_ref, lnw_ref, lnb_ref, e_ref,
                 o_ref, prev_ref, s_ref, r_s, w_s, k_s, v_s, kk_s, b_s, g_s, bon_s, y_s):
    nb = c_ref.shape[0]
    t = RW_TILE

    @pl.when(pl.program_id(0) == 0)
    def _():
        prev_ref[...] = jnp.zeros_like(prev_ref)
        s_ref[...] = jnp.zeros_like(s_ref)
        r_s[...] = jnp.zeros_like(r_s)

    row = lax.broadcasted_iota(I32, (t, 1024), 0)
    for b in range(nb):
        c = c_ref[b]
        pr = jnp.where(row == 0, prev_ref[b, 7:8, :], pltpu.roll(c, 1, axis=0))
        prev_ref[b] = c[t - 8:]
        c = c + (pr - c) * mu_ref[...]
        r, k, v = c[:, 0:256], c[:, 256:512], c[:, 512:768]
        wd, ad, gd = c[:, 768:832], c[:, 832:896], c[:, 896:1024]
        w_log = -_softplus(-(w0_ref[...] + _dot(jnp.tanh(wd), w2_ref[...]))) - 0.5
        a = _sigmoid(a0_ref[...] + _dot(ad, a2_ref[...]))
        kk = k * kkp_ref[...]
        kk = kk / jnp.maximum(jnp.sqrt(_segsum_bc(kk * kk)), 1e-12)
        k2 = k * (1.0 + (a - 1.0) * ka_ref[...])
        r_s[b, 8:, :] = r
        w_s[b] = jnp.exp(-jnp.exp(w_log))
        k_s[b] = k2
        v_s[b] = v
        kk_s[b] = kk
        b_s[b] = kk * a
        g_s[b] = _dot(_sigmoid(gd), g2_ref[...])
        bon_s[b] = _segsum_bc(r * k2 * rk_ref[...]) * v

    gb = RW_GROUP if nb % RW_GROUP == 0 else nb
    nr = gb * HEAD_DIM
    eye4 = jnp.where(lax.broadcasted_iota(I32, (nr, 256), 0) % HEAD_DIM
                     == lax.broadcasted_iota(I32, (nr, 256), 1) % HEAD_DIM, 1.0, 0.0)

    def seg(*xs, passes):
        parts = []
        for x, n in zip(xs, passes):
            parts.extend(_split2(x) if n == 2 else (x.astype(BF16),))
        out = jnp.dot(jnp.concatenate(parts, axis=0), e_ref[...], preferred_element_type=F32)
        res, r0 = [], 0
        for n in passes:
            res.append(out[r0:r0 + nr] + out[r0 + nr:r0 + 2 * nr] if n == 2 else out[r0:r0 + nr])
            r0 += n * nr
        return res

    def rows(ref, b0, i):
        return jnp.concatenate([jnp.broadcast_to(ref[b0 + b, pl.ds(i, 1), :], (HEAD_DIM, 256)) for b in range(gb)],
                               axis=0)

    def put_y(ybc, b0, i):
        yd = ybc * eye4
        for b in range(gb):
            y_s[b0 + b, pl.ds(i, 1), :] = jnp.sum(yd[b * HEAD_DIM:(b + 1) * HEAD_DIM], axis=0, keepdims=True)

    def step(tt, carry):
        for b0 in range(0, nb, gb):
            sl = slice(b0 * HEAD_DIM, (b0 + gb) * HEAD_DIM)
            s = s_ref[sl, :]
            vbc = _segsum_bc(eye4 * rows(v_s, b0, tt))
            sa, ybc = seg(s * rows(kk_s, b0, tt), s * rows(r_s, b0, tt + 7), passes=(2, RW_READOUT_PASSES))
            put_y(ybc, b0, tt + 7)
            s_ref[sl, :] = s * rows(w_s, b0, tt) - sa * rows(b_s, b0, tt) + vbc * rows(k_s, b0, tt)
        return carry

    lax.fori_loop(0, t, step, 0)
    for b0 in range(0, nb, gb):
        ybc, = seg(s_ref[b0 * HEAD_DIM:(b0 + gb) * HEAD_DIM, :] * rows(r_s, b0, t + 7), passes=(RW_READOUT_PASSES,))
        put_y(ybc, b0, t + 7)

    for b in range(nb):
        y = y_s[b, 8:, :]
        d = y - _segsum_bc(y) * (1.0 / HEAD_DIM)
        var = _segsum_bc(d * d) * (1.0 / HEAD_DIM)
        y = d * lax.rsqrt(var + RWKV_GN_EPS) * lnw_ref[...] + lnb_ref[...] + bon_s[b]
        o_ref[b] = y * g_s[b]


def _rwkv_mixer(cols, mu, w0, w2, a0, a2, g2, k_k, k_a, r_k, ln_w, ln_b):
    bsz, s, _ = cols.shape
    t = RW_TILE
    full = lambda shp: pl.BlockSpec(shp, lambda c: (0,) * len(shp))
    r1 = lambda v: v.reshape(1, -1)
    tile = lambda: pltpu.VMEM((bsz, t, 256), F32)
    return pl.pallas_call(
        _rwkv_kernel,
        grid=(s // t,),
        in_specs=[pl.BlockSpec((bsz, t, 1024), lambda c: (0, c, COL["rw_r"] // 1024)),
                  full((1, 1024)), full((1, 256)), full((64, 256)), full((1, 256)), full((64, 256)), full((128, 256)),
                  full((1, 256)), full((1, 256)), full((1, 256)), full((1, 256)), full((1, 256)), full((256, 256))],
        out_specs=pl.BlockSpec((bsz, t, 256), lambda c: (0, c, 0)),
        out_shape=jax.ShapeDtypeStruct((bsz, s, 256), F32),
        scratch_shapes=([pltpu.VMEM((bsz, 8, 1024), F32), pltpu.VMEM((bsz * HEAD_DIM, 256), F32),
                         pltpu.VMEM((bsz, t + 8, 256), F32)] + [tile() for _ in range(7)]
                        + [pltpu.VMEM((bsz, t + 8, 256), F32)]),
        compiler_params=_cparams(("arbitrary",)),
    )(cols, r1(mu), r1(w0), w2, r1(a0), a2, g2, r1(k_k), r1(k_a), r1(r_k), r1(ln_w), r1(ln_b),
      jnp.asarray(np.kron(np.eye(N_HEADS), np.ones((HEAD_DIM, HEAD_DIM))), BF16))


NSA_STRIDE = 16
NSA_SEL_BLOCK = 64
NSA_TOPN = 16
NSA_WINDOW = 512
NSA_FORCE = 1e4
NSA_TQ = 256
NSA_TK = 512
NSA_SLOPES = tuple(2.0 ** (-8.0 * (h + 1) / N_HEADS) for h in range(N_HEADS))
SEL_OFF = -float(2 ** 30)


def _dot3_nt(a, b):
    ah, al = _split2(a)
    bh, bl = _split2(b)
    d = lambda x, y: lax.dot_general(x, y, (((1,), (1,)), ((), ())), preferred_element_type=F32)
    return d(ah, bh) + d(ah, bl) + d(al, bh)


def _nsa_compress_kernel(kv_ref, pos_ref, w1_ref, w2_ref, o_ref):
    nbp = o_ref.shape[0]
    outs = []
    for c in range(2):
        acc_a = jnp.zeros((nbp, 256), F32)
        acc_b = jnp.zeros((nbp, 256), F32)
        for r in range(NSA_STRIDE):
            xr = kv_ref[pl.ds(r, nbp, stride=NSA_STRIDE), :][:, HEAD_DIM * c:HEAD_DIM * (c + 1)]
            acc_a = acc_a + _dot(xr + pos_ref[c, r:r + 1, :], w1_ref[c, 64 * r:64 * (r + 1), :])
            acc_b = acc_b + _dot(xr + pos_ref[c, 16 + r:17 + r, :], w1_ref[c, 1024 + 64 * r:1024 + 64 * (r + 1), :])
        hdn = _gelu_tanh(acc_a + pltpu.roll(acc_b, nbp - 1, axis=0))
        outs.append(_dot(hdn, w2_ref[c]))
    o_ref[...] = jnp.concatenate(outs, axis=1)


def _nsa_cmp_kernel(q_ref, kv_ref, ov_ref, ocmp_ref, qaug_ref, used_ref):
    tq = NSA_TQ
    nbp = kv_ref.shape[0]
    q0 = pl.program_id(1) * tq
    kv = kv_ref[...]
    kc, vc = kv[:, :HEAD_DIM], kv[:, HEAD_DIM:]
    pos = q0 + lax.broadcasted_iota(I32, (tq, nbp), 0)
    blk_end = lax.broadcasted_iota(I32, (tq, nbp), 1) * NSA_STRIDE + (2 * NSA_STRIDE - 1)
    dist = pos - blk_end
    mask = dist >= 0
    distf = dist.astype(F32)
    q = q_ref[...] * (HEAD_DIM ** -0.5)
    lane = lax.broadcasted_iota(I32, (tq, 128), 1)
    psum = jnp.zeros((tq, nbp), F32)
    for h in range(N_HEADS):
        qh = q[:, HEAD_DIM * h:HEAD_DIM * (h + 1)]
        s = jnp.where(mask, _dot3_nt(qh, kc) - NSA_SLOPES[h] * distf, -jnp.inf)
        m = jnp.max(s, axis=1, keepdims=True)
        m = jnp.where(m == -jnp.inf, 0.0, m)
        e = jnp.where(mask, jnp.exp(s - m), 0.0)
        p = e / jnp.maximum(jnp.sum(e, axis=1, keepdims=True), 1e-30)
        ocmp_ref[h] = _dot(p, vc)
        psum = psum + p
        qt = q[:, 128 * (h // 2):128 * (h // 2 + 1)]
        keep = (lane < HEAD_DIM) if h % 2 == 0 else (lane >= HEAD_DIM)
        qaug_ref[h, :, 0:128] = jnp.where(keep, qt, 0.0).astype(BF16)
    imp_t = _dot_exact_rhs(psum, ov_ref[...]).T
    j = lax.broadcasted_iota(I32, (128, tq), 0)
    pos_t = q0 + lax.broadcasted_iota(I32, (128, tq), 1)
    cur = pos_t // NSA_SEL_BLOCK
    forced = (j == 0) | (j == cur) | (j == cur - 1)
    valid = j * NSA_SEL_BLOCK <= pos_t
    x = jnp.where(valid, jnp.where(forced, NSA_FORCE, imp_t), -jnp.inf)
    jf = j.astype(F32)
    picked = jnp.zeros((128, tq), F32)
    for _ in range(NSA_TOPN):
        mx = jnp.max(x, axis=0, keepdims=True)
        first = jnp.min(jnp.where(x == mx, jf, 128.0), axis=0, keepdims=True)
        hit = jf == first
        picked = jnp.where(hit, 1.0, picked)
        x = jnp.where(hit, -jnp.inf, x)
    used = jnp.where(valid & (picked > 0.0), 1.0, 0.0)
    bias = jnp.where(used > 0.0, 0.0, SEL_OFF).T
    for h in range(N_HEADS):
        qaug_ref[h, :, 128:256] = bias.astype(BF16)
    used_ref[...] = jnp.broadcast_to(jnp.max(used.T, axis=0, keepdims=True), (8, 128))


def _nsa_flash_kernel(need_ref, qaug_ref, kv_ref, o_ref, kaug_s, v_s, *, window):
    tq, tk = NSA_TQ, NSA_TK
    s_len = kv_ref.shape[0]
    tk = min(tk, s_len)
    kdim = 128 if window else 256
    qi = pl.program_id(1)
    q0 = qi * tq

    @pl.when(qi == 0)
    def _():
        def build(i, c):
            r0 = pl.multiple_of(i * tk, tk)
            kv = kv_ref[pl.ds(r0, tk), :]
            k = kv[:, :HEAD_DIM]
            parts = [k, k]
            if not window:
                blk = (r0 + lax.broadcasted_iota(I32, (tk, 128), 0)) // NSA_SEL_BLOCK
                parts.append(jnp.where(blk == lax.broadcasted_iota(I32, (tk, 128), 1), 1.0, 0.0))
            kaug_s[pl.ds(r0, tk), :] = jnp.concatenate(parts, axis=1).astype(BF16)
            v_s[pl.ds(r0, tk), :] = kv[:, HEAD_DIM:].astype(BF16)
            return c
        lax.fori_loop(0, s_len // tk, build, 0)

    q = qaug_ref[...][:, :, :kdim].reshape(N_HEADS * tq, kdim)
    row = lax.broadcasted_iota(I32, (N_HEADS * tq, tk), 0)
    t_q = q0 + (row & (tq - 1))
    col = lax.broadcasted_iota(I32, (N_HEADS * tq, tk), 1)

    n_kt = s_len // tk
    need_base = (pl.program_id(0) * pl.num_programs(1) + qi) * n_kt

    colf = lax.broadcasted_iota(I32, (1, tk), 1).astype(F32)

    def body(kt, carry):
        if window:
            return tile(kt, carry, True)
        k_last = kt * tk + (tk - 1)
        return lax.cond(need_ref[need_base + kt] != 0,
                        lambda c: lax.cond(k_last <= q0, lambda d: tile(kt, d, False), lambda d: tile(kt, d, True), c),
                        lambda c: c, carry)

    def tile(kt, carry, masked):
        m, l, acc = carry
        k0 = pl.multiple_of(kt * tk, tk)
        s = lax.dot_general(q, kaug_s[pl.ds(k0, tk), :], (((1,), (1,)), ((), ())), preferred_element_type=F32)
        rel = colf + (k0 - q0).astype(F32)
        s = jnp.concatenate([s[h * tq:(h + 1) * tq] + NSA_SLOPES[h] * rel for h in range(N_HEADS)], axis=0)
        if masked:
            t_k = k0 + col
            ok = t_k <= t_q
            if window:
                ok = ok & (t_q - t_k < NSA_WINDOW)
            m_new = jnp.maximum(m, jnp.max(jnp.where(ok, s, NEG_BIG), axis=1, keepdims=True))
            p = jnp.where(ok, jnp.exp(s - m_new), 0.0)
        else:
            m_new = jnp.maximum(m, jnp.max(s, axis=1, keepdims=True))
            p = jnp.exp(s - m_new)
        alpha = jnp.exp(m - m_new)
        l = alpha * l + jnp.sum(p, axis=1, keepdims=True)
        acc = alpha * acc + jnp.dot(p.astype(BF16), v_s[pl.ds(k0, tk), :], preferred_element_type=F32)
        return m_new, l, acc

    lo = jnp.maximum(q0 - NSA_WINDOW, 0) // tk if window else 0
    hi = (q0 + tq - 1) // tk + 1
    init = (jnp.full((N_HEADS * tq, 1), NEG_BIG, F32), jnp.zeros((N_HEADS * tq, 1), F32),
            jnp.zeros((N_HEADS * tq, HEAD_DIM), F32))
    m, l, acc = lax.fori_loop(lo, hi, body, init)
    o_ref[...] = (acc / l).reshape(N_HEADS, tq, HEAD_DIM)


def _nsa_overlap(s):
    nbp, nsel = s // NSA_STRIDE, s // NSA_SEL_BLOCK
    cs = np.arange(nbp) * NSA_STRIDE
    ss = np.arange(128) * NSA_SEL_BLOCK
    ov = np.minimum(cs[:, None] + 2 * NSA_STRIDE, ss[None, :] + NSA_SEL_BLOCK) - np.maximum(cs[:, None], ss[None, :])
    ov = np.clip(ov, 0, None) / (2 * NSA_STRIDE)
    ov[nbp - 1:, :] = 0.0
    ov[:, nsel:] = 0.0
    return jnp.asarray(ov, BF16)


def _nsa_branches(cols, cmp_pos, cmp_w1, cmp_w2):
    bsz, s, _ = cols.shape
    nbp = s // NSA_STRIDE
    tq = NSA_TQ
    kvcmp = pl.pallas_call(
        _nsa_compress_kernel,
        grid=(bsz,),
        in_specs=[pl.BlockSpec((None, s, 128), lambda b: (b, 0, COL["nsa_kvc"] // 128)),
                  pl.BlockSpec((2, 32, 64), lambda b: (0, 0, 0)),
                  pl.BlockSpec((2, 2048, 256), lambda b: (0, 0, 0)),
                  pl.BlockSpec((2, 256, 64), lambda b: (0, 0, 0))],
        out_specs=pl.BlockSpec((None, nbp, 128), lambda b: (b, 0, 0)),
        out_shape=jax.ShapeDtypeStruct((bsz, nbp, 128), F32),
        compiler_params=_cparams(("arbitrary",), 48),
    )(cols, cmp_pos, cmp_w1.astype(BF16), cmp_w2)
    head_out = lambda dt, w: (pl.BlockSpec((None, N_HEADS, tq, w), lambda b, i: (b, 0, i, 0)),
                              jax.ShapeDtypeStruct((bsz, N_HEADS, s, w), dt))
    (oc_spec, oc_shape), (qa_spec, qa_shape) = head_out(F32, HEAD_DIM), head_out(BF16, 256)
    nqt = s // tq
    o_cmp, qaug, used = pl.pallas_call(
        _nsa_cmp_kernel,
        grid=(bsz, nqt),
        in_specs=[pl.BlockSpec((None, tq, 256), lambda b, i: (b, i, COL["nsa_q"] // 256)),
                  pl.BlockSpec((None, nbp, 128), lambda b, i: (b, 0, 0)),
                  pl.BlockSpec((nbp, 128), lambda b, i: (0, 0))],
        out_specs=[oc_spec, qa_spec, pl.BlockSpec((None, None, 8, 128), lambda b, i: (b, i, 0, 0))],
        out_shape=[oc_shape, qa_shape, jax.ShapeDtypeStruct((bsz, nqt, 8, 128), F32)],
        compiler_params=_cparams(("arbitrary", "arbitrary"), 48),
    )(cols, kvcmp, _nsa_overlap(s))
    tk = min(NSA_TK, s)
    bpt = tk // NSA_SEL_BLOCK
    need = (used[:, :, 0, :(s // tk) * bpt].reshape(bsz, nqt, s // tk, bpt).max(axis=-1) > 0).astype(I32).reshape(-1)

    def flash(name, window):
        hmap = lambda b, i, need_ref: (b, 0, i, 0)
        return pl.pallas_call(
            functools.partial(_nsa_flash_kernel, window=window),
            grid_spec=pltpu.PrefetchScalarGridSpec(
                num_scalar_prefetch=1,
                grid=(bsz, nqt),
                in_specs=[pl.BlockSpec((None, N_HEADS, tq, 256), hmap),
                          pl.BlockSpec((None, s, 128), lambda b, i, need_ref: (b, 0, COL[name] // 128))],
                out_specs=pl.BlockSpec((None, N_HEADS, tq, HEAD_DIM), hmap),
                scratch_shapes=[pltpu.VMEM((s, 128 if window else 256), BF16), pltpu.VMEM((s, HEAD_DIM), BF16)]),
            out_shape=oc_shape,
            compiler_params=_cparams(("arbitrary", "arbitrary"), 48),
        )(need, qaug, cols)

    return o_cmp, flash("nsa_kvs", False), flash("nsa_kvw", True)


def _nsa_mixer(cols, cmp_pos, cmp_w1, cmp_w2):
    o_cmp, o_sel, o_win = _nsa_branches(cols, cmp_pos, cmp_w1, cmp_w2)
    bsz, s, _ = cols.shape
    g = jax.nn.sigmoid(cols[..., COL["nsa_gate"]:COL["nsa_gate"] + 12]).reshape(bsz, s, N_HEADS, 3)
    tr = lambda o: jnp.moveaxis(o, 1, 2)
    out = g[..., 0:1] * tr(o_cmp) + g[..., 1:2] * tr(o_sel) + g[..., 2:3] * tr(o_win)
    return out.reshape(bsz, s, 256)


MIX_TILE = 256


def _mix_out_kernel(h_ref, ssm_ref, lru_ref, rw_ref, oc_ref, os_ref, ow_ref, gate_ref, w_ref, o_ref):
    acc = h_ref[...]
    acc = acc + _dot(ssm_ref[...], w_ref[0:256, :])
    acc = acc + _dot(lru_ref[...], w_ref[256:512, :])
    acc = acc + _dot(rw_ref[...], w_ref[512:768, :])
    g = _sigmoid(gate_ref[...])
    for h in range(N_HEADS):
        y = (g[:, 3 * h:3 * h + 1] * oc_ref[h] + g[:, 3 * h + 1:3 * h + 2] * os_ref[h]
             + g[:, 3 * h + 2:3 * h + 3] * ow_ref[h])
        acc = acc + _dot(y, w_ref[768 + HEAD_DIM * h:768 + HEAD_DIM * (h + 1), :])
    o_ref[...] = acc


def _mix_out(h, y_ssm, y_lru, y_rw, o_cmp, o_sel, o_win, cols, w_out):
    bsz, s, d = h.shape
    t = MIX_TILE
    tok = lambda w: pl.BlockSpec((None, t, w), lambda b, i: (b, i, 0))
    hd = pl.BlockSpec((None, N_HEADS, t, HEAD_DIM), lambda b, i: (b, 0, i, 0))
    return pl.pallas_call(
        _mix_out_kernel,
        grid=(bsz, s // t),
        in_specs=[tok(d), tok(256), tok(256), tok(256), hd, hd, hd,
                  pl.BlockSpec((None, t, 128), lambda b, i: (b, i, COL["nsa_gate"] // 128)),
                  pl.BlockSpec((4 * GROUP_WIDTH, d), lambda b, i: (0, 0))],
        out_specs=tok(d),
        out_shape=jax.ShapeDtypeStruct((bsz, s, d), F32),
        compiler_params=_cparams(("arbitrary", "arbitrary")),
    )(h, y_ssm, y_lru, y_rw, o_cmp, o_sel, o_win, cols, w_out.astype(BF16))


PEER_HEADS = 8
PEER_NKEYS = 128
PEER_TOPK = 16
PEER_TT = 128
PEER_TQ = 256
PEER_ACT_UNROLL = 16
PEER_OUT_UNROLL = 128


PEER_ROW = 4


def _store_packed(x, o_ref):
    rows, d = x.shape
    b = pltpu.bitcast(x, I32)
    r = b + 0x7FFF + ((b >> 16) & 1)
    word = (r[:, d // 2:] & jnp.int32(-65536)) | lax.shift_right_logical(r[:, :d // 2], 16)
    for s in range(PEER_ROW):
        o_ref[pl.ds(s, rows, stride=PEER_ROW), :] = word[:, 128 * s:128 * (s + 1)]


def _norm_matmul2_kernel(x_ref, g_ref, w_ref, o_ref, xpk_ref, xn_ref):
    @pl.when(pl.program_id(1) == 0)
    def _():
        x = x_ref[...]
        y = x * lax.rsqrt(jnp.mean(x * x, axis=-1, keepdims=True) + NORM_EPS) * g_ref[...]
        _store_packed(y, xpk_ref)
        xn_ref[...] = y.astype(BF16)

    o_ref[...] = jnp.dot(xn_ref[...], w_ref[...], preferred_element_type=F32)


def _norm_matmul2(x, g, w, tm, tn):
    m, k = x.shape
    n = w.shape[1]
    return pl.pallas_call(
        _norm_matmul2_kernel,
        grid=(m // tm, n // tn),
        in_specs=[pl.BlockSpec((tm, k), lambda i, j: (i, 0)),
                  pl.BlockSpec((1, k), lambda i, j: (0, 0)),
                  pl.BlockSpec((k, tn), lambda i, j: (0, j))],
        out_specs=[pl.BlockSpec((tm, tn), lambda i, j: (i, j)),
                   pl.BlockSpec((tm * PEER_ROW, 128), lambda i, j: (i, 0))],
        out_shape=[jax.ShapeDtypeStruct((m, n), F32), jax.ShapeDtypeStruct((m * PEER_ROW, 128), I32)],
        scratch_shapes=[pltpu.VMEM((tm, k), BF16)],
        compiler_params=_cparams(("arbitrary", "arbitrary")),
    )(x, g, w)


def _top_rows(x, n, payload=None):
    r = x.shape[0]
    rid = lax.broadcasted_iota(I32, x.shape, 0).astype(F32)
    vals, ids, pays = [], [], []
    for _ in range(n):
        mx = jnp.max(x, axis=0, keepdims=True)
        first = jnp.min(jnp.where(x == mx, rid, float(r)), axis=0, keepdims=True)
        hit = rid == first
        vals.append(mx)
        ids.append(first)
        if payload is not None:
            pays.append(jnp.sum(jnp.where(hit, payload, 0.0), axis=0, keepdims=True))
        x = jnp.where(hit, -jnp.inf, x)
    cat = lambda v: jnp.concatenate(v, axis=0)
    return (cat(vals), cat(ids)) + ((cat(pays),) if payload is not None else ())


def _peer_route_kernel(q_ref, keys_ref, eidx_ref, gate_ref):
    k = PEER_TOPK
    tq = q_ref.shape[0]
    gates, eids = [], []
    for h in range(PEER_HEADS):
        s, i = [], []
        for c in range(2):
            qh = q_ref[:, 128 * (2 * h + c):128 * (2 * h + c + 1)]
            sv, si = _top_rows(_dot3_nt(keys_ref[h, c], qh), k)
            s.append(sv)
            i.append(si)
        cand, cidx = [], []
        for a in range(k):
            nb = k // (a + 1)
            cand.append(s[0][a:a + 1, :] + s[1][:nb])
            cidx.append(i[0][a:a + 1, :] * float(PEER_NKEYS) + i[1][:nb])
        n_c = sum(c.shape[0] for c in cand)
        pad = (-n_c) % 8
        cand.append(jnp.full((pad, tq), -jnp.inf, F32))
        cidx.append(jnp.zeros((pad, tq), F32))
        top, _, eidx = _top_rows(jnp.concatenate(cand, axis=0), k, payload=jnp.concatenate(cidx, axis=0))
        e = jnp.exp(top - top[0:1, :])
        gates.append(e / jnp.sum(e, axis=0, keepdims=True))
        eids.append(eidx)
    gate_ref[...] = jnp.concatenate(gates, axis=0).T
    eidx_ref[...] = (jnp.concatenate(eids, axis=0).T * float(PEER_ROW)).astype(I32)


def _peer_route(q, keys):
    t = q.shape[0]
    tq = min(PEER_TQ, t)
    nk = PEER_HEADS * PEER_TOPK
    return pl.pallas_call(
        _peer_route_kernel,
        grid=(t // tq,),
        in_specs=[pl.BlockSpec((tq, q.shape[1]), lambda i: (i, 0)),
                  pl.BlockSpec(keys.shape, lambda i: (0, 0, 0, 0))],
        out_specs=[pl.BlockSpec((tq, nk), lambda i: (i, 0)), pl.BlockSpec((tq, nk), lambda i: (i, 0))],
        out_shape=[jax.ShapeDtypeStruct((t, nk), I32), jax.ShapeDtypeStruct((t, nk), F32)],
        compiler_params=_cparams(("arbitrary",), 48),
    )(q, keys)


PACK_TILE = 512


def _pack_kernel(t_ref, o_ref):
    _store_packed(t_ref[...], o_ref)


def _pack_table(tab):
    e, d = tab.shape
    te = PACK_TILE
    return pl.pallas_call(
        _pack_kernel,
        grid=(e // te,),
        in_specs=[pl.BlockSpec((te, d), lambda i: (i, 0))],
        out_specs=pl.BlockSpec((te * PEER_ROW, 128), lambda i: (i, 0)),
        out_shape=jax.ShapeDtypeStruct((e * PEER_ROW, 128), I32),
        compiler_params=_cparams(("arbitrary",)),
    )(tab)


def _expert_halves(tbl_ref, r):
    word = tbl_ref[pl.ds(pl.multiple_of(r, PEER_ROW), PEER_ROW), :]
    return pltpu.bitcast(word << 16, F32), pltpu.bitcast(word & jnp.int32(-65536), F32)


_BITREV8 = (0, 4, 2, 6, 1, 5, 3, 7)


def _rowsum8(ps):
    sub = lax.broadcasted_iota(I32, (8, 128), 0)
    q = [ps[_BITREV8[p]] for p in range(8)]
    cur = [jnp.concatenate([q[2 * i], q[2 * i + 1]], axis=0) for i in range(4)]
    for h in (2, 1):
        m = (sub & h) == 0
        cur = [jnp.where(m, x, y) + jnp.where(m, pltpu.roll(x, 8 - h, axis=0), pltpu.roll(y, h, axis=0))
               for x, y in zip(cur[0::2], cur[1::2])]
    return cur[0]


def _peer_act_kernel(idx_ref, x_ref, gate_ref, tbl_ref, o_ref, q_ref, s_ref):
    tt, nk = gate_ref.shape

    @pl.when(pl.program_id(0) == 0)
    def _():
        q_ref[...] = jnp.zeros_like(q_ref)

    def finish(slot, t):
        s_ref[pl.ds(t, 1), :] = jnp.sum(q_ref[slot].T, axis=0, keepdims=True)

    def tok(t, c):
        finish((t + 1) % 2, jnp.maximum(t - 1, 0))
        xb = pltpu.bitcast(x_ref[pl.ds(pl.multiple_of(t * PEER_ROW, PEER_ROW), PEER_ROW), :], BF16)
        slot = t % 2
        for g in range(nk // 8):
            ps = []
            for j in range(8):
                r = pl.multiple_of(idx_ref[t * nk + 8 * g + j], PEER_ROW)
                prod = pltpu.bitcast(pltpu.bitcast(tbl_ref[pl.ds(r, PEER_ROW), :], BF16) * xb, I32)
                ps.append(pltpu.bitcast(prod << 16, F32) + pltpu.bitcast(prod & jnp.int32(-65536), F32))
            q_ref[slot, 8 * g:8 * (g + 1), :] = _rowsum8(ps)
        return c

    lax.fori_loop(0, tt, tok, 0)
    finish((tt - 1) % 2, tt - 1)
    o_ref[...] = gate_ref[...] * _gelu_tanh(s_ref[...])


def _peer_out_kernel(idx_ref, w_ref, tbl_ref, h_ref, o_ref):
    tt = h_ref.shape[0]
    nk = idx_ref.shape[0] // tt

    def tok(t, c):
        def grp(g, accs):
            accs = list(accs)
            k0 = t * nk + g * PEER_OUT_UNROLL
            for j in range(PEER_OUT_UNROLL):
                w = w_ref[k0 + j]
                vlo, vhi = _expert_halves(tbl_ref, idx_ref[k0 + j])
                a = 2 * (j % 2)
                accs[a] = accs[a] + vlo * w
                accs[a + 1] = accs[a + 1] + vhi * w
            return tuple(accs)

        z = jnp.zeros((PEER_ROW, 128), F32)
        l0, h0, l1, h1 = lax.fori_loop(0, nk // PEER_OUT_UNROLL, grp, (z, z, z, z))
        o_ref[t] = h_ref[t] + jnp.concatenate([l0 + l1, h0 + h1], axis=0)
        return c

    lax.fori_loop(0, tt, tok, 0)


def _peer_experts(eidx_t, gate_t, xn3, h3, tbl_u, tbl_v):
    t, nk = eidx_t.shape
    tt = min(PEER_TT, t)
    smem = lambda: pl.BlockSpec((tt * nk,), lambda i: (i,), memory_space=pltpu.SMEM)
    vm = lambda: pl.BlockSpec((tt, nk), lambda i: (i, 0))
    tok = lambda: pl.BlockSpec((tt, 8, 128), lambda i: (i, 0, 0))
    tbl = lambda: pl.BlockSpec(tbl_u.shape, lambda i: (0, 0), pipeline_mode=pl.Buffered(1))
    eidx_flat = eidx_t.reshape(t * nk)
    w_t = pl.pallas_call(
        _peer_act_kernel,
        grid=(t // tt,),
        in_specs=[smem(), pl.BlockSpec((tt * PEER_ROW, 128), lambda i: (i, 0)), vm(), tbl()],
        out_specs=vm(),
        out_shape=jax.ShapeDtypeStruct((t, nk), F32),
        scratch_shapes=[pltpu.VMEM((2, nk, 128), F32), pltpu.VMEM((tt, nk), F32)],
        compiler_params=_cparams(("arbitrary",), 48),
    )(eidx_flat, xn3, gate_t, tbl_u)
    return pl.pallas_call(
        _peer_out_kernel,
        grid=(t // tt,),
        in_specs=[smem(), smem(), tbl(), tok()],
        out_specs=tok(),
        out_shape=jax.ShapeDtypeStruct(h3.shape, F32),
        compiler_params=_cparams(("arbitrary",), 48),
    )(eidx_flat, w_t.reshape(t * nk), tbl_v, h3)


def _peer_block(h, g, wq, keys, u, v):
    t, d = h.shape
    q, xpk = _norm_matmul2(h, g.reshape(1, d), wq.astype(BF16), min(512, t), 1024)
    eidx_t, gate_t = _peer_route(q, keys)
    out = _peer_experts(eidx_t, gate_t, xpk, h.reshape(t, 8, 128), _pack_table(u), _pack_table(v))
    return out.reshape(t, d)


def _rmsnorm_kernel(x_ref, g_ref, o_ref):
    x = x_ref[...]
    o_ref[...] = x * lax.rsqrt(jnp.mean(x * x, axis=-1, keepdims=True) + NORM_EPS) * g_ref[...]


def _rmsnorm(x, g, tm=512):
    m, d = x.shape
    return pl.pallas_call(
        _rmsnorm_kernel,
        grid=(m // tm,),
        in_specs=[pl.BlockSpec((tm, d), lambda i: (i, 0)), pl.BlockSpec((1, d), lambda i: (0, 0))],
        out_specs=pl.BlockSpec((tm, d), lambda i: (i, 0)),
        out_shape=jax.ShapeDtypeStruct((m, d), F32),
        compiler_params=_cparams(("arbitrary",)),
    )(x, g.reshape(1, d))


def kernel(x, mix_norm, w_in, w_out, ssm_conv_w, ssm_conv_b, ssm_dt_bias, ssm_a_log, ssm_d, ssm_norm, lru_conv_w, lru_conv_b, lru_wa, lru_ba, lru_wi, lru_bi, lru_lambda, rwkv_mu, rwkv_w0, rwkv_w2, rwkv_a0, rwkv_a2, rwkv_g2, rwkv_kk, rwkv_ka, rwkv_rk, rwkv_ln_w, rwkv_ln_b, nsa_cmp_pos, nsa_cmp_w1, nsa_cmp_w2, ffn_norm, peer_wq, peer_keys, peer_u, peer_v, final_norm):
    bsz, s, d = x.shape
    t = bsz * s
    h = x
    for l in range(w_in.shape[0]):
        w_l = _arrange_cols(w_in[l]).astype(BF16)
        cols = _norm_matmul(h.reshape(t, d), mix_norm[l].reshape(1, d), w_l, 512, 640).reshape(bsz, s, NCOL)
        y_ssm = _ssd_mixer(cols, ssm_conv_w[l], ssm_conv_b[l], ssm_dt_bias[l], ssm_a_log[l], ssm_d[l], ssm_norm[l])
        y_lru = _lru_mixer(cols, lru_conv_w[l], lru_conv_b[l], lru_wa[l], lru_ba[l], lru_wi[l], lru_bi[l], lru_lambda[l])
        y_rw = _rwkv_mixer(cols, rwkv_mu[l], rwkv_w0[l], rwkv_w2[l], rwkv_a0[l], rwkv_a2[l], rwkv_g2[l],
                           rwkv_kk[l], rwkv_ka[l], rwkv_rk[l], rwkv_ln_w[l], rwkv_ln_b[l])
        o_cmp, o_sel, o_win = _nsa_branches(cols, nsa_cmp_pos[l], nsa_cmp_w1[l], nsa_cmp_w2[l])
        h = _mix_out(h, y_ssm, y_lru, y_rw, o_cmp, o_sel, o_win, cols, w_out[l])
        h = _peer_block(h.reshape(t, d), ffn_norm[l], peer_wq[l], peer_keys[l], peer_u[l], peer_v[l]).reshape(bsz, s, d)
    return _rmsnorm(h.reshape(t, d), final_norm).reshape(bsz, s, d)
```

```python
import functools
import math

import numpy as np
import jax
import jax.numpy as jnp
from jax import lax
from jax.experimental import pallas as pl
from jax.experimental.pallas import tpu as pltpu

F32 = jnp.float32
BF16 = jnp.bfloat16
I32 = jnp.int32

NORM_EPS = 1e-6
GROUP_WIDTH = 256
HEAD_DIM = 64
N_HEADS = 4
LRU_C = 8.0
NEG_BIG = -1e30


COL = dict(ssd_z=0, ssd_x=256, lru_gate=512, lru_x=768, rw_r=1024, rw_k=1280, rw_v=1536, rw_wa=1792, rw_g=1920,
           nsa_q=2048, nsa_kvc=2304, nsa_kvs=2432, nsa_kvw=2560, nsa_gate=2688, ssd_b=2816, ssd_c=2944, ssd_dt=3072)
NCOL = 3200


def _col_perm():
    src = np.full((NCOL,), -1, np.int64)

    def put(dst, s0, n):
        src[dst:dst + n] = np.arange(s0, s0 + n)

    put(COL["ssd_z"], 0, 256)
    put(COL["ssd_x"], 256, 256)
    put(COL["ssd_b"], 512, 128)
    put(COL["ssd_c"], 640, 128)
    put(COL["ssd_dt"], 768, 4)
    put(COL["lru_gate"], 772, 256)
    put(COL["lru_x"], 1028, 256)
    put(COL["rw_r"], 1284, 1024)
    put(COL["nsa_q"], 2308, 256)
    put(COL["nsa_kvc"], 2564, 128)
    put(COL["nsa_kvs"], 2692, 128)
    put(COL["nsa_kvw"], 2820, 128)
    put(COL["nsa_gate"], 2948, 12)
    return src


def _arrange_cols(a):
    src = _col_perm()
    out = jnp.take(a, jnp.asarray(np.maximum(src, 0)), axis=-1)
    return jnp.where(jnp.asarray(src >= 0), out, jnp.zeros((), a.dtype))


def _cparams(sem, vmem_mb=None):
    kw = dict(dimension_semantics=sem)
    if vmem_mb is not None:
        kw["vmem_limit_bytes"] = vmem_mb * 1024 * 1024
    return pltpu.CompilerParams(**kw)


def _dot(a, b):
    return jnp.dot(a.astype(BF16), b.astype(BF16), preferred_element_type=F32)


def _dot_nt(a, b):
    return lax.dot_general(a.astype(BF16), b.astype(BF16), (((1,), (1,)), ((), ())), preferred_element_type=F32)


def _dot_tn(a, b):
    return lax.dot_general(a.astype(BF16), b.astype(BF16), (((0,), (0,)), ((), ())), preferred_element_type=F32)


def _split3(x):
    h = x.astype(BF16)
    r = x - h.astype(F32)
    m = r.astype(BF16)
    l = (r - m.astype(F32)).astype(BF16)
    return h, m, l


def _split2(x):
    h = x.astype(BF16)
    return h, (x - h.astype(F32)).astype(BF16)


def _dot_exact_rhs(a, b_exact):
    h, m, l = _split3(a)
    d = lambda t: jnp.dot(t, b_exact, preferred_element_type=F32)
    return d(h) + d(m) + d(l)


def _dot_exact_lhs(a_exact, b):
    h, m, l = _split3(b)
    d = lambda t: jnp.dot(a_exact, t, preferred_element_type=F32)
    return d(h) + d(m) + d(l)


def _sigmoid(x):
    return 1.0 / (1.0 + jnp.exp(-x))


def _silu(x):
    return x * _sigmoid(x)


def _gelu_tanh(x):
    return 0.5 * x * (1.0 + jnp.tanh(math.sqrt(2.0 / math.pi) * (x + 0.044715 * (x * x * x))))


def _softplus(x):
    return jnp.maximum(x, 0.0) + jnp.log1p(jnp.exp(-jnp.abs(x)))


def _conv4(cur, tail, w, b):
    c = cur.shape[1]
    row8 = lax.broadcasted_iota(I32, (8, c), 0)
    acc = cur * w[3:4, :] + b
    for k in (1, 2, 3):
        cr = pltpu.roll(cur, k, axis=0)
        tr = pltpu.roll(tail, k, axis=0)
        head = jnp.where(row8 < k, tr, cr[:8])
        sh = jnp.concatenate([head, cr[8:]], axis=0)
        acc = acc + sh * w[3 - k:4 - k, :]
    return acc


def _norm_matmul_kernel(x_ref, g_ref, w_ref, o_ref, xn_ref):
    @pl.when(pl.program_id(1) == 0)
    def _():
        x = x_ref[...]
        y = x * lax.rsqrt(jnp.mean(x * x, axis=-1, keepdims=True) + NORM_EPS) * g_ref[...]
        xn_ref[...] = y.astype(BF16)

    o_ref[...] = jnp.dot(xn_ref[...], w_ref[...], preferred_element_type=F32)


def _norm_matmul(x, g, w, tm, tn):
    m, k = x.shape
    n = w.shape[1]
    return pl.pallas_call(
        _norm_matmul_kernel,
        grid=(m // tm, n // tn),
        in_specs=[pl.BlockSpec((tm, k), lambda i, j: (i, 0)),
                  pl.BlockSpec((1, k), lambda i, j: (0, 0)),
                  pl.BlockSpec((k, tn), lambda i, j: (0, j))],
        out_specs=pl.BlockSpec((tm, tn), lambda i, j: (i, j)),
        out_shape=jax.ShapeDtypeStruct((m, n), F32),
        scratch_shapes=[pltpu.VMEM((tm, k), BF16)],
        compiler_params=_cparams(("arbitrary", "arbitrary")),
    )(x, g, w)


SSD_CHUNK = 128


def _ssd_kernel(z_ref, x_ref, b_ref, c_ref, dt_ref, cwx_ref, cbx_ref, cwb_ref, cbb_ref, cwc_ref, cbc_ref,
                dtb_ref, alog_ref, d_ref, nw_ref, o_ref, tx_ref, tb_ref, tc_ref, st_ref):
    t = SSD_CHUNK

    @pl.when(pl.program_id(1) == 0)
    def _():
        tx_ref[...] = jnp.zeros_like(tx_ref)
        tb_ref[...] = jnp.zeros_like(tb_ref)
        tc_ref[...] = jnp.zeros_like(tc_ref)
        st_ref[...] = jnp.zeros_like(st_ref)

    xr, br, cr = x_ref[...], b_ref[...], c_ref[...]
    xs = _silu(_conv4(xr, tx_ref[...], cwx_ref[...], cbx_ref[...]))
    bm = _silu(_conv4(br, tb_ref[...], cwb_ref[...], cbb_ref[...]))
    cm = _silu(_conv4(cr, tc_ref[...], cwc_ref[...], cbc_ref[...]))
    tx_ref[...] = xr[t - 8:]
    tb_ref[...] = br[t - 8:]
    tc_ref[...] = cr[t - 8:]

    dt = _softplus(dt_ref[...] + dtb_ref[...])
    a = dt * (-jnp.exp(alog_ref[...]))
    ri = lax.broadcasted_iota(I32, (t, t), 0)
    ci = lax.broadcasted_iota(I32, (t, t), 1)
    tril = ri >= ci
    acs = _dot_exact_lhs(jnp.where(tril, 1.0, 0.0).astype(BF16), a)
    acs_t = acs.T
    d_row = d_ref[...]
    ys = []
    for h in range(N_HEADS):
        g = h // 2
        lo, hi = HEAD_DIM * h, HEAD_DIM * (h + 1)
        b_g = bm[:, HEAD_DIM * g:HEAD_DIM * (g + 1)]
        c_g = cm[:, HEAD_DIM * g:HEAD_DIM * (g + 1)]
        x_h = xs[:, lo:hi]
        acs_c = acs[:, h:h + 1]
        lmat = jnp.exp(jnp.where(tril, acs_c - acs_t[h:h + 1, :], NEG_BIG))
        scores = _dot_nt(c_g, b_g) * lmat
        xdt = x_h * dt[:, h:h + 1]
        st = st_ref[h]
        y = _dot(scores, xdt) + _dot(c_g, st) * jnp.exp(acs_c) + x_h * d_row[:, lo:hi]
        a_last = acs[t - 1:t, h:h + 1]
        bw = b_g * jnp.exp(a_last - acs_c)
        st_ref[h] = st * jnp.exp(a_last) + _dot_tn(bw, xdt)
        ys.append(y)
    y = jnp.concatenate(ys, axis=1) * _silu(z_ref[...])
    outs = []
    for g in range(2):
        yg = y[:, 128 * g:128 * (g + 1)]
        outs.append(yg * lax.rsqrt(jnp.mean(yg * yg, axis=-1, keepdims=True) + NORM_EPS))
    o_ref[...] = jnp.concatenate(outs, axis=1) * nw_ref[...]


def _ssd_mixer(cols, conv_w, conv_b, dt_bias, a_log, d_skip, norm_w):
    bsz, s, _ = cols.shape
    t = SSD_CHUNK
    row = lambda v, n: jnp.pad(v.reshape(1, -1), ((0, 0), (0, n - v.size)))
    full = lambda shp: pl.BlockSpec(shp, lambda b, c: (0,) * len(shp))
    colspec = lambda w, name: pl.BlockSpec((None, t, w), lambda b, c: (b, c, COL[name] // w))
    return pl.pallas_call(
        _ssd_kernel,
        grid=(bsz, s // t),
        in_specs=[colspec(256, "ssd_z"), colspec(256, "ssd_x"), colspec(128, "ssd_b"), colspec(128, "ssd_c"),
                  colspec(128, "ssd_dt"),
                  full((4, 256)), full((1, 256)), full((4, 128)), full((1, 128)), full((4, 128)), full((1, 128)),
                  full((1, 128)), full((1, 128)), full((1, 256)), full((1, 256))],
        out_specs=pl.BlockSpec((None, t, 256), lambda b, c: (b, c, 0)),
        out_shape=jax.ShapeDtypeStruct((bsz, s, 256), F32),
        scratch_shapes=[pltpu.VMEM((8, 256), F32), pltpu.VMEM((8, 128), F32), pltpu.VMEM((8, 128), F32),
                        pltpu.VMEM((N_HEADS, HEAD_DIM, HEAD_DIM), F32)],
        compiler_params=_cparams(("arbitrary", "arbitrary")),
    )(cols, cols, cols, cols, cols,
      conv_w[:, :256], conv_b[:256].reshape(1, 256), conv_w[:, 256:384], conv_b[256:384].reshape(1, 128),
      conv_w[:, 384:512], conv_b[384:512].reshape(1, 128),
      row(dt_bias, 128), row(a_log, 128), jnp.repeat(d_skip, HEAD_DIM).reshape(1, 256), norm_w.reshape(1, 256))


LRU_TILE = 256


def _lru_kernel(gate_ref, xr_ref, cw_ref, cb_ref, wa_ref, ba_ref, wi_ref, bi_ref, lam_ref, o_ref, tail_ref, h_ref):
    t = LRU_TILE

    @pl.when(pl.program_id(1) == 0)
    def _():
        tail_ref[...] = jnp.zeros_like(tail_ref)
        h_ref[...] = jnp.zeros_like(h_ref)

    xr = xr_ref[...]
    xc = _conv4(xr, tail_ref[...], cw_ref[...], cb_ref[...])
    tail_ref[...] = xr[t - 8:]
    r = _sigmoid(_dot(xc, wa_ref[...]) + ba_ref[...])
    i = _sigmoid(_dot(xc, wi_ref[...]) + bi_ref[...])
    log_a = LRU_C * r * (-_softplus(-lam_ref[...]))
    a = jnp.exp(log_a)
    u = jnp.sqrt(-jnp.tanh(log_a) * (a * a + 1.0)) * (i * xc)
    rows = lax.broadcasted_iota(I32, (t, GROUP_WIDTH), 0)
    d = 1
    while d < t:
        a_sh = pltpu.roll(a, d, axis=0)
        u_sh = pltpu.roll(u, d, axis=0)
        m = rows >= d
        u = jnp.where(m, a * u_sh + u, u)
        a = jnp.where(m, a * a_sh, a)
        d *= 2
    h = a * h_ref[7:8, :] + u
    h_ref[...] = h[t - 8:]
    o_ref[...] = h * _gelu_tanh(gate_ref[...])


def _lru_mixer(cols, conv_w, conv_b, wa, ba, wi, bi, lam):
    bsz, s, _ = cols.shape
    t = LRU_TILE
    cb = COL["lru_gate"] // 256
    bd = lambda w: jax.scipy.linalg.block_diag(*[w[k] for k in range(w.shape[0])]).astype(BF16)
    full = lambda shp: pl.BlockSpec(shp, lambda b, c: (0,) * len(shp))
    return pl.pallas_call(
        _lru_kernel,
        grid=(bsz, s // t),
        in_specs=[pl.BlockSpec((None, t, 256), lambda b, c: (b, c, cb)),
                  pl.BlockSpec((None, t, 256), lambda b, c: (b, c, cb + 1)),
                  full((4, 256)), full((1, 256)), full((256, 256)), full((1, 256)), full((256, 256)), full((1, 256)),
                  full((1, 256))],
        out_specs=pl.BlockSpec((None, t, 256), lambda b, c: (b, c, 0)),
        out_shape=jax.ShapeDtypeStruct((bsz, s, 256), F32),
        scratch_shapes=[pltpu.VMEM((8, 256), F32), pltpu.VMEM((8, 256), F32)],
        compiler_params=_cparams(("arbitrary", "arbitrary")),
    )(cols, cols, conv_w, conv_b.reshape(1, 256), bd(wa), ba.reshape(1, 256), bd(wi), bi.reshape(1, 256),
      lam.reshape(1, 256))


RW_TILE = 128
RW_GROUP = 4
RW_READOUT_PASSES = 1
RWKV_GN_EPS = 64e-5


def _segsum_bc(x):
    r = x.shape[0]
    m = lax.broadcasted_iota(I32, (r, 128), 1) < HEAD_DIM
    outs = []
    for j in range(2):
        xj = x[:, 128 * j:128 * (j + 1)]
        lo = jnp.sum(jnp.where(m, xj, 0.0), axis=1, keepdims=True)
        hi = jnp.sum(jnp.where(m, 0.0, xj), axis=1, keepdims=True)
        outs.append(jnp.where(m, lo, hi))
    return jnp.concatenate(outs, axis=1)


def _rwkv_kernel(c_ref, mu_ref, w0_ref, w2_ref, a0_ref, a2_ref, g2_ref, kkp_ref, ka_ref, rk_ref, lnw_ref, lnb_ref, e_ref,
                 o_ref, prev_ref, s_ref, r_s, w_s, k_s, v_s, kk_s, b_s, g_s, bon_s, y_s):
    nb = c_ref.shape[0]
    t = RW_TILE

    @pl.when(pl.program_id(0) == 0)
    def _():
        prev_ref[...] = jnp.zeros_like(prev_ref)
        s_ref[...] = jnp.zeros_like(s_ref)
        r_s[...] = jnp.zeros_like(r_s)

    row = lax.broadcasted_iota(I32, (t, 1024), 0)
    for b in range(nb):
        c = c_ref[b]
        pr = jnp.where(row == 0, prev_ref[b, 7:8, :], pltpu.roll(c, 1, axis=0))
        prev_ref[b] = c[t - 8:]
        c = c + (pr - c) * mu_ref[...]
        r, k, v = c[:, 0:256], c[:, 256:512], c[:, 512:768]
        wd, ad, gd = c[:, 768:832], c[:, 832:896], c[:, 896:1024]
        w_log = -_softplus(-(w0_ref[...] + _dot(jnp.tanh(wd), w2_ref[...]))) - 0.5
        a = _sigmoid(a0_ref[...] + _dot(ad, a2_ref[...]))
        kk = k * kkp_ref[...]
        kk = kk / jnp.maximum(jnp.sqrt(_segsum_bc(kk * kk)), 1e-12)
        k2 = k * (1.0 + (a - 1.0) * ka_ref[...])
        r_s[b, 8:, :] = r
        w_s[b] = jnp.exp(-jnp.exp(w_log))
        k_s[b] = k2
        v_s[b] = v
        kk_s[b] = kk
        b_s[b] = kk * a
        g_s[b] = _dot(_sigmoid(gd), g2_ref[...])
        bon_s[b] = _segsum_bc(r * k2 * rk_ref[...]) * v

    gb = RW_GROUP if nb % RW_GROUP == 0 else nb
    nr = gb * HEAD_DIM
    eye4 = jnp.where(lax.broadcasted_iota(I32, (nr, 256), 0) % HEAD_DIM
                     == lax.broadcasted_iota(I32, (nr, 256), 1) % HEAD_DIM, 1.0, 0.0)

    def seg(*xs, passes):
        parts = []
        for x, n in zip(xs, passes):
            parts.extend(_split2(x) if n == 2 else (x.astype(BF16),))
        out = jnp.dot(jnp.concatenate(parts, axis=0), e_ref[...], preferred_element_type=F32)
        res, r0 = [], 0
        for n in passes:
            res.append(out[r0:r0 + nr] + out[r0 + nr:r0 + 2 * nr] if n == 2 else out[r0:r0 + nr])
            r0 += n * nr
        return res

    def rows(ref, b0, i):
        return jnp.concatenate([jnp.broadcast_to(ref[b0 + b, pl.ds(i, 1), :], (HEAD_DIM, 256)) for b in range(gb)],
                               axis=0)

    def put_y(ybc, b0, i):
        yd = ybc * eye4
        for b in range(gb):
            y_s[b0 + b, pl.ds(i, 1), :] = jnp.sum(yd[b * HEAD_DIM:(b + 1) * HEAD_DIM], axis=0, keepdims=True)

    def step(tt, carry):
        for b0 in range(0, nb, gb):
            sl = slice(b0 * HEAD_DIM, (b0 + gb) * HEAD_DIM)
            s = s_ref[sl, :]
            vbc = _segsum_bc(eye4 * rows(v_s, b0, tt))
            sa, ybc = seg(s * rows(kk_s, b0, tt), s * rows(r_s, b0, tt + 7), passes=(2, RW_READOUT_PASSES))
            put_y(ybc, b0, tt + 7)
            s_ref[sl, :] = s * rows(w_s, b0, tt) - sa * rows(b_s, b0, tt) + vbc * rows(k_s, b0, tt)
        return carry

    lax.fori_loop(0, t, step, 0)
    for b0 in range(0, nb, gb):
        ybc, = seg(s_ref[b0 * HEAD_DIM:(b0 + gb) * HEAD_DIM, :] * rows(r_s, b0, t + 7), passes=(RW_READOUT_PASSES,))
        put_y(ybc, b0, t + 7)

    for b in range(nb):
        y = y_s[b, 8:, :]
        d = y - _segsum_bc(y) * (1.0 / HEAD_DIM)
        var = _segsum_bc(d * d) * (1.0 / HEAD_DIM)
        y = d * lax.rsqrt(var + RWKV_GN_EPS) * lnw_ref[...] + lnb_ref[...] + bon_s[b]
        o_ref[b] = y * g_s[b]


def _rwkv_mixer(cols, mu, w0, w2, a0, a2, g2, k_k, k_a, r_k, ln_w, ln_b):
    bsz, s, _ = cols.shape
    t = RW_TILE
    full = lambda shp: pl.BlockSpec(shp, lambda c: (0,) * len(shp))
    r1 = lambda v: v.reshape(1, -1)
    tile = lambda: pltpu.VMEM((bsz, t, 256), F32)
    return pl.pallas_call(
        _rwkv_kernel,
        grid=(s // t,),
        in_specs=[pl.BlockSpec((bsz, t, 1024), lambda c: (0, c, COL["rw_r"] // 1024)),
                  full((1, 1024)), full((1, 256)), full((64, 256)), full((1, 256)), full((64, 256)), full((128, 256)),
                  full((1, 256)), full((1, 256)), full((1, 256)), full((1, 256)), full((1, 256)), full((256, 256))],
        out_specs=pl.BlockSpec((bsz, t, 256), lambda c: (0, c, 0)),
        out_shape=jax.ShapeDtypeStruct((bsz, s, 256), F32),
        scratch_shapes=([pltpu.VMEM((bsz, 8, 1024), F32), pltpu.VMEM((bsz * HEAD_DIM, 256), F32),
                         pltpu.VMEM((bsz, t + 8, 256), F32)] + [tile() for _ in range(7)]
                        + [pltpu.VMEM((bsz, t + 8, 256), F32)]),
        compiler_params=_cparams(("arbitrary",)),
    )(cols, r1(mu), r1(w0), w2, r1(a0), a2, g2, r1(k_k), r1(k_a), r1(r_k), r1(ln_w), r1(ln_b),
      jnp.asarray(np.kron(np.eye(N_HEADS), np.ones((HEAD_DIM, HEAD_DIM))), BF16))


NSA_STRIDE = 16
NSA_SEL_BLOCK = 64
NSA_TOPN = 16
NSA_WINDOW = 512
NSA_FORCE = 1e4
NSA_TQ = 256
NSA_TK = 512
NSA_SLOPES = tuple(2.0 ** (-8.0 * (h + 1) / N_HEADS) for h in range(N_HEADS))
SEL_OFF = -float(2 ** 30)


def _dot3_nt(a, b):
    ah, al = _split2(a)
    bh, bl = _split2(b)
    d = lambda x, y: lax.dot_general(x, y, (((1,), (1,)), ((), ())), preferred_element_type=F32)
    return d(ah, bh) + d(ah, bl) + d(al, bh)


def _nsa_compress_kernel(kv_ref, pos_ref, w1_ref, w2_ref, o_ref):
    nbp = o_ref.shape[0]
    outs = []
    for c in range(2):
        acc_a = jnp.zeros((nbp, 256), F32)
        acc_b = jnp.zeros((nbp, 256), F32)
        for r in range(NSA_STRIDE):
            xr = kv_ref[pl.ds(r, nbp, stride=NSA_STRIDE), :][:, HEAD_DIM * c:HEAD_DIM * (c + 1)]
            acc_a = acc_a + _dot(xr + pos_ref[c, r:r + 1, :], w1_ref[c, 64 * r:64 * (r + 1), :])
            acc_b = acc_b + _dot(xr + pos_ref[c, 16 + r:17 + r, :], w1_ref[c, 1024 + 64 * r:1024 + 64 * (r + 1), :])
        hdn = _gelu_tanh(acc_a + pltpu.roll(acc_b, nbp - 1, axis=0))
        outs.append(_dot(hdn, w2_ref[c]))
    o_ref[...] = jnp.concatenate(outs, axis=1)


def _nsa_cmp_kernel(q_ref, kv_ref, ov_ref, ocmp_ref, qaug_ref, used_ref):
    tq = NSA_TQ
    nbp = kv_ref.shape[0]
    q0 = pl.program_id(1) * tq
    kv = kv_ref[...]
    kc, vc = kv[:, :HEAD_DIM], kv[:, HEAD_DIM:]
    pos = q0 + lax.broadcasted_iota(I32, (tq, nbp), 0)
    blk_end = lax.broadcasted_iota(I32, (tq, nbp), 1) * NSA_STRIDE + (2 * NSA_STRIDE - 1)
    dist = pos - blk_end
    mask = dist >= 0
    distf = dist.astype(F32)
    q = q_ref[...] * (HEAD_DIM ** -0.5)
    lane = lax.broadcasted_iota(I32, (tq, 128), 1)
    psum = jnp.zeros((tq, nbp), F32)
    for h in range(N_HEADS):
        qh = q[:, HEAD_DIM * h:HEAD_DIM * (h + 1)]
        s = jnp.where(mask, _dot3_nt(qh, kc) - NSA_SLOPES[h] * distf, -jnp.inf)
        m = jnp.max(s, axis=1, keepdims=True)
        m = jnp.where(m == -jnp.inf, 0.0, m)
        e = jnp.where(mask, jnp.exp(s - m), 0.0)
        p = e / jnp.maximum(jnp.sum(e, axis=1, keepdims=True), 1e-30)
        ocmp_ref[h] = _dot(p, vc)
        psum = psum + p
        qt = q[:, 128 * (h // 2):128 * (h // 2 + 1)]
        keep = (lane < HEAD_DIM) if h % 2 == 0 else (lane >= HEAD_DIM)
        qaug_ref[h, :, 0:128] = jnp.where(keep, qt, 0.0).astype(BF16)
    imp_t = _dot_exact_rhs(psum, ov_ref[...]).T
    j = lax.broadcasted_iota(I32, (128, tq), 0)
    pos_t = q0 + lax.broadcasted_iota(I32, (128, tq), 1)
    cur = pos_t // NSA_SEL_BLOCK
    forced = (j == 0) | (j == cur) | (j == cur - 1)
    valid = j * NSA_SEL_BLOCK <= pos_t
    x = jnp.where(valid, jnp.where(forced, NSA_FORCE, imp_t), -jnp.inf)
    jf = j.astype(F32)
    picked = jnp.zeros((128, tq), F32)
    for _ in range(NSA_TOPN):
        mx = jnp.max(x, axis=0, keepdims=True)
        first = jnp.min(jnp.where(x == mx, jf, 128.0), axis=0, keepdims=True)
        hit = jf == first
        picked = jnp.where(hit, 1.0, picked)
        x = jnp.where(hit, -jnp.inf, x)
    used = jnp.where(valid & (picked > 0.0), 1.0, 0.0)
    bias = jnp.where(used > 0.0, 0.0, SEL_OFF).T
    for h in range(N_HEADS):
        qaug_ref[h, :, 128:256] = bias.astype(BF16)
    used_ref[...] = jnp.broadcast_to(jnp.max(used.T, axis=0, keepdims=True), (8, 128))


def _nsa_flash_kernel(need_ref, qaug_ref, kv_ref, o_ref, kaug_s, v_s, *, window):
    tq, tk = NSA_TQ, NSA_TK
    s_len = kv_ref.shape[0]
    tk = min(tk, s_len)
    kdim = 128 if window else 256
    qi = pl.program_id(1)
    q0 = qi * tq

    @pl.when(qi == 0)
    def _():
        def build(i, c):
            r0 = pl.multiple_of(i * tk, tk)
            kv = kv_ref[pl.ds(r0, tk), :]
            k = kv[:, :HEAD_DIM]
            parts = [k, k]
            if not window:
                blk = (r0 + lax.broadcasted_iota(I32, (tk, 128), 0)) // NSA_SEL_BLOCK
                parts.append(jnp.where(blk == lax.broadcasted_iota(I32, (tk, 128), 1), 1.0, 0.0))
            kaug_s[pl.ds(r0, tk), :] = jnp.concatenate(parts, axis=1).astype(BF16)
            v_s[pl.ds(r0, tk), :] = kv[:, HEAD_DIM:].astype(BF16)
            return c
        lax.fori_loop(0, s_len // tk, build, 0)

    q = qaug_ref[...][:, :, :kdim].reshape(N_HEADS * tq, kdim)
    row = lax.broadcasted_iota(I32, (N_HEADS * tq, tk), 0)
    t_q = q0 + (row & (tq - 1))
    head = lax.broadcasted_iota(I32, (N_HEADS * tq, 1), 0) // tq
    slope = jnp.where(head == 0, NSA_SLOPES[0], jnp.where(head == 1, NSA_SLOPES[1],
                      jnp.where(head == 2, NSA_SLOPES[2], NSA_SLOPES[3])))
    col = lax.broadcasted_iota(I32, (N_HEADS * tq, tk), 1)

    n_kt = s_len // tk
    need_base = (pl.program_id(0) * pl.num_programs(1) + qi) * n_kt

    def body(kt, carry):
        if window:
            return tile(kt, carry)
        return lax.cond(need_ref[need_base + kt] != 0, lambda c: tile(kt, c), lambda c: c, carry)

    def tile(kt, carry):
        m, l, acc = carry
        k0 = pl.multiple_of(kt * tk, tk)
        s = lax.dot_general(q, kaug_s[pl.ds(k0, tk), :], (((1,), (1,)), ((), ())), preferred_element_type=F32)
        t_k = k0 + col
        s = s + slope * (t_k - q0).astype(F32)
        ok = t_k <= t_q
        if window:
            ok = ok & (t_q - t_k < NSA_WINDOW)
        m_new = jnp.maximum(m, jnp.max(jnp.where(ok, s, NEG_BIG), axis=1, keepdims=True))
        p = jnp.where(ok, jnp.exp(s - m_new), 0.0)
        alpha = jnp.exp(m - m_new)
        l = alpha * l + jnp.sum(p, axis=1, keepdims=True)
        acc = alpha * acc + jnp.dot(p.astype(BF16), v_s[pl.ds(k0, tk), :], preferred_element_type=F32)
        return m_new, l, acc

    lo = jnp.maximum(q0 - NSA_WINDOW, 0) // tk if window else 0
    hi = (q0 + tq - 1) // tk + 1
    init = (jnp.full((N_HEADS * tq, 1), NEG_BIG, F32), jnp.zeros((N_HEADS * tq, 1), F32),
            jnp.zeros((N_HEADS * tq, HEAD_DIM), F32))
    m, l, acc = lax.fori_loop(lo, hi, body, init)
    o_ref[...] = (acc / l).reshape(N_HEADS, tq, HEAD_DIM)


def _nsa_overlap(s):
    nbp, nsel = s // NSA_STRIDE, s // NSA_SEL_BLOCK
    cs = np.arange(nbp) * NSA_STRIDE
    ss = np.arange(128) * NSA_SEL_BLOCK
    ov = np.minimum(cs[:, None] + 2 * NSA_STRIDE, ss[None, :] + NSA_SEL_BLOCK) - np.maximum(cs[:, None], ss[None, :])
    ov = np.clip(ov, 0, None) / (2 * NSA_STRIDE)
    ov[nbp - 1:, :] = 0.0
    ov[:, nsel:] = 0.0
    return jnp.asarray(ov, BF16)


def _nsa_branches(cols, cmp_pos, cmp_w1, cmp_w2):
    bsz, s, _ = cols.shape
    nbp = s // NSA_STRIDE
    tq = NSA_TQ
    kvcmp = pl.pallas_call(
        _nsa_compress_kernel,
        grid=(bsz,),
        in_specs=[pl.BlockSpec((None, s, 128), lambda b: (b, 0, COL["nsa_kvc"] // 128)),
                  pl.BlockSpec((2, 32, 64), lambda b: (0, 0, 0)),
                  pl.BlockSpec((2, 2048, 256), lambda b: (0, 0, 0)),
                  pl.BlockSpec((2, 256, 64), lambda b: (0, 0, 0))],
        out_specs=pl.BlockSpec((None, nbp, 128), lambda b: (b, 0, 0)),
        out_shape=jax.ShapeDtypeStruct((bsz, nbp, 128), F32),
        compiler_params=_cparams(("arbitrary",), 48),
    )(cols, cmp_pos, cmp_w1.astype(BF16), cmp_w2)
    head_out = lambda dt, w: (pl.BlockSpec((None, N_HEADS, tq, w), lambda b, i: (b, 0, i, 0)),
                              jax.ShapeDtypeStruct((bsz, N_HEADS, s, w), dt))
    (oc_spec, oc_shape), (qa_spec, qa_shape) = head_out(F32, HEAD_DIM), head_out(BF16, 256)
    nqt = s // tq
    o_cmp, qaug, used = pl.pallas_call(
        _nsa_cmp_kernel,
        grid=(bsz, nqt),
        in_specs=[pl.BlockSpec((None, tq, 256), lambda b, i: (b, i, COL["nsa_q"] // 256)),
                  pl.BlockSpec((None, nbp, 128), lambda b, i: (b, 0, 0)),
                  pl.BlockSpec((nbp, 128), lambda b, i: (0, 0))],
        out_specs=[oc_spec, qa_spec, pl.BlockSpec((None, None, 8, 128), lambda b, i: (b, i, 0, 0))],
        out_shape=[oc_shape, qa_shape, jax.ShapeDtypeStruct((bsz, nqt, 8, 128), F32)],
        compiler_params=_cparams(("arbitrary", "arbitrary"), 48),
    )(cols, kvcmp, _nsa_overlap(s))
    tk = min(NSA_TK, s)
    bpt = tk // NSA_SEL_BLOCK
    need = (used[:, :, 0, :(s // tk) * bpt].reshape(bsz, nqt, s // tk, bpt).max(axis=-1) > 0).astype(I32).reshape(-1)

    def flash(name, window):
        hmap = lambda b, i, need_ref: (b, 0, i, 0)
        return pl.pallas_call(
            functools.partial(_nsa_flash_kernel, window=window),
            grid_spec=pltpu.PrefetchScalarGridSpec(
                num_scalar_prefetch=1,
                grid=(bsz, nqt),
                in_specs=[pl.BlockSpec((None, N_HEADS, tq, 256), hmap),
                          pl.BlockSpec((None, s, 128), lambda b, i, need_ref: (b, 0, COL[name] // 128))],
                out_specs=pl.BlockSpec((None, N_HEADS, tq, HEAD_DIM), hmap),
                scratch_shapes=[pltpu.VMEM((s, 128 if window else 256), BF16), pltpu.VMEM((s, HEAD_DIM), BF16)]),
            out_shape=oc_shape,
            compiler_params=_cparams(("arbitrary", "arbitrary"), 48),
        )(need, qaug, cols)

    return o_cmp, flash("nsa_kvs", False), flash("nsa_kvw", True)


def _nsa_mixer(cols, cmp_pos, cmp_w1, cmp_w2):
    o_cmp, o_sel, o_win = _nsa_branches(cols, cmp_pos, cmp_w1, cmp_w2)
    bsz, s, _ = cols.shape
    g = jax.nn.sigmoid(cols[..., COL["nsa_gate"]:COL["nsa_gate"] + 12]).reshape(bsz, s, N_HEADS, 3)
    tr = lambda o: jnp.moveaxis(o, 1, 2)
    out = g[..., 0:1] * tr(o_cmp) + g[..., 1:2] * tr(o_sel) + g[..., 2:3] * tr(o_win)
    return out.reshape(bsz, s, 256)


MIX_TILE = 256


def _mix_out_kernel(h_ref, ssm_ref, lru_ref, rw_ref, oc_ref, os_ref, ow_ref, gate_ref, w_ref, o_ref):
    acc = h_ref[...]
    acc = acc + _dot(ssm_ref[...], w_ref[0:256, :])
    acc = acc + _dot(lru_ref[...], w_ref[256:512, :])
    acc = acc + _dot(rw_ref[...], w_ref[512:768, :])
    g = _sigmoid(gate_ref[...])
    for h in range(N_HEADS):
        y = (g[:, 3 * h:3 * h + 1] * oc_ref[h] + g[:, 3 * h + 1:3 * h + 2] * os_ref[h]
             + g[:, 3 * h + 2:3 * h + 3] * ow_ref[h])
        acc = acc + _dot(y, w_ref[768 + HEAD_DIM * h:768 + HEAD_DIM * (h + 1), :])
    o_ref[...] = acc


def _mix_out(h, y_ssm, y_lru, y_rw, o_cmp, o_sel, o_win, cols, w_out):
    bsz, s, d = h.shape
    t = MIX_TILE
    tok = lambda w: pl.BlockSpec((None, t, w), lambda b, i: (b, i, 0))
    hd = pl.BlockSpec((None, N_HEADS, t, HEAD_DIM), lambda b, i: (b, 0, i, 0))
    return pl.pallas_call(
        _mix_out_kernel,
        grid=(bsz, s // t),
        in_specs=[tok(d), tok(256), tok(256), tok(256), hd, hd, hd,
                  pl.BlockSpec((None, t, 128), lambda b, i: (b, i, COL["nsa_gate"] // 128)),
                  pl.BlockSpec((4 * GROUP_WIDTH, d), lambda b, i: (0, 0))],
        out_specs=tok(d),
        out_shape=jax.ShapeDtypeStruct((bsz, s, d), F32),
        compiler_params=_cparams(("arbitrary", "arbitrary")),
    )(h, y_ssm, y_lru, y_rw, o_cmp, o_sel, o_win, cols, w_out.astype(BF16))


PEER_HEADS = 8
PEER_NKEYS = 128
PEER_TOPK = 16
PEER_TT = 128
PEER_TQ = 256
PEER_ACT_UNROLL = 16
PEER_OUT_UNROLL = 128


PEER_ROW = 4


def _store_packed(x, o_ref):
    rows, d = x.shape
    b = pltpu.bitcast(x, I32)
    r = b + 0x7FFF + ((b >> 16) & 1)
    word = (r[:, d // 2:] & jnp.int32(-65536)) | lax.shift_right_logical(r[:, :d // 2], 16)
    for s in range(PEER_ROW):
        o_ref[pl.ds(s, rows, stride=PEER_ROW), :] = word[:, 128 * s:128 * (s + 1)]


def _norm_matmul2_kernel(x_ref, g_ref, w_ref, o_ref, xpk_ref, xn_ref):
    @pl.when(pl.program_id(1) == 0)
    def _():
        x = x_ref[...]
        y = x * lax.rsqrt(jnp.mean(x * x, axis=-1, keepdims=True) + NORM_EPS) * g_ref[...]
        _store_packed(y, xpk_ref)
        xn_ref[...] = y.astype(BF16)

    o_ref[...] = jnp.dot(xn_ref[...], w_ref[...], preferred_element_type=F32)


def _norm_matmul2(x, g, w, tm, tn):
    m, k = x.shape
    n = w.shape[1]
    return pl.pallas_call(
        _norm_matmul2_kernel,
        grid=(m // tm, n // tn),
        in_specs=[pl.BlockSpec((tm, k), lambda i, j: (i, 0)),
                  pl.BlockSpec((1, k), lambda i, j: (0, 0)),
                  pl.BlockSpec((k, tn), lambda i, j: (0, j))],
        out_specs=[pl.BlockSpec((tm, tn), lambda i, j: (i, j)),
                   pl.BlockSpec((tm * PEER_ROW, 128), lambda i, j: (i, 0))],
        out_shape=[jax.ShapeDtypeStruct((m, n), F32), jax.ShapeDtypeStruct((m * PEER_ROW, 128), I32)],
        scratch_shapes=[pltpu.VMEM((tm, k), BF16)],
        compiler_params=_cparams(("arbitrary", "arbitrary")),
    )(x, g, w)


def _top_rows(x, n, payload=None):
    r = x.shape[0]
    rid = lax.broadcasted_iota(I32, x.shape, 0).astype(F32)
    vals, ids, pays = [], [], []
    for _ in range(n):
        mx = jnp.max(x, axis=0, keepdims=True)
        first = jnp.min(jnp.where(x == mx, rid, float(r)), axis=0, keepdims=True)
        hit = rid == first
        vals.append(mx)
        ids.append(first)
        if payload is not None:
            pays.append(jnp.sum(jnp.where(hit, payload, 0.0), axis=0, keepdims=True))
        x = jnp.where(hit, -jnp.inf, x)
    cat = lambda v: jnp.concatenate(v, axis=0)
    return (cat(vals), cat(ids)) + ((cat(pays),) if payload is not None else ())


def _peer_route_kernel(q_ref, keys_ref, eidx_ref, gate_ref):
    k = PEER_TOPK
    tq = q_ref.shape[0]
    gates, eids = [], []
    for h in range(PEER_HEADS):
        s, i = [], []
        for c in range(2):
            qh = q_ref[:, 128 * (2 * h + c):128 * (2 * h + c + 1)]
            sv, si = _top_rows(_dot3_nt(keys_ref[h, c], qh), k)
            s.append(sv)
            i.append(si)
        cand, cidx = [], []
        for a in range(k):
            nb = k // (a + 1)
            cand.append(s[0][a:a + 1, :] + s[1][:nb])
            cidx.append(i[0][a:a + 1, :] * float(PEER_NKEYS) + i[1][:nb])
        n_c = sum(c.shape[0] for c in cand)
        pad = (-n_c) % 8
        cand.append(jnp.full((pad, tq), -jnp.inf, F32))
        cidx.append(jnp.zeros((pad, tq), F32))
        top, _, eidx = _top_rows(jnp.concatenate(cand, axis=0), k, payload=jnp.concatenate(cidx, axis=0))
        e = jnp.exp(top - top[0:1, :])
        gates.append(e / jnp.sum(e, axis=0, keepdims=True))
        eids.append(eidx)
    gate_ref[...] = jnp.concatenate(gates, axis=0).T
    eidx_ref[...] = (jnp.concatenate(eids, axis=0).T * float(PEER_ROW)).astype(I32)


def _peer_route(q, keys):
    t = q.shape[0]
    tq = min(PEER_TQ, t)
    nk = PEER_HEADS * PEER_TOPK
    return pl.pallas_call(
        _peer_route_kernel,
        grid=(t // tq,),
        in_specs=[pl.BlockSpec((tq, q.shape[1]), lambda i: (i, 0)),
                  pl.BlockSpec(keys.shape, lambda i: (0, 0, 0, 0))],
        out_specs=[pl.BlockSpec((tq, nk), lambda i: (i, 0)), pl.BlockSpec((tq, nk), lambda i: (i, 0))],
        out_shape=[jax.ShapeDtypeStruct((t, nk), I32), jax.ShapeDtypeStruct((t, nk), F32)],
        compiler_params=_cparams(("arbitrary",), 48),
    )(q, keys)


PACK_TILE = 512


def _pack_kernel(t_ref, o_ref):
    _store_packed(t_ref[...], o_ref)


def _pack_table(tab):
    e, d = tab.shape
    te = PACK_TILE
    return pl.pallas_call(
        _pack_kernel,
        grid=(e // te,),
        in_specs=[pl.BlockSpec((te, d), lambda i: (i, 0))],
        out_specs=pl.BlockSpec((te * PEER_ROW, 128), lambda i: (i, 0)),
        out_shape=jax.ShapeDtypeStruct((e * PEER_ROW, 128), I32),
        compiler_params=_cparams(("arbitrary",)),
    )(tab)


def _expert_halves(tbl_ref, r):
    word = tbl_ref[pl.ds(pl.multiple_of(r, PEER_ROW), PEER_ROW), :]
    return pltpu.bitcast(word << 16, F32), pltpu.bitcast(word & jnp.int32(-65536), F32)


_BITREV8 = (0, 4, 2, 6, 1, 5, 3, 7)


def _rowsum_pairs(cur):
    sub = lax.broadcasted_iota(I32, (8, 128), 0)
    for h in (2, 1):
        m = (sub & h) == 0
        cur = [jnp.where(m, x, y) + jnp.where(m, pltpu.roll(x, 8 - h, axis=0), pltpu.roll(y, h, axis=0))
               for x, y in zip(cur[0::2], cur[1::2])]
    return cur[0]


def _peer_act_kernel(idx_ref, x_ref, gate_ref, tbl_ref, o_ref, q_ref, s_ref):
    tt, nk = gate_ref.shape
    sub = lax.broadcasted_iota(I32, (8, 128), 0)

    @pl.when(pl.program_id(0) == 0)
    def _():
        q_ref[...] = jnp.zeros_like(q_ref)

    def finish(slot, t):
        s_ref[pl.ds(t, 1), :] = jnp.sum(q_ref[slot].T, axis=0, keepdims=True)

    def tok(t, c):
        finish((t + 1) % 2, jnp.maximum(t - 1, 0))
        xw = x_ref[pl.ds(pl.multiple_of(t * PEER_ROW, PEER_ROW), PEER_ROW), :]
        xb = pltpu.bitcast(jnp.concatenate([xw, xw], axis=0), BF16)
        slot = t % 2
        for g in range(nk // 8):
            fs = []
            for i in range(4):
                ra = pl.multiple_of(idx_ref[t * nk + 8 * g + _BITREV8[2 * i]], PEER_ROW)
                rb = pl.multiple_of(idx_ref[t * nk + 8 * g + _BITREV8[2 * i + 1]], PEER_ROW)
                pa = pltpu.bitcast(pltpu.bitcast(tbl_ref[pl.ds(ra + PEER_ROW, 8), :], BF16) * xb, I32)
                pb = pltpu.bitcast(pltpu.bitcast(tbl_ref[pl.ds(rb, 8), :], BF16) * xb, I32)
                prod = jnp.where(sub < PEER_ROW, pa, pb)
                fs.append(pltpu.bitcast(prod << 16, F32) + pltpu.bitcast(prod & jnp.int32(-65536), F32))
            q_ref[slot, 8 * g:8 * (g + 1), :] = _rowsum_pairs(fs)
        return c

    lax.fori_loop(0, tt, tok, 0)
    finish((tt - 1) % 2, tt - 1)
    o_ref[...] = gate_ref[...] * _gelu_tanh(s_ref[...])


def _peer_out_kernel(idx_ref, w_ref, tbl_ref, h_ref, o_ref):
    tt = h_ref.shape[0]
    nk = idx_ref.shape[0] // tt

    def tok(t, c):
        def grp(g, accs):
            accs = list(accs)
            k0 = t * nk + g * PEER_OUT_UNROLL
            for j in range(PEER_OUT_UNROLL):
                w = w_ref[k0 + j]
                vlo, vhi = _expert_halves(tbl_ref, idx_ref[k0 + j])
                a = 2 * (j % 2)
                accs[a] = accs[a] + vlo * w
                accs[a + 1] = accs[a + 1] + vhi * w
            return tuple(accs)

        z = jnp.zeros((PEER_ROW, 128), F32)
        l0, h0, l1, h1 = lax.fori_loop(0, nk // PEER_OUT_UNROLL, grp, (z, z, z, z))
        o_ref[t] = h_ref[t] + jnp.concatenate([l0 + l1, h0 + h1], axis=0)
        return c

    lax.fori_loop(0, tt, tok, 0)


def _peer_experts(eidx_t, gate_t, xn3, h3, tbl_u, tbl_v):
    t, nk = eidx_t.shape
    tt = min(PEER_TT, t)
    smem = lambda: pl.BlockSpec((tt * nk,), lambda i: (i,), memory_space=pltpu.SMEM)
    vm = lambda: pl.BlockSpec((tt, nk), lambda i: (i, 0))
    tok = lambda: pl.BlockSpec((tt, 8, 128), lambda i: (i, 0, 0))
    tbl = lambda a: pl.BlockSpec(a.shape, lambda i: (0, 0), pipeline_mode=pl.Buffered(1))
    eidx_flat = eidx_t.reshape(t * nk)
    w_t = pl.pallas_call(
        _peer_act_kernel,
        grid=(t // tt,),
        in_specs=[smem(), pl.BlockSpec((tt * PEER_ROW, 128), lambda i: (i, 0)), vm(), tbl(tbl_u)],
        out_specs=vm(),
        out_shape=jax.ShapeDtypeStruct((t, nk), F32),
        scratch_shapes=[pltpu.VMEM((2, nk, 128), F32), pltpu.VMEM((tt, nk), F32)],
        compiler_params=_cparams(("arbitrary",), 48),
    )(eidx_flat, xn3, gate_t, tbl_u)
    return pl.pallas_call(
        _peer_out_kernel,
        grid=(t // tt,),
        in_specs=[smem(), smem(), tbl(tbl_v), tok()],
        out_specs=tok(),
        out_shape=jax.ShapeDtypeStruct(h3.shape, F32),
        compiler_params=_cparams(("arbitrary",), 48),
    )(eidx_flat, w_t.reshape(t * nk), tbl_v, h3)


def _peer_block(h, g, wq, keys, u, v):
    t, d = h.shape
    q, xpk = _norm_matmul2(h, g.reshape(1, d), wq.astype(BF16), min(512, t), 1024)
    eidx_t, gate_t = _peer_route(q, keys)
    tbl_u = jnp.pad(_pack_table(u), ((PEER_ROW, PEER_ROW), (0, 0)))
    out = _peer_experts(eidx_t, gate_t, xpk, h.reshape(t, 8, 128), tbl_u, _pack_table(v))
    return out.reshape(t, d)


def _rmsnorm_kernel(x_ref, g_ref, o_ref):
    x = x_ref[...]
    o_ref[...] = x * lax.rsqrt(jnp.mean(x * x, axis=-1, keepdims=True) + NORM_EPS) * g_ref[...]


def _rmsnorm(x, g, tm=512):
    m, d = x.shape
    return pl.pallas_call(
        _rmsnorm_kernel,
        grid=(m // tm,),
        in_specs=[pl.BlockSpec((tm, d), lambda i: (i, 0)), pl.BlockSpec((1, d), lambda i: (0, 0))],
        out_specs=pl.BlockSpec((tm, d), lambda i: (i, 0)),
        out_shape=jax.ShapeDtypeStruct((m, d), F32),
        compiler_params=_cparams(("arbitrary",)),
    )(x, g.reshape(1, d))


def kernel(x, mix_norm, w_in, w_out, ssm_conv_w, ssm_conv_b, ssm_dt_bias, ssm_a_log, ssm_d, ssm_norm, lru_conv_w, lru_conv_b, lru_wa, lru_ba, lru_wi, lru_bi, lru_lambda, rwkv_mu, rwkv_w0, rwkv_w2, rwkv_a0, rwkv_a2, rwkv_g2, rwkv_kk, rwkv_ka, rwkv_rk, rwkv_ln_w, rwkv_ln_b, nsa_cmp_pos, nsa_cmp_w1, nsa_cmp_w2, ffn_norm, peer_wq, peer_keys, peer_u, peer_v, final_norm):
    bsz, s, d = x.shape
    t = bsz * s
    h = x
    for l in range(w_in.shape[0]):
        w_l = _arrange_cols(w_in[l]).astype(BF16)
        cols = _norm_matmul(h.reshape(t, d), mix_norm[l].reshape(1, d), w_l, 512, 640).reshape(bsz, s, NCOL)
        y_ssm = _ssd_mixer(cols, ssm_conv_w[l], ssm_conv_b[l], ssm_dt_bias[l], ssm_a_log[l], ssm_d[l], ssm_norm[l])
        y_lru = _lru_mixer(cols, lru_conv_w[l], lru_conv_b[l], lru_wa[l], lru_ba[l], lru_wi[l], lru_bi[l], lru_lambda[l])
        y_rw = _rwkv_mixer(cols, rwkv_mu[l], rwkv_w0[l], rwkv_w2[l], rwkv_a0[l], rwkv_a2[l], rwkv_g2[l],
                           rwkv_kk[l], rwkv_ka[l], rwkv_rk[l], rwkv_ln_w[l], rwkv_ln_b[l])
        o_cmp, o_sel, o_win = _nsa_branches(cols, nsa_cmp_pos[l], nsa_cmp_w1[l], nsa_cmp_w2[l])
        h = _mix_out(h, y_ssm, y_lru, y_rw, o_cmp, o_sel, o_win, cols, w_out[l])
        h = _peer_block(h.reshape(t, d), ffn_norm[l], peer_wq[l], peer_keys[l], peer_u[l], peer_v[l]).reshape(bsz, s, d)
    return _rmsnorm(h.reshape(t, d), final_norm).reshape(bsz, s, d)
```

```python
import functools
import math

import numpy as np
import jax
import jax.numpy as jnp
from jax import lax
from jax.experimental import pallas as pl
from jax.experimental.pallas import tpu as pltpu

F32 = jnp.float32
BF16 = jnp.bfloat16
I32 = jnp.int32

NORM_EPS = 1e-6
GROUP_WIDTH = 256
HEAD_DIM = 64
N_HEADS = 4
LRU_C = 8.0
NEG_BIG = -1e30


COL = dict(ssd_z=0, ssd_x=256, lru_gate=512, lru_x=768, rw_r=1024, rw_k=1280, rw_v=1536, rw_wa=1792, rw_g=1920,
           nsa_q=2048, nsa_kvc=2304, nsa_kvs=2432, nsa_kvw=2560, nsa_gate=2688, ssd_b=2816, ssd_c=2944, ssd_dt=3072)
NCOL = 3200


def _col_perm():
    src = np.full((NCOL,), -1, np.int64)

    def put(dst, s0, n):
        src[dst:dst + n] = np.arange(s0, s0 + n)

    put(COL["ssd_z"], 0, 256)
    put(COL["ssd_x"], 256, 256)
    put(COL["ssd_b"], 512, 128)
    put(COL["ssd_c"], 640, 128)
    put(COL["ssd_dt"], 768, 4)
    put(COL["lru_gate"], 772, 256)
    put(COL["lru_x"], 1028, 256)
    put(COL["rw_r"], 1284, 1024)
    put(COL["nsa_q"], 2308, 256)
    put(COL["nsa_kvc"], 2564, 128)
    put(COL["nsa_kvs"], 2692, 128)
    put(COL["nsa_kvw"], 2820, 128)
    put(COL["nsa_gate"], 2948, 12)
    return src


def _arrange_cols(a):
    src = _col_perm()
    out = jnp.take(a, jnp.asarray(np.maximum(src, 0)), axis=-1)
    return jnp.where(jnp.asarray(src >= 0), out, jnp.zeros((), a.dtype))


def _cparams(sem, vmem_mb=None):
    kw = dict(dimension_semantics=sem)
    if vmem_mb is not None:
        kw["vmem_limit_bytes"] = vmem_mb * 1024 * 1024
    return pltpu.CompilerParams(**kw)


def _dot(a, b):
    return jnp.dot(a.astype(BF16), b.astype(BF16), preferred_element_type=F32)


def _dot_nt(a, b):
    return lax.dot_general(a.astype(BF16), b.astype(BF16), (((1,), (1,)), ((), ())), preferred_element_type=F32)


def _dot_tn(a, b):
    return lax.dot_general(a.astype(BF16), b.astype(BF16), (((0,), (0,)), ((), ())), preferred_element_type=F32)


def _split3(x):
    h = x.astype(BF16)
    r = x - h.astype(F32)
    m = r.astype(BF16)
    l = (r - m.astype(F32)).astype(BF16)
    return h, m, l


def _split2(x):
    h = x.astype(BF16)
    return h, (x - h.astype(F32)).astype(BF16)


def _dot_exact_rhs(a, b_exact):
    h, m, l = _split3(a)
    d = lambda t: jnp.dot(t, b_exact, preferred_element_type=F32)
    return d(h) + d(m) + d(l)


def _dot_exact_lhs(a_exact, b):
    h, m, l = _split3(b)
    d = lambda t: jnp.dot(a_exact, t, preferred_element_type=F32)
    return d(h) + d(m) + d(l)


def _sigmoid(x):
    return 1.0 / (1.0 + jnp.exp(-x))


def _silu(x):
    return x * _sigmoid(x)


def _gelu_tanh(x):
    return 0.5 * x * (1.0 + jnp.tanh(math.sqrt(2.0 / math.pi) * (x + 0.044715 * (x * x * x))))


def _softplus(x):
    return jnp.maximum(x, 0.0) + jnp.log1p(jnp.exp(-jnp.abs(x)))


def _conv4(cur, tail, w, b):
    c = cur.shape[1]
    row8 = lax.broadcasted_iota(I32, (8, c), 0)
    acc = cur * w[3:4, :] + b
    for k in (1, 2, 3):
        cr = pltpu.roll(cur, k, axis=0)
        tr = pltpu.roll(tail, k, axis=0)
        head = jnp.where(row8 < k, tr, cr[:8])
        sh = jnp.concatenate([head, cr[8:]], axis=0)
        acc = acc + sh * w[3 - k:4 - k, :]
    return acc


def _norm_matmul_kernel(x_ref, g_ref, w_ref, o_ref, xn_ref):
    @pl.when(pl.program_id(1) == 0)
    def _():
        x = x_ref[...]
        y = x * lax.rsqrt(jnp.mean(x * x, axis=-1, keepdims=True) + NORM_EPS) * g_ref[...]
        xn_ref[...] = y.astype(BF16)

    o_ref[...] = jnp.dot(xn_ref[...], w_ref[...], preferred_element_type=F32)


def _norm_matmul(x, g, w, tm, tn):
    m, k = x.shape
    n = w.shape[1]
    return pl.pallas_call(
        _norm_matmul_kernel,
        grid=(m // tm, n // tn),
        in_specs=[pl.BlockSpec((tm, k), lambda i, j: (i, 0)),
                  pl.BlockSpec((1, k), lambda i, j: (0, 0)),
                  pl.BlockSpec((k, tn), lambda i, j: (0, j))],
        out_specs=pl.BlockSpec((tm, tn), lambda i, j: (i, j)),
        out_shape=jax.ShapeDtypeStruct((m, n), F32),
        scratch_shapes=[pltpu.VMEM((tm, k), BF16)],
        compiler_params=_cparams(("arbitrary", "arbitrary")),
    )(x, g, w)


SSD_CHUNK = 128


def _ssd_kernel(z_ref, x_ref, b_ref, c_ref, dt_ref, cwx_ref, cbx_ref, cwb_ref, cbb_ref, cwc_ref, cbc_ref,
                dtb_ref, alog_ref, d_ref, nw_ref, o_ref, tx_ref, tb_ref, tc_ref, st_ref):
    t = SSD_CHUNK

    @pl.when(pl.program_id(1) == 0)
    def _():
        tx_ref[...] = jnp.zeros_like(tx_ref)
        tb_ref[...] = jnp.zeros_like(tb_ref)
        tc_ref[...] = jnp.zeros_like(tc_ref)
        st_ref[...] = jnp.zeros_like(st_ref)

    xr, br, cr = x_ref[...], b_ref[...], c_ref[...]
    xs = _silu(_conv4(xr, tx_ref[...], cwx_ref[...], cbx_ref[...]))
    bm = _silu(_conv4(br, tb_ref[...], cwb_ref[...], cbb_ref[...]))
    cm = _silu(_conv4(cr, tc_ref[...], cwc_ref[...], cbc_ref[...]))
    tx_ref[...] = xr[t - 8:]
    tb_ref[...] = br[t - 8:]
    tc_ref[...] = cr[t - 8:]

    dt = _softplus(dt_ref[...] + dtb_ref[...])
    a = dt * (-jnp.exp(alog_ref[...]))
    ri = lax.broadcasted_iota(I32, (t, t), 0)
    ci = lax.broadcasted_iota(I32, (t, t), 1)
    tril = ri >= ci
    acs = _dot_exact_lhs(jnp.where(tril, 1.0, 0.0).astype(BF16), a)
    acs_t = acs.T
    d_row = d_ref[...]
    ys = []
    for h in range(N_HEADS):
        g = h // 2
        lo, hi = HEAD_DIM * h, HEAD_DIM * (h + 1)
        b_g = bm[:, HEAD_DIM * g:HEAD_DIM * (g + 1)]
        c_g = cm[:, HEAD_DIM * g:HEAD_DIM * (g + 1)]
        x_h = xs[:, lo:hi]
        acs_c = acs[:, h:h + 1]
        lmat = jnp.exp(jnp.where(tril, acs_c - acs_t[h:h + 1, :], NEG_BIG))
        scores = _dot_nt(c_g, b_g) * lmat
        xdt = x_h * dt[:, h:h + 1]
        st = st_ref[h]
        y = _dot(scores, xdt) + _dot(c_g, st) * jnp.exp(acs_c) + x_h * d_row[:, lo:hi]
        a_last = acs[t - 1:t, h:h + 1]
        bw = b_g * jnp.exp(a_last - acs_c)
        st_ref[h] = st * jnp.exp(a_last) + _dot_tn(bw, xdt)
        ys.append(y)
    y = jnp.concatenate(ys, axis=1) * _silu(z_ref[...])
    outs = []
    for g in range(2):
        yg = y[:, 128 * g:128 * (g + 1)]
        outs.append(yg * lax.rsqrt(jnp.mean(yg * yg, axis=-1, keepdims=True) + NORM_EPS))
    o_ref[...] = jnp.concatenate(outs, axis=1) * nw_ref[...]


def _ssd_mixer(cols, conv_w, conv_b, dt_bias, a_log, d_skip, norm_w):
    bsz, s, _ = cols.shape
    t = SSD_CHUNK
    row = lambda v, n: jnp.pad(v.reshape(1, -1), ((0, 0), (0, n - v.size)))
    full = lambda shp: pl.BlockSpec(shp, lambda b, c: (0,) * len(shp))
    colspec = lambda w, name: pl.BlockSpec((None, t, w), lambda b, c: (b, c, COL[name] // w))
    return pl.pallas_call(
        _ssd_kernel,
        grid=(bsz, s // t),
        in_specs=[colspec(256, "ssd_z"), colspec(256, "ssd_x"), colspec(128, "ssd_b"), colspec(128, "ssd_c"),
                  colspec(128, "ssd_dt"),
                  full((4, 256)), full((1, 256)), full((4, 128)), full((1, 128)), full((4, 128)), full((1, 128)),
                  full((1, 128)), full((1, 128)), full((1, 256)), full((1, 256))],
        out_specs=pl.BlockSpec((None, t, 256), lambda b, c: (b, c, 0)),
        out_shape=jax.ShapeDtypeStruct((bsz, s, 256), F32),
        scratch_shapes=[pltpu.VMEM((8, 256), F32), pltpu.VMEM((8, 128), F32), pltpu.VMEM((8, 128), F32),
                        pltpu.VMEM((N_HEADS, HEAD_DIM, HEAD_DIM), F32)],
        compiler_params=_cparams(("arbitrary", "arbitrary")),
    )(cols, cols, cols, cols, cols,
      conv_w[:, :256], conv_b[:256].reshape(1, 256), conv_w[:, 256:384], conv_b[256:384].reshape(1, 128),
      conv_w[:, 384:512], conv_b[384:512].reshape(1, 128),
      row(dt_bias, 128), row(a_log, 128), jnp.repeat(d_skip, HEAD_DIM).reshape(1, 256), norm_w.reshape(1, 256))


LRU_TILE = 256


def _lru_kernel(gate_ref, xr_ref, cw_ref, cb_ref, wa_ref, ba_ref, wi_ref, bi_ref, lam_ref, o_ref, tail_ref, h_ref):
    t = LRU_TILE

    @pl.when(pl.program_id(1) == 0)
    def _():
        tail_ref[...] = jnp.zeros_like(tail_ref)
        h_ref[...] = jnp.zeros_like(h_ref)

    xr = xr_ref[...]
    xc = _conv4(xr, tail_ref[...], cw_ref[...], cb_ref[...])
    tail_ref[...] = xr[t - 8:]
    r = _sigmoid(_dot(xc, wa_ref[...]) + ba_ref[...])
    i = _sigmoid(_dot(xc, wi_ref[...]) + bi_ref[...])
    log_a = LRU_C * r * (-_softplus(-lam_ref[...]))
    a = jnp.exp(log_a)
    u = jnp.sqrt(-jnp.tanh(log_a) * (a * a + 1.0)) * (i * xc)
    rows = lax.broadcasted_iota(I32, (t, GROUP_WIDTH), 0)
    d = 1
    while d < t:
        a_sh = pltpu.roll(a, d, axis=0)
        u_sh = pltpu.roll(u, d, axis=0)
        m = rows >= d
        u = jnp.where(m, a * u_sh + u, u)
        a = jnp.where(m, a * a_sh, a)
        d *= 2
    h = a * h_ref[7:8, :] + u
    h_ref[...] = h[t - 8:]
    o_ref[...] = h * _gelu_tanh(gate_ref[...])


def _lru_mixer(cols, conv_w, conv_b, wa, ba, wi, bi, lam):
    bsz, s, _ = cols.shape
    t = LRU_TILE
    cb = COL["lru_gate"] // 256
    bd = lambda w: jax.scipy.linalg.block_diag(*[w[k] for k in range(w.shape[0])]).astype(BF16)
    full = lambda shp: pl.BlockSpec(shp, lambda b, c: (0,) * len(shp))
    return pl.pallas_call(
        _lru_kernel,
        grid=(bsz, s // t),
        in_specs=[pl.BlockSpec((None, t, 256), lambda b, c: (b, c, cb)),
                  pl.BlockSpec((None, t, 256), lambda b, c: (b, c, cb + 1)),
                  full((4, 256)), full((1, 256)), full((256, 256)), full((1, 256)), full((256, 256)), full((1, 256)),
                  full((1, 256))],
        out_specs=pl.BlockSpec((None, t, 256), lambda b, c: (b, c, 0)),
        out_shape=jax.ShapeDtypeStruct((bsz, s, 256), F32),
        scratch_shapes=[pltpu.VMEM((8, 256), F32), pltpu.VMEM((8, 256), F32)],
        compiler_params=_cparams(("arbitrary", "arbitrary")),
    )(cols, cols, conv_w, conv_b.reshape(1, 256), bd(wa), ba.reshape(1, 256), bd(wi), bi.reshape(1, 256),
      lam.reshape(1, 256))


RW_TILE = 128
RW_GROUP = 4
RW_READOUT_PASSES = 1
RWKV_GN_EPS = 64e-5


def _segsum_bc(x):
    r = x.shape[0]
    m = lax.broadcasted_iota(I32, (r, 128), 1) < HEAD_DIM
    outs = []
    for j in range(2):
        xj = x[:, 128 * j:128 * (j + 1)]
        lo = jnp.sum(jnp.where(m, xj, 0.0), axis=1, keepdims=True)
        hi = jnp.sum(jnp.where(m, 0.0, xj), axis=1, keepdims=True)
        outs.append(jnp.where(m, lo, hi))
    return jnp.concatenate(outs, axis=1)


def _rwkv_kernel(c_ref, mu_ref, w0_ref, w2_ref, a0_ref, a2_ref, g2_ref, kkp_ref, ka_ref, rk_ref, lnw_ref, lnb_ref, e_ref,
                 o_ref, prev_ref, s_ref, r_s, w_s, k_s, v_s, kk_s, b_s, g_s, bon_s, y_s):
    nb = c_ref.shape[0]
    t = RW_TILE

    @pl.when(pl.program_id(0) == 0)
    def _():
        prev_ref[...] = jnp.zeros_like(prev_ref)
        s_ref[...] = jnp.zeros_like(s_ref)
        r_s[...] = jnp.zeros_like(r_s)

    row = lax.broadcasted_iota(I32, (t, 1024), 0)
    for b in range(nb):
        c = c_ref[b]
        pr = jnp.where(row == 0, prev_ref[b, 7:8, :], pltpu.roll(c, 1, axis=0))
        prev_ref[b] = c[t - 8:]
        c = c + (pr - c) * mu_ref[...]
        r, k, v = c[:, 0:256], c[:, 256:512], c[:, 512:768]
        wd, ad, gd = c[:, 768:832], c[:, 832:896], c[:, 896:1024]
        w_log = -_softplus(-(w0_ref[...] + _dot(jnp.tanh(wd), w2_ref[...]))) - 0.5
        a = _sigmoid(a0_ref[...] + _dot(ad, a2_ref[...]))
        kk = k * kkp_ref[...]
        kk = kk / jnp.maximum(jnp.sqrt(_segsum_bc(kk * kk)), 1e-12)
        k2 = k * (1.0 + (a - 1.0) * ka_ref[...])
        r_s[b, 8:, :] = r
        w_s[b] = jnp.exp(-jnp.exp(w_log))
        k_s[b] = k2
        v_s[b] = v
        kk_s[b] = kk
        b_s[b] = kk * a
        g_s[b] = _dot(_sigmoid(gd), g2_ref[...])
        bon_s[b] = _segsum_bc(r * k2 * rk_ref[...]) * v

    gb = RW_GROUP if nb % RW_GROUP == 0 else nb
    nr = gb * HEAD_DIM
    eye4 = jnp.where(lax.broadcasted_iota(I32, (nr, 256), 0) % HEAD_DIM
                     == lax.broadcasted_iota(I32, (nr, 256), 1) % HEAD_DIM, 1.0, 0.0)

    def seg(*xs, passes):
        parts = []
        for x, n in zip(xs, passes):
            parts.extend(_split2(x) if n == 2 else (x.astype(BF16),))
        out = jnp.dot(jnp.concatenate(parts, axis=0), e_ref[...], preferred_element_type=F32)
        res, r0 = [], 0
        for n in passes:
            res.append(out[r0:r0 + nr] + out[r0 + nr:r0 + 2 * nr] if n == 2 else out[r0:r0 + nr])
            r0 += n * nr
        return res

    def rows(ref, b0, i):
        return jnp.concatenate([jnp.broadcast_to(ref[b0 + b, pl.ds(i, 1), :], (HEAD_DIM, 256)) for b in range(gb)],
                               axis=0)

    def put_y(ybc, b0, i):
        yd = ybc * eye4
        for b in range(gb):
            y_s[b0 + b, pl.ds(i, 1), :] = jnp.sum(yd[b * HEAD_DIM:(b + 1) * HEAD_DIM], axis=0, keepdims=True)

    def step(tt, carry):
        for b0 in range(0, nb, gb):
            sl = slice(b0 * HEAD_DIM, (b0 + gb) * HEAD_DIM)
            s = s_ref[sl, :]
            vbc = _segsum_bc(eye4 * rows(v_s, b0, tt))
            sa, ybc = seg(s * rows(kk_s, b0, tt), s * rows(r_s, b0, tt + 7), passes=(2, RW_READOUT_PASSES))
            put_y(ybc, b0, tt + 7)
            s_ref[sl, :] = s * rows(w_s, b0, tt) - sa * rows(b_s, b0, tt) + vbc * rows(k_s, b0, tt)
        return carry

    lax.fori_loop(0, t, step, 0)
    for b0 in range(0, nb, gb):
        ybc, = seg(s_ref[b0 * HEAD_DIM:(b0 + gb) * HEAD_DIM, :] * rows(r_s, b0, t + 7), passes=(RW_READOUT_PASSES,))
        put_y(ybc, b0, t + 7)

    for b in range(nb):
        y = y_s[b, 8:, :]
        d = y - _segsum_bc(y) * (1.0 / HEAD_DIM)
        var = _segsum_bc(d * d) * (1.0 / HEAD_DIM)
        y = d * lax.rsqrt(var + RWKV_GN_EPS) * lnw_ref[...] + lnb_ref[...] + bon_s[b]
        o_ref[b] = y * g_s[b]


def _rwkv_mixer(cols, mu, w0, w2, a0, a2, g2, k_k, k_a, r_k, ln_w, ln_b):
    bsz, s, _ = cols.shape
    t = RW_TILE
    full = lambda shp: pl.BlockSpec(shp, lambda c: (0,) * len(shp))
    r1 = lambda v: v.reshape(1, -1)
    tile = lambda: pltpu.VMEM((bsz, t, 256), F32)
    return pl.pallas_call(
        _rwkv_kernel,
        grid=(s // t,),
        in_specs=[pl.BlockSpec((bsz, t, 1024), lambda c: (0, c, COL["rw_r"] // 1024)),
                  full((1, 1024)), full((1, 256)), full((64, 256)), full((1, 256)), full((64, 256)), full((128, 256)),
                  full((1, 256)), full((1, 256)), full((1, 256)), full((1, 256)), full((1, 256)), full((256, 256))],
        out_specs=pl.BlockSpec((bsz, t, 256), lambda c: (0, c, 0)),
        out_shape=jax.ShapeDtypeStruct((bsz, s, 256), F32),
        scratch_shapes=([pltpu.VMEM((bsz, 8, 1024), F32), pltpu.VMEM((bsz * HEAD_DIM, 256), F32),
                         pltpu.VMEM((bsz, t + 8, 256), F32)] + [tile() for _ in range(7)]
                        + [pltpu.VMEM((bsz, t + 8, 256), F32)]),
        compiler_params=_cparams(("arbitrary",)),
    )(cols, r1(mu), r1(w0), w2, r1(a0), a2, g2, r1(k_k), r1(k_a), r1(r_k), r1(ln_w), r1(ln_b),
      jnp.asarray(np.kron(np.eye(N_HEADS), np.ones((HEAD_DIM, HEAD_DIM))), BF16))


NSA_STRIDE = 16
NSA_SEL_BLOCK = 64
NSA_TOPN = 16
NSA_WINDOW = 512
NSA_FORCE = 1e4
NSA_TQ = 256
NSA_TK = 1024
NSA_TK_WIN = 256
NSA_SLOPES = tuple(2.0 ** (-8.0 * (h + 1) / N_HEADS) for h in range(N_HEADS))
SEL_OFF = -float(2 ** 30)


def _dot3_nt(a, b):
    ah, al = _split2(a)
    bh, bl = _split2(b)
    d = lambda x, y: lax.dot_general(x, y, (((1,), (1,)), ((), ())), preferred_element_type=F32)
    return d(ah, bh) + d(ah, bl) + d(al, bh)


def _nsa_compress_kernel(kv_ref, pos_ref, w1_ref, w2_ref, o_ref):
    nbp = o_ref.shape[0]
    outs = []
    for c in range(2):
        acc_a = jnp.zeros((nbp, 256), F32)
        acc_b = jnp.zeros((nbp, 256), F32)
        for r in range(NSA_STRIDE):
            xr = kv_ref[pl.ds(r, nbp, stride=NSA_STRIDE), :][:, HEAD_DIM * c:HEAD_DIM * (c + 1)]
            acc_a = acc_a + _dot(xr + pos_ref[c, r:r + 1, :], w1_ref[c, 64 * r:64 * (r + 1), :])
            acc_b = acc_b + _dot(xr + pos_ref[c, 16 + r:17 + r, :], w1_ref[c, 1024 + 64 * r:1024 + 64 * (r + 1), :])
        hdn = _gelu_tanh(acc_a + pltpu.roll(acc_b, nbp - 1, axis=0))
        outs.append(_dot(hdn, w2_ref[c]))
    o_ref[...] = jnp.concatenate(outs, axis=1)


def _nsa_cmp_kernel(q_ref, kv_ref, ov_ref, ocmp_ref, qaug_ref, used_ref):
    tq = NSA_TQ
    nbp = kv_ref.shape[0]
    q0 = pl.program_id(1) * tq
    kv = kv_ref[...]
    kc, vc = kv[:, :HEAD_DIM], kv[:, HEAD_DIM:]
    pos = q0 + lax.broadcasted_iota(I32, (tq, nbp), 0)
    blk_end = lax.broadcasted_iota(I32, (tq, nbp), 1) * NSA_STRIDE + (2 * NSA_STRIDE - 1)
    dist = pos - blk_end
    mask = dist >= 0
    distf = dist.astype(F32)
    q = q_ref[...] * (HEAD_DIM ** -0.5)
    lane = lax.broadcasted_iota(I32, (tq, 128), 1)
    psum = jnp.zeros((tq, nbp), F32)
    for h in range(N_HEADS):
        qh = q[:, HEAD_DIM * h:HEAD_DIM * (h + 1)]
        s = jnp.where(mask, _dot3_nt(qh, kc) - NSA_SLOPES[h] * distf, -jnp.inf)
        m = jnp.max(s, axis=1, keepdims=True)
        m = jnp.where(m == -jnp.inf, 0.0, m)
        e = jnp.where(mask, jnp.exp(s - m), 0.0)
        p = e / jnp.maximum(jnp.sum(e, axis=1, keepdims=True), 1e-30)
        ocmp_ref[h] = _dot(p, vc)
        psum = psum + p
        qt = q[:, 128 * (h // 2):128 * (h // 2 + 1)]
        keep = (lane < HEAD_DIM) if h % 2 == 0 else (lane >= HEAD_DIM)
        qaug_ref[h, :, 0:128] = jnp.where(keep, qt, 0.0).astype(BF16)
    imp_t = _dot_exact_rhs(psum, ov_ref[...]).T
    j = lax.broadcasted_iota(I32, (128, tq), 0)
    pos_t = q0 + lax.broadcasted_iota(I32, (128, tq), 1)
    cur = pos_t // NSA_SEL_BLOCK
    forced = (j == 0) | (j == cur) | (j == cur - 1)
    valid = j * NSA_SEL_BLOCK <= pos_t
    x = jnp.where(valid, jnp.where(forced, NSA_FORCE, imp_t), -jnp.inf)
    jf = j.astype(F32)
    picked = jnp.zeros((128, tq), F32)
    for _ in range(NSA_TOPN):
        mx = jnp.max(x, axis=0, keepdims=True)
        first = jnp.min(jnp.where(x == mx, jf, 128.0), axis=0, keepdims=True)
        hit = jf == first
        picked = jnp.where(hit, 1.0, picked)
        x = jnp.where(hit, -jnp.inf, x)
    used = jnp.where(valid & (picked > 0.0), 1.0, 0.0)
    bias = jnp.where(used > 0.0, 0.0, SEL_OFF).T
    for h in range(N_HEADS):
        qaug_ref[h, :, 128:256] = bias.astype(BF16)
    used_ref[...] = jnp.broadcast_to(jnp.max(used.T, axis=0, keepdims=True), (8, 128))


def _nsa_flash_kernel(need_ref, qaug_ref, kv_ref, o_ref, kaug_s, v_s, *, window):
    tq = NSA_TQ
    s_len = kv_ref.shape[0]
    tk = min(NSA_TK_WIN if window else NSA_TK, s_len)
    kdim = 128 if window else 256
    qi = pl.program_id(1)
    q0 = qi * tq

    @pl.when(qi == 0)
    def _():
        def build(i, c):
            r0 = pl.multiple_of(i * tk, tk)
            kv = kv_ref[pl.ds(r0, tk), :]
            k = kv[:, :HEAD_DIM]
            parts = [k, k]
            if not window:
                blk = (r0 + lax.broadcasted_iota(I32, (tk, 128), 0)) // NSA_SEL_BLOCK
                parts.append(jnp.where(blk == lax.broadcasted_iota(I32, (tk, 128), 1), 1.0, 0.0))
            kaug_s[pl.ds(r0, tk), :] = jnp.concatenate(parts, axis=1).astype(BF16)
            v_s[pl.ds(r0, tk), :] = kv[:, HEAD_DIM:].astype(BF16)
            return c
        lax.fori_loop(0, s_len // tk, build, 0)

    q = qaug_ref[...][:, :, :kdim].reshape(N_HEADS * tq, kdim)
    row = lax.broadcasted_iota(I32, (N_HEADS * tq, tk), 0)
    t_q = q0 + (row & (tq - 1))
    head = lax.broadcasted_iota(I32, (N_HEADS * tq, 1), 0) // tq
    slope = jnp.where(head == 0, NSA_SLOPES[0], jnp.where(head == 1, NSA_SLOPES[1],
                      jnp.where(head == 2, NSA_SLOPES[2], NSA_SLOPES[3])))
    col = lax.broadcasted_iota(I32, (N_HEADS * tq, tk), 1)

    n_kt = s_len // tk
    need_base = (pl.program_id(0) * pl.num_programs(1) + qi) * n_kt

    def body(kt, carry):
        if window:
            return tile(kt, carry)
        return lax.cond(need_ref[need_base + kt] != 0, lambda c: tile(kt, c), lambda c: c, carry)

    def tile(kt, carry):
        m, l, acc = carry
        k0 = pl.multiple_of(kt * tk, tk)
        s = lax.dot_general(q, kaug_s[pl.ds(k0, tk), :], (((1,), (1,)), ((), ())), preferred_element_type=F32)
        t_k = k0 + col
        s = s + slope * (t_k - q0).astype(F32)
        ok = t_k <= t_q
        if window:
            ok = ok & (t_q - t_k < NSA_WINDOW)
        m_new = jnp.maximum(m, jnp.max(jnp.where(ok, s, NEG_BIG), axis=1, keepdims=True))
        p = jnp.where(ok, jnp.exp(s - m_new), 0.0)
        alpha = jnp.exp(m - m_new)
        l = alpha * l + jnp.sum(p, axis=1, keepdims=True)
        acc = alpha * acc + jnp.dot(p.astype(BF16), v_s[pl.ds(k0, tk), :], preferred_element_type=F32)
        return m_new, l, acc

    lo = jnp.maximum(q0 - NSA_WINDOW, 0) // tk if window else 0
    hi = (q0 + tq - 1) // tk + 1
    init = (jnp.full((N_HEADS * tq, 1), NEG_BIG, F32), jnp.zeros((N_HEADS * tq, 1), F32),
            jnp.zeros((N_HEADS * tq, HEAD_DIM), F32))
    m, l, acc = lax.fori_loop(lo, hi, body, init)
    o_ref[...] = (acc / l).reshape(N_HEADS, tq, HEAD_DIM)


def _nsa_overlap(s):
    nbp, nsel = s // NSA_STRIDE, s // NSA_SEL_BLOCK
    cs = np.arange(nbp) * NSA_STRIDE
    ss = np.arange(128) * NSA_SEL_BLOCK
    ov = np.minimum(cs[:, None] + 2 * NSA_STRIDE, ss[None, :] + NSA_SEL_BLOCK) - np.maximum(cs[:, None], ss[None, :])
    ov = np.clip(ov, 0, None) / (2 * NSA_STRIDE)
    ov[nbp - 1:, :] = 0.0
    ov[:, nsel:] = 0.0
    return jnp.asarray(ov, BF16)


def _nsa_branches(cols, cmp_pos, cmp_w1, cmp_w2):
    bsz, s, _ = cols.shape
    nbp = s // NSA_STRIDE
    tq = NSA_TQ
    kvcmp = pl.pallas_call(
        _nsa_compress_kernel,
        grid=(bsz,),
        in_specs=[pl.BlockSpec((None, s, 128), lambda b: (b, 0, COL["nsa_kvc"] // 128)),
                  pl.BlockSpec((2, 32, 64), lambda b: (0, 0, 0)),
                  pl.BlockSpec((2, 2048, 256), lambda b: (0, 0, 0)),
                  pl.BlockSpec((2, 256, 64), lambda b: (0, 0, 0))],
        out_specs=pl.BlockSpec((None, nbp, 128), lambda b: (b, 0, 0)),
        out_shape=jax.ShapeDtypeStruct((bsz, nbp, 128), F32),
        compiler_params=_cparams(("arbitrary",), 48),
    )(cols, cmp_pos, cmp_w1.astype(BF16), cmp_w2)
    head_out = lambda dt, w: (pl.BlockSpec((None, N_HEADS, tq, w), lambda b, i: (b, 0, i, 0)),
                              jax.ShapeDtypeStruct((bsz, N_HEADS, s, w), dt))
    (oc_spec, oc_shape), (qa_spec, qa_shape) = head_out(F32, HEAD_DIM), head_out(BF16, 256)
    nqt = s // tq
    o_cmp, qaug, used = pl.pallas_call(
        _nsa_cmp_kernel,
        grid=(bsz, nqt),
        in_specs=[pl.BlockSpec((None, tq, 256), lambda b, i: (b, i, COL["nsa_q"] // 256)),
                  pl.BlockSpec((None, nbp, 128), lambda b, i: (b, 0, 0)),
                  pl.BlockSpec((nbp, 128), lambda b, i: (0, 0))],
        out_specs=[oc_spec, qa_spec, pl.BlockSpec((None, None, 8, 128), lambda b, i: (b, i, 0, 0))],
        out_shape=[oc_shape, qa_shape, jax.ShapeDtypeStruct((bsz, nqt, 8, 128), F32)],
        compiler_params=_cparams(("arbitrary", "arbitrary"), 48),
    )(cols, kvcmp, _nsa_overlap(s))
    tk = min(NSA_TK, s)
    bpt = tk // NSA_SEL_BLOCK
    need = (used[:, :, 0, :(s // tk) * bpt].reshape(bsz, nqt, s // tk, bpt).max(axis=-1) > 0).astype(I32).reshape(-1)

    def flash(name, window):
        hmap = lambda b, i, need_ref: (b, 0, i, 0)
        return pl.pallas_call(
            functools.partial(_nsa_flash_kernel, window=window),
            grid_spec=pltpu.PrefetchScalarGridSpec(
                num_scalar_prefetch=1,
                grid=(bsz, nqt),
                in_specs=[pl.BlockSpec((None, N_HEADS, tq, 256), hmap),
                          pl.BlockSpec((None, s, 128), lambda b, i, need_ref: (b, 0, COL[name] // 128))],
                out_specs=pl.BlockSpec((None, N_HEADS, tq, HEAD_DIM), hmap),
                scratch_shapes=[pltpu.VMEM((s, 128 if window else 256), BF16), pltpu.VMEM((s, HEAD_DIM), BF16)]),
            out_shape=oc_shape,
            compiler_params=_cparams(("arbitrary", "arbitrary"), 48),
        )(need, qaug, cols)

    return o_cmp, flash("nsa_kvs", False), flash("nsa_kvw", True)


def _nsa_mixer(cols, cmp_pos, cmp_w1, cmp_w2):
    o_cmp, o_sel, o_win = _nsa_branches(cols, cmp_pos, cmp_w1, cmp_w2)
    bsz, s, _ = cols.shape
    g = jax.nn.sigmoid(cols[..., COL["nsa_gate"]:COL["nsa_gate"] + 12]).reshape(bsz, s, N_HEADS, 3)
    tr = lambda o: jnp.moveaxis(o, 1, 2)
    out = g[..., 0:1] * tr(o_cmp) + g[..., 1:2] * tr(o_sel) + g[..., 2:3] * tr(o_win)
    return out.reshape(bsz, s, 256)


MIX_TILE = 256


def _mix_out_kernel(h_ref, ssm_ref, lru_ref, rw_ref, oc_ref, os_ref, ow_ref, gate_ref, w_ref, o_ref):
    acc = h_ref[...]
    acc = acc + _dot(ssm_ref[...], w_ref[0:256, :])
    acc = acc + _dot(lru_ref[...], w_ref[256:512, :])
    acc = acc + _dot(rw_ref[...], w_ref[512:768, :])
    g = _sigmoid(gate_ref[...])
    for h in range(N_HEADS):
        y = (g[:, 3 * h:3 * h + 1] * oc_ref[h] + g[:, 3 * h + 1:3 * h + 2] * os_ref[h]
             + g[:, 3 * h + 2:3 * h + 3] * ow_ref[h])
        acc = acc + _dot(y, w_ref[768 + HEAD_DIM * h:768 + HEAD_DIM * (h + 1), :])
    o_ref[...] = acc


def _mix_out(h, y_ssm, y_lru, y_rw, o_cmp, o_sel, o_win, cols, w_out):
    bsz, s, d = h.shape
    t = MIX_TILE
    tok = lambda w: pl.BlockSpec((None, t, w), lambda b, i: (b, i, 0))
    hd = pl.BlockSpec((None, N_HEADS, t, HEAD_DIM), lambda b, i: (b, 0, i, 0))
    return pl.pallas_call(
        _mix_out_kernel,
        grid=(bsz, s // t),
        in_specs=[tok(d), tok(256), tok(256), tok(256), hd, hd, hd,
                  pl.BlockSpec((None, t, 128), lambda b, i: (b, i, COL["nsa_gate"] // 128)),
                  pl.BlockSpec((4 * GROUP_WIDTH, d), lambda b, i: (0, 0))],
        out_specs=tok(d),
        out_shape=jax.ShapeDtypeStruct((bsz, s, d), F32),
        compiler_params=_cparams(("arbitrary", "arbitrary")),
    )(h, y_ssm, y_lru, y_rw, o_cmp, o_sel, o_win, cols, w_out.astype(BF16))


PEER_HEADS = 8
PEER_NKEYS = 128
PEER_TOPK = 16
PEER_TT = 256
PEER_TQ = 256
PEER_ACT_UNROLL = 16
PEER_OUT_UNROLL = 128


PEER_ROW = 4


def _store_packed(x, o_ref):
    rows, d = x.shape
    b = pltpu.bitcast(x, I32)
    r = b + 0x7FFF + ((b >> 16) & 1)
    word = (r[:, d // 2:] & jnp.int32(-65536)) | lax.shift_right_logical(r[:, :d // 2], 16)
    for s in range(PEER_ROW):
        o_ref[pl.ds(s, rows, stride=PEER_ROW), :] = word[:, 128 * s:128 * (s + 1)]


def _norm_matmul2_kernel(x_ref, g_ref, w_ref, o_ref, xpk_ref, xn_ref):
    @pl.when(pl.program_id(1) == 0)
    def _():
        x = x_ref[...]
        y = x * lax.rsqrt(jnp.mean(x * x, axis=-1, keepdims=True) + NORM_EPS) * g_ref[...]
        _store_packed(y, xpk_ref)
        xn_ref[...] = y.astype(BF16)

    o_ref[...] = jnp.dot(xn_ref[...], w_ref[...], preferred_element_type=F32)


def _norm_matmul2(x, g, w, tm, tn):
    m, k = x.shape
    n = w.shape[1]
    return pl.pallas_call(
        _norm_matmul2_kernel,
        grid=(m // tm, n // tn),
        in_specs=[pl.BlockSpec((tm, k), lambda i, j: (i, 0)),
                  pl.BlockSpec((1, k), lambda i, j: (0, 0)),
                  pl.BlockSpec((k, tn), lambda i, j: (0, j))],
        out_specs=[pl.BlockSpec((tm, tn), lambda i, j: (i, j)),
                   pl.BlockSpec((tm * PEER_ROW, 128), lambda i, j: (i, 0))],
        out_shape=[jax.ShapeDtypeStruct((m, n), F32), jax.ShapeDtypeStruct((m * PEER_ROW, 128), I32)],
        scratch_shapes=[pltpu.VMEM((tm, k), BF16)],
        compiler_params=_cparams(("arbitrary", "arbitrary")),
    )(x, g, w)


def _top_rows(x, n, payload=None):
    r = x.shape[0]
    rid = lax.broadcasted_iota(I32, x.shape, 0).astype(F32)
    vals, ids, pays = [], [], []
    for _ in range(n):
        mx = jnp.max(x, axis=0, keepdims=True)
        first = jnp.min(jnp.where(x == mx, rid, float(r)), axis=0, keepdims=True)
        hit = rid == first
        vals.append(mx)
        ids.append(first)
        if payload is not None:
            pays.append(jnp.sum(jnp.where(hit, payload, 0.0), axis=0, keepdims=True))
        x = jnp.where(hit, -jnp.inf, x)
    cat = lambda v: jnp.concatenate(v, axis=0)
    return (cat(vals), cat(ids)) + ((cat(pays),) if payload is not None else ())


def _peer_route_kernel(q_ref, keys_ref, eidx_ref, gate_ref):
    k = PEER_TOPK
    tq = q_ref.shape[0]
    gates, eids = [], []
    for h in range(PEER_HEADS):
        s, i = [], []
        for c in range(2):
            qh = q_ref[:, 128 * (2 * h + c):128 * (2 * h + c + 1)]
            sv, si = _top_rows(_dot3_nt(keys_ref[h, c], qh), k)
            s.append(sv)
            i.append(si)
        cand, cidx = [], []
        for a in range(k):
            nb = k // (a + 1)
            cand.append(s[0][a:a + 1, :] + s[1][:nb])
            cidx.append(i[0][a:a + 1, :] * float(PEER_NKEYS) + i[1][:nb])
        n_c = sum(c.shape[0] for c in cand)
        pad = (-n_c) % 8
        cand.append(jnp.full((pad, tq), -jnp.inf, F32))
        cidx.append(jnp.zeros((pad, tq), F32))
        top, _, eidx = _top_rows(jnp.concatenate(cand, axis=0), k, payload=jnp.concatenate(cidx, axis=0))
        e = jnp.exp(top - top[0:1, :])
        gates.append(e / jnp.sum(e, axis=0, keepdims=True))
        eids.append(eidx)
    gate_ref[...] = jnp.concatenate(gates, axis=0).T
    eidx_ref[...] = (jnp.concatenate(eids, axis=0).T * float(PEER_ROW)).astype(I32)


def _peer_route(q, keys):
    t = q.shape[0]
    tq = min(PEER_TQ, t)
    nk = PEER_HEADS * PEER_TOPK
    return pl.pallas_call(
        _peer_route_kernel,
        grid=(t // tq,),
        in_specs=[pl.BlockSpec((tq, q.shape[1]), lambda i: (i, 0)),
                  pl.BlockSpec(keys.shape, lambda i: (0, 0, 0, 0))],
        out_specs=[pl.BlockSpec((tq, nk), lambda i: (i, 0)), pl.BlockSpec((tq, nk), lambda i: (i, 0))],
        out_shape=[jax.ShapeDtypeStruct((t, nk), I32), jax.ShapeDtypeStruct((t, nk), F32)],
        compiler_params=_cparams(("arbitrary",), 48),
    )(q, keys)


PACK_TILE = 512


def _pack_kernel(t_ref, o_ref):
    _store_packed(t_ref[...], o_ref)


def _pack_table(tab):
    e, d = tab.shape
    te = PACK_TILE
    return pl.pallas_call(
        _pack_kernel,
        grid=(e // te,),
        in_specs=[pl.BlockSpec((te, d), lambda i: (i, 0))],
        out_specs=pl.BlockSpec((te * PEER_ROW, 128), lambda i: (i, 0)),
        out_shape=jax.ShapeDtypeStruct((e * PEER_ROW, 128), I32),
        compiler_params=_cparams(("arbitrary",)),
    )(tab)


def _expert_halves(tbl_ref, r):
    word = tbl_ref[pl.ds(pl.multiple_of(r, PEER_ROW), PEER_ROW), :]
    return pltpu.bitcast(word << 16, F32), pltpu.bitcast(word & jnp.int32(-65536), F32)


_BITREV8 = (0, 4, 2, 6, 1, 5, 3, 7)


def _rowsum_pairs(cur):
    sub = lax.broadcasted_iota(I32, (8, 128), 0)
    for h in (2, 1):
        m = (sub & h) == 0
        cur = [jnp.where(m, x, y) + jnp.where(m, pltpu.roll(x, 8 - h, axis=0), pltpu.roll(y, h, axis=0))
               for x, y in zip(cur[0::2], cur[1::2])]
    return cur[0]


def _peer_act_kernel(idx_ref, x_ref, gate_ref, tbl_ref, o_ref, q_ref, s_ref):
    tt, nk = gate_ref.shape
    sub = lax.broadcasted_iota(I32, (8, 128), 0)

    @pl.when(pl.program_id(0) == 0)
    def _():
        q_ref[...] = jnp.zeros_like(q_ref)

    def finish(slot, t):
        s_ref[pl.ds(t, 1), :] = jnp.sum(q_ref[slot].T, axis=0, keepdims=True)

    def tok(t, c):
        finish((t + 1) % 2, jnp.maximum(t - 1, 0))
        xw = x_ref[pl.ds(pl.multiple_of(t * PEER_ROW, PEER_ROW), PEER_ROW), :]
        xb = pltpu.bitcast(jnp.concatenate([xw, xw], axis=0), BF16)
        slot = t % 2
        for g in range(nk // 8):
            fs = []
            for i in range(4):
                ra = pl.multiple_of(idx_ref[t * nk + 8 * g + _BITREV8[2 * i]], PEER_ROW)
                rb = pl.multiple_of(idx_ref[t * nk + 8 * g + _BITREV8[2 * i + 1]], PEER_ROW)
                pa = pltpu.bitcast(pltpu.bitcast(tbl_ref[pl.ds(ra + PEER_ROW, 8), :], BF16) * xb, I32)
                pb = pltpu.bitcast(pltpu.bitcast(tbl_ref[pl.ds(rb, 8), :], BF16) * xb, I32)
                prod = jnp.where(sub < PEER_ROW, pa, pb)
                fs.append(pltpu.bitcast(prod << 16, F32) + pltpu.bitcast(prod & jnp.int32(-65536), F32))
            q_ref[slot, 8 * g:8 * (g + 1), :] = _rowsum_pairs(fs)
        return c

    lax.fori_loop(0, tt, tok, 0)
    finish((tt - 1) % 2, tt - 1)
    o_ref[...] = gate_ref[...] * _gelu_tanh(s_ref[...])


def _peer_out_kernel(idx_ref, w_ref, tbl_ref, h_ref, o_ref):
    tt = h_ref.shape[0]
    nk = idx_ref.shape[0] // tt

    def tok(t, c):
        def grp(g, accs):
            accs = list(accs)
            k0 = t * nk + g * PEER_OUT_UNROLL
            for j in range(PEER_OUT_UNROLL):
                w = w_ref[k0 + j]
                vlo, vhi = _expert_halves(tbl_ref, idx_ref[k0 + j])
                a = 2 * (j % 2)
                accs[a] = accs[a] + vlo * w
                accs[a + 1] = accs[a + 1] + vhi * w
            return tuple(accs)

        z = jnp.zeros((PEER_ROW, 128), F32)
        l0, h0, l1, h1 = lax.fori_loop(0, nk // PEER_OUT_UNROLL, grp, (z, z, z, z))
        o_ref[t] = h_ref[t] + jnp.concatenate([l0 + l1, h0 + h1], axis=0)
        return c

    lax.fori_loop(0, tt, tok, 0)


def _peer_experts(eidx_t, gate_t, xn3, h3, tbl_u, tbl_v):
    t, nk = eidx_t.shape
    tt = min(PEER_TT, t)
    smem = lambda: pl.BlockSpec((tt * nk,), lambda i: (i,), memory_space=pltpu.SMEM)
    vm = lambda: pl.BlockSpec((tt, nk), lambda i: (i, 0))
    tok = lambda: pl.BlockSpec((tt, 8, 128), lambda i: (i, 0, 0))
    tbl = lambda a: pl.BlockSpec(a.shape, lambda i: (0, 0), pipeline_mode=pl.Buffered(1))
    eidx_flat = eidx_t.reshape(t * nk)
    w_t = pl.pallas_call(
        _peer_act_kernel,
        grid=(t // tt,),
        in_specs=[smem(), pl.BlockSpec((tt * PEER_ROW, 128), lambda i: (i, 0)), vm(), tbl(tbl_u)],
        out_specs=vm(),
        out_shape=jax.ShapeDtypeStruct((t, nk), F32),
        scratch_shapes=[pltpu.VMEM((2, nk, 128), F32), pltpu.VMEM((tt, nk), F32)],
        compiler_params=_cparams(("arbitrary",), 48),
    )(eidx_flat, xn3, gate_t, tbl_u)
    return pl.pallas_call(
        _peer_out_kernel,
        grid=(t // tt,),
        in_specs=[smem(), smem(), tbl(tbl_v), tok()],
        out_specs=tok(),
        out_shape=jax.ShapeDtypeStruct(h3.shape, F32),
        compiler_params=_cparams(("arbitrary",), 48),
    )(eidx_flat, w_t.reshape(t * nk), tbl_v, h3)


def _peer_block(h, g, wq, keys, u, v):
    t, d = h.shape
    q, xpk = _norm_matmul2(h, g.reshape(1, d), wq.astype(BF16), min(512, t), 1024)
    eidx_t, gate_t = _peer_route(q, keys)
    tbl_u = jnp.pad(_pack_table(u), ((PEER_ROW, PEER_ROW), (0, 0)))
    out = _peer_experts(eidx_t, gate_t, xpk, h.reshape(t, 8, 128), tbl_u, _pack_table(v))
    return out.reshape(t, d)


def _rmsnorm_kernel(x_ref, g_ref, o_ref):
    x = x_ref[...]
    o_ref[...] = x * lax.rsqrt(jnp.mean(x * x, axis=-1, keepdims=True) + NORM_EPS) * g_ref[...]


def _rmsnorm(x, g, tm=512):
    m, d = x.shape
    return pl.pallas_call(
        _rmsnorm_kernel,
        grid=(m // tm,),
        in_specs=[pl.BlockSpec((tm, d), lambda i: (i, 0)), pl.BlockSpec((1, d), lambda i: (0, 0))],
        out_specs=pl.BlockSpec((tm, d), lambda i: (i, 0)),
        out_shape=jax.ShapeDtypeStruct((m, d), F32),
        compiler_params=_cparams(("arbitrary",)),
    )(x, g.reshape(1, d))


def kernel(x, mix_norm, w_in, w_out, ssm_conv_w, ssm_conv_b, ssm_dt_bias, ssm_a_log, ssm_d, ssm_norm, lru_conv_w, lru_conv_b, lru_wa, lru_ba, lru_wi, lru_bi, lru_lambda, rwkv_mu, rwkv_w0, rwkv_w2, rwkv_a0, rwkv_a2, rwkv_g2, rwkv_kk, rwkv_ka, rwkv_rk, rwkv_ln_w, rwkv_ln_b, nsa_cmp_pos, nsa_cmp_w1, nsa_cmp_w2, ffn_norm, peer_wq, peer_keys, peer_u, peer_v, final_norm):
    bsz, s, d = x.shape
    t = bsz * s
    h = x
    for l in range(w_in.shape[0]):
        w_l = _arrange_cols(w_in[l]).astype(BF16)
        cols = _norm_matmul(h.reshape(t, d), mix_norm[l].reshape(1, d), w_l, 512, 640).reshape(bsz, s, NCOL)
        y_ssm = _ssd_mixer(cols, ssm_conv_w[l], ssm_conv_b[l], ssm_dt_bias[l], ssm_a_log[l], ssm_d[l], ssm_norm[l])
        y_lru = _lru_mixer(cols, lru_conv_w[l], lru_conv_b[l], lru_wa[l], lru_ba[l], lru_wi[l], lru_bi[l], lru_lambda[l])
        y_rw = _rwkv_mixer(cols, rwkv_mu[l], rwkv_w0[l], rwkv_w2[l], rwkv_a0[l], rwkv_a2[l], rwkv_g2[l],
                           rwkv_kk[l], rwkv_ka[l], rwkv_rk[l], rwkv_ln_w[l], rwkv_ln_b[l])
        o_cmp, o_sel, o_win = _nsa_branches(cols, nsa_cmp_pos[l], nsa_cmp_w1[l], nsa_cmp_w2[l])
        h = _mix_out(h, y_ssm, y_lru, y_rw, o_cmp, o_sel, o_win, cols, w_out[l])
        h = _peer_block(h.reshape(t, d), ffn_norm[l], peer_wq[l], peer_keys[l], peer_u[l], peer_v[l]).reshape(bsz, s, d)
    return _rmsnorm(h.reshape(t, d), final_norm).reshape(bsz, s, d)
```

```python
import functools
import math

import numpy as np
import jax
import jax.numpy as jnp
from jax import lax
from jax.experimental import pallas as pl
from jax.experimental.pallas import tpu as pltpu

F32 = jnp.float32
BF16 = jnp.bfloat16
I32 = jnp.int32

NORM_EPS = 1e-6
GROUP_WIDTH = 256
HEAD_DIM = 64
N_HEADS = 4
LRU_C = 8.0
NEG_BIG = -1e30


COL = dict(ssd_z=0, ssd_x=256, lru_gate=512, lru_x=768, rw_r=1024, rw_k=1280, rw_v=1536, rw_wa=1792, rw_g=1920,
           nsa_q=2048, nsa_kvc=2304, nsa_kvs=2432, nsa_kvw=2560, nsa_gate=2688, ssd_b=2816, ssd_c=2944, ssd_dt=3072)
NCOL = 3200


def _col_perm():
    src = np.full((NCOL,), -1, np.int64)

    def put(dst, s0, n):
        src[dst:dst + n] = np.arange(s0, s0 + n)

    put(COL["ssd_z"], 0, 256)
    put(COL["ssd_x"], 256, 256)
    put(COL["ssd_b"], 512, 128)
    put(COL["ssd_c"], 640, 128)
    put(COL["ssd_dt"], 768, 4)
    put(COL["lru_gate"], 772, 256)
    put(COL["lru_x"], 1028, 256)
    put(COL["rw_r"], 1284, 1024)
    put(COL["nsa_q"], 2308, 256)
    put(COL["nsa_kvc"], 2564, 128)
    put(COL["nsa_kvs"], 2692, 128)
    put(COL["nsa_kvw"], 2820, 128)
    put(COL["nsa_gate"], 2948, 12)
    return src


def _arrange_cols(a):
    src = _col_perm()
    out = jnp.take(a, jnp.asarray(np.maximum(src, 0)), axis=-1)
    return jnp.where(jnp.asarray(src >= 0), out, jnp.zeros((), a.dtype))


def _cparams(sem, vmem_mb=None):
    kw = dict(dimension_semantics=sem)
    if vmem_mb is not None:
        kw["vmem_limit_bytes"] = vmem_mb * 1024 * 1024
    return pltpu.CompilerParams(**kw)


def _dot(a, b):
    return jnp.dot(a.astype(BF16), b.astype(BF16), preferred_element_type=F32)


def _dot_nt(a, b):
    return lax.dot_general(a.astype(BF16), b.astype(BF16), (((1,), (1,)), ((), ())), preferred_element_type=F32)


def _dot_tn(a, b):
    return lax.dot_general(a.astype(BF16), b.astype(BF16), (((0,), (0,)), ((), ())), preferred_element_type=F32)


def _split3(x):
    h = x.astype(BF16)
    r = x - h.astype(F32)
    m = r.astype(BF16)
    l = (r - m.astype(F32)).astype(BF16)
    return h, m, l


def _split2(x):
    h = x.astype(BF16)
    return h, (x - h.astype(F32)).astype(BF16)


def _dot_exact_rhs(a, b_exact):
    h, m, l = _split3(a)
    d = lambda t: jnp.dot(t, b_exact, preferred_element_type=F32)
    return d(h) + d(m) + d(l)


def _dot_exact_lhs(a_exact, b):
    h, m, l = _split3(b)
    d = lambda t: jnp.dot(a_exact, t, preferred_element_type=F32)
    return d(h) + d(m) + d(l)


def _sigmoid(x):
    return 1.0 / (1.0 + jnp.exp(-x))


def _silu(x):
    return x * _sigmoid(x)


def _gelu_tanh(x):
    return 0.5 * x * (1.0 + jnp.tanh(math.sqrt(2.0 / math.pi) * (x + 0.044715 * (x * x * x))))


def _softplus(x):
    return jnp.maximum(x, 0.0) + jnp.log1p(jnp.exp(-jnp.abs(x)))


def _conv4(cur, tail, w, b):
    c = cur.shape[1]
    row8 = lax.broadcasted_iota(I32, (8, c), 0)
    acc = cur * w[3:4, :] + b
    for k in (1, 2, 3):
        cr = pltpu.roll(cur, k, axis=0)
        tr = pltpu.roll(tail, k, axis=0)
        head = jnp.where(row8 < k, tr, cr[:8])
        sh = jnp.concatenate([head, cr[8:]], axis=0)
        acc = acc + sh * w[3 - k:4 - k, :]
    return acc


def _norm_matmul_kernel(x_ref, g_ref, w_ref, o_ref, xn_ref):
    @pl.when(pl.program_id(1) == 0)
    def _():
        x = x_ref[...]
        y = x * lax.rsqrt(jnp.mean(x * x, axis=-1, keepdims=True) + NORM_EPS) * g_ref[...]
        xn_ref[...] = y.astype(BF16)

    o_ref[...] = jnp.dot(xn_ref[...], w_ref[...], preferred_element_type=F32)


def _norm_matmul(x, g, w, tm, tn):
    m, k = x.shape
    n = w.shape[1]
    return pl.pallas_call(
        _norm_matmul_kernel,
        grid=(m // tm, n // tn),
        in_specs=[pl.BlockSpec((tm, k), lambda i, j: (i, 0)),
                  pl.BlockSpec((1, k), lambda i, j: (0, 0)),
                  pl.BlockSpec((k, tn), lambda i, j: (0, j))],
        out_specs=pl.BlockSpec((tm, tn), lambda i, j: (i, j)),
        out_shape=jax.ShapeDtypeStruct((m, n), F32),
        scratch_shapes=[pltpu.VMEM((tm, k), BF16)],
        compiler_params=_cparams(("arbitrary", "arbitrary")),
    )(x, g, w)


SSD_CHUNK = 128


def _ssd_kernel(z_ref, x_ref, b_ref, c_ref, dt_ref, cwx_ref, cbx_ref, cwb_ref, cbb_ref, cwc_ref, cbc_ref,
                dtb_ref, alog_ref, d_ref, nw_ref, o_ref, tx_ref, tb_ref, tc_ref, st_ref):
    t = SSD_CHUNK

    @pl.when(pl.program_id(1) == 0)
    def _():
        tx_ref[...] = jnp.zeros_like(tx_ref)
        tb_ref[...] = jnp.zeros_like(tb_ref)
        tc_ref[...] = jnp.zeros_like(tc_ref)
        st_ref[...] = jnp.zeros_like(st_ref)

    xr, br, cr = x_ref[...], b_ref[...], c_ref[...]
    xs = _silu(_conv4(xr, tx_ref[...], cwx_ref[...], cbx_ref[...]))
    bm = _silu(_conv4(br, tb_ref[...], cwb_ref[...], cbb_ref[...]))
    cm = _silu(_conv4(cr, tc_ref[...], cwc_ref[...], cbc_ref[...]))
    tx_ref[...] = xr[t - 8:]
    tb_ref[...] = br[t - 8:]
    tc_ref[...] = cr[t - 8:]

    dt = _softplus(dt_ref[...] + dtb_ref[...])
    a = dt * (-jnp.exp(alog_ref[...]))
    ri = lax.broadcasted_iota(I32, (t, t), 0)
    ci = lax.broadcasted_iota(I32, (t, t), 1)
    tril = ri >= ci
    acs = _dot_exact_lhs(jnp.where(tril, 1.0, 0.0).astype(BF16), a)
    acs_t = acs.T
    d_row = d_ref[...]
    ys = []
    for h in range(N_HEADS):
        g = h // 2
        lo, hi = HEAD_DIM * h, HEAD_DIM * (h + 1)
        b_g = bm[:, HEAD_DIM * g:HEAD_DIM * (g + 1)]
        c_g = cm[:, HEAD_DIM * g:HEAD_DIM * (g + 1)]
        x_h = xs[:, lo:hi]
        acs_c = acs[:, h:h + 1]
        lmat = jnp.exp(jnp.where(tril, acs_c - acs_t[h:h + 1, :], NEG_BIG))
        scores = _dot_nt(c_g, b_g) * lmat
        xdt = x_h * dt[:, h:h + 1]
        st = st_ref[h]
        y = _dot(scores, xdt) + _dot(c_g, st) * jnp.exp(acs_c) + x_h * d_row[:, lo:hi]
        a_last = acs[t - 1:t, h:h + 1]
        bw = b_g * jnp.exp(a_last - acs_c)
        st_ref[h] = st * jnp.exp(a_last) + _dot_tn(bw, xdt)
        ys.append(y)
    y = jnp.concatenate(ys, axis=1) * _silu(z_ref[...])
    outs = []
    for g in range(2):
        yg = y[:, 128 * g:128 * (g + 1)]
        outs.append(yg * lax.rsqrt(jnp.mean(yg * yg, axis=-1, keepdims=True) + NORM_EPS))
    o_ref[...] = jnp.concatenate(outs, axis=1) * nw_ref[...]


def _ssd_mixer(cols, conv_w, conv_b, dt_bias, a_log, d_skip, norm_w):
    bsz, s, _ = cols.shape
    t = SSD_CHUNK
    row = lambda v, n: jnp.pad(v.reshape(1, -1), ((0, 0), (0, n - v.size)))
    full = lambda shp: pl.BlockSpec(shp, lambda b, c: (0,) * len(shp))
    colspec = lambda w, name: pl.BlockSpec((None, t, w), lambda b, c: (b, c, COL[name] // w))
    return pl.pallas_call(
        _ssd_kernel,
        grid=(bsz, s // t),
        in_specs=[colspec(256, "ssd_z"), colspec(256, "ssd_x"), colspec(128, "ssd_b"), colspec(128, "ssd_c"),
                  colspec(128, "ssd_dt"),
                  full((4, 256)), full((1, 256)), full((4, 128)), full((1, 128)), full((4, 128)), full((1, 128)),
                  full((1, 128)), full((1, 128)), full((1, 256)), full((1, 256))],
        out_specs=pl.BlockSpec((None, t, 256), lambda b, c: (b, c, 0)),
        out_shape=jax.ShapeDtypeStruct((bsz, s, 256), F32),
        scratch_shapes=[pltpu.VMEM((8, 256), F32), pltpu.VMEM((8, 128), F32), pltpu.VMEM((8, 128), F32),
                        pltpu.VMEM((N_HEADS, HEAD_DIM, HEAD_DIM), F32)],
        compiler_params=_cparams(("arbitrary", "arbitrary")),
    )(cols, cols, cols, cols, cols,
      conv_w[:, :256], conv_b[:256].reshape(1, 256), conv_w[:, 256:384], conv_b[256:384].reshape(1, 128),
      conv_w[:, 384:512], conv_b[384:512].reshape(1, 128),
      row(dt_bias, 128), row(a_log, 128), jnp.repeat(d_skip, HEAD_DIM).reshape(1, 256), norm_w.reshape(1, 256))


LRU_TILE = 256


def _lru_kernel(gate_ref, xr_ref, cw_ref, cb_ref, wa_ref, ba_ref, wi_ref, bi_ref, lam_ref, o_ref, tail_ref, h_ref):
    t = LRU_TILE

    @pl.when(pl.program_id(1) == 0)
    def _():
        tail_ref[...] = jnp.zeros_like(tail_ref)
        h_ref[...] = jnp.zeros_like(h_ref)

    xr = xr_ref[...]
    xc = _conv4(xr, tail_ref[...], cw_ref[...], cb_ref[...])
    tail_ref[...] = xr[t - 8:]
    r = _sigmoid(_dot(xc, wa_ref[...]) + ba_ref[...])
    i = _sigmoid(_dot(xc, wi_ref[...]) + bi_ref[...])
    log_a = LRU_C * r * (-_softplus(-lam_ref[...]))
    a = jnp.exp(log_a)
    u = jnp.sqrt(-jnp.tanh(log_a) * (a * a + 1.0)) * (i * xc)
    rows = lax.broadcasted_iota(I32, (t, GROUP_WIDTH), 0)
    d = 1
    while d < t:
        a_sh = pltpu.roll(a, d, axis=0)
        u_sh = pltpu.roll(u, d, axis=0)
        m = rows >= d
        u = jnp.where(m, a * u_sh + u, u)
        a = jnp.where(m, a * a_sh, a)
        d *= 2
    h = a * h_ref[7:8, :] + u
    h_ref[...] = h[t - 8:]
    o_ref[...] = h * _gelu_tanh(gate_ref[...])


def _lru_mixer(cols, conv_w, conv_b, wa, ba, wi, bi, lam):
    bsz, s, _ = cols.shape
    t = LRU_TILE
    cb = COL["lru_gate"] // 256
    bd = lambda w: jax.scipy.linalg.block_diag(*[w[k] for k in range(w.shape[0])]).astype(BF16)
    full = lambda shp: pl.BlockSpec(shp, lambda b, c: (0,) * len(shp))
    return pl.pallas_call(
        _lru_kernel,
        grid=(bsz, s // t),
        in_specs=[pl.BlockSpec((None, t, 256), lambda b, c: (b, c, cb)),
                  pl.BlockSpec((None, t, 256), lambda b, c: (b, c, cb + 1)),
                  full((4, 256)), full((1, 256)), full((256, 256)), full((1, 256)), full((256, 256)), full((1, 256)),
                  full((1, 256))],
        out_specs=pl.BlockSpec((None, t, 256), lambda b, c: (b, c, 0)),
        out_shape=jax.ShapeDtypeStruct((bsz, s, 256), F32),
        scratch_shapes=[pltpu.VMEM((8, 256), F32), pltpu.VMEM((8, 256), F32)],
        compiler_params=_cparams(("arbitrary", "arbitrary")),
    )(cols, cols, conv_w, conv_b.reshape(1, 256), bd(wa), ba.reshape(1, 256), bd(wi), bi.reshape(1, 256),
      lam.reshape(1, 256))


RW_TILE = 128
RW_GROUP = 4
RW_READOUT_PASSES = 1
RWKV_GN_EPS = 64e-5


def _segsum_bc(x):
    r = x.shape[0]
    m = lax.broadcasted_iota(I32, (r, 128), 1) < HEAD_DIM
    outs = []
    for j in range(2):
        xj = x[:, 128 * j:128 * (j + 1)]
        lo = jnp.sum(jnp.where(m, xj, 0.0), axis=1, keepdims=True)
        hi = jnp.sum(jnp.where(m, 0.0, xj), axis=1, keepdims=True)
        outs.append(jnp.where(m, lo, hi))
    return jnp.concatenate(outs, axis=1)


def _rwkv_kernel(c_ref, mu_ref, w0_ref, w2_ref, a0_ref, a2_ref, g2_ref, kkp_ref, ka_ref, rk_ref, lnw_ref, lnb_ref, e_ref,
                 o_ref, prev_ref, s_ref, r_s, w_s, k_s, v_s, kk_s, b_s, g_s, bon_s, y_s):
    nb = c_ref.shape[0]
    t = RW_TILE

    @pl.when(pl.program_id(0) == 0)
    def _():
        prev_ref[...] = jnp.zeros_like(prev_ref)
        s_ref[...] = jnp.zeros_like(s_ref)
        r_s[...] = jnp.zeros_like(r_s)

    row = lax.broadcasted_iota(I32, (t, 1024), 0)
    for b in range(nb):
        c = c_ref[b]
        pr = jnp.where(row == 0, prev_ref[b, 7:8, :], pltpu.roll(c, 1, axis=0))
        prev_ref[b] = c[t - 8:]
        c = c + (pr - c) * mu_ref[...]
        r, k, v = c[:, 0:256], c[:, 256:512], c[:, 512:768]
        wd, ad, gd = c[:, 768:832], c[:, 832:896], c[:, 896:1024]
        w_log = -_softplus(-(w0_ref[...] + _dot(jnp.tanh(wd), w2_ref[...]))) - 0.5
        a = _sigmoid(a0_ref[...] + _dot(ad, a2_ref[...]))
        kk = k * kkp_ref[...]
        kk = kk / jnp.maximum(jnp.sqrt(_segsum_bc(kk * kk)), 1e-12)
        k2 = k * (1.0 + (a - 1.0) * ka_ref[...])
        r_s[b, 8:, :] = r
        w_s[b] = jnp.exp(-jnp.exp(w_log))
        k_s[b] = k2
        v_s[b] = v
        kk_s[b] = kk
        b_s[b] = kk * a
        g_s[b] = _dot(_sigmoid(gd), g2_ref[...])
        bon_s[b] = _segsum_bc(r * k2 * rk_ref[...]) * v

    gb = RW_GROUP if nb % RW_GROUP == 0 else nb
    nr = gb * HEAD_DIM
    eye4 = jnp.where(lax.broadcasted_iota(I32, (nr, 256), 0) % HEAD_DIM
                     == lax.broadcasted_iota(I32, (nr, 256), 1) % HEAD_DIM, 1.0, 0.0)

    def seg(*xs, passes):
        parts = []
        for x, n in zip(xs, passes):
            parts.extend(_split2(x) if n == 2 else (x.astype(BF16),))
        out = jnp.dot(jnp.concatenate(parts, axis=0), e_ref[...], preferred_element_type=F32)
        res, r0 = [], 0
        for n in passes:
            res.append(out[r0:r0 + nr] + out[r0 + nr:r0 + 2 * nr] if n == 2 else out[r0:r0 + nr])
            r0 += n * nr
        return res

    def rows(ref, b0, i):
        return jnp.concatenate([jnp.broadcast_to(ref[b0 + b, pl.ds(i, 1), :], (HEAD_DIM, 256)) for b in range(gb)],
                               axis=0)

    def put_y(ybc, b0, i):
        yd = ybc * eye4
        for b in range(gb):
            y_s[b0 + b, pl.ds(i, 1), :] = jnp.sum(yd[b * HEAD_DIM:(b + 1) * HEAD_DIM], axis=0, keepdims=True)

    def step(tt, carry):
        for b0 in range(0, nb, gb):
            sl = slice(b0 * HEAD_DIM, (b0 + gb) * HEAD_DIM)
            s = s_ref[sl, :]
            vbc = _segsum_bc(eye4 * rows(v_s, b0, tt))
            sa, ybc = seg(s * rows(kk_s, b0, tt), s * rows(r_s, b0, tt + 7), passes=(2, RW_READOUT_PASSES))
            put_y(ybc, b0, tt + 7)
            s_ref[sl, :] = s * rows(w_s, b0, tt) - sa * rows(b_s, b0, tt) + vbc * rows(k_s, b0, tt)
        return carry

    lax.fori_loop(0, t, step, 0)
    for b0 in range(0, nb, gb):
        ybc, = seg(s_ref[b0 * HEAD_DIM:(b0 + gb) * HEAD_DIM, :] * rows(r_s, b0, t + 7), passes=(RW_READOUT_PASSES,))
        put_y(ybc, b0, t + 7)

    for b in range(nb):
        y = y_s[b, 8:, :]
        d = y - _segsum_bc(y) * (1.0 / HEAD_DIM)
        var = _segsum_bc(d * d) * (1.0 / HEAD_DIM)
        y = d * lax.rsqrt(var + RWKV_GN_EPS) * lnw_ref[...] + lnb_ref[...] + bon_s[b]
        o_ref[b] = y * g_s[b]


def _rwkv_mixer(cols, mu, w0, w2, a0, a2, g2, k_k, k_a, r_k, ln_w, ln_b):
    bsz, s, _ = cols.shape
    t = RW_TILE
    full = lambda shp: pl.BlockSpec(shp, lambda c: (0,) * len(shp))
    r1 = lambda v: v.reshape(1, -1)
    tile = lambda: pltpu.VMEM((bsz, t, 256), F32)
    return pl.pallas_call(
        _rwkv_kernel,
        grid=(s // t,),
        in_specs=[pl.BlockSpec((bsz, t, 1024), lambda c: (0, c, COL["rw_r"] // 1024)),
                  full((1, 1024)), full((1, 256)), full((64, 256)), full((1, 256)), full((64, 256)), full((128, 256)),
                  full((1, 256)), full((1, 256)), full((1, 256)), full((1, 256)), full((1, 256)), full((256, 256))],
        out_specs=pl.BlockSpec((bsz, t, 256), lambda c: (0, c, 0)),
        out_shape=jax.ShapeDtypeStruct((bsz, s, 256), F32),
        scratch_shapes=([pltpu.VMEM((bsz, 8, 1024), F32), pltpu.VMEM((bsz * HEAD_DIM, 256), F32),
                         pltpu.VMEM((bsz, t + 8, 256), F32)] + [tile() for _ in range(7)]
                        + [pltpu.VMEM((bsz, t + 8, 256), F32)]),
        compiler_params=_cparams(("arbitrary",)),
    )(cols, r1(mu), r1(w0), w2, r1(a0), a2, g2, r1(k_k), r1(k_a), r1(r_k), r1(ln_w), r1(ln_b),
      jnp.asarray(np.kron(np.eye(N_HEADS), np.ones((HEAD_DIM, HEAD_DIM))), BF16))


NSA_STRIDE = 16
NSA_SEL_BLOCK = 64
NSA_TOPN = 16
NSA_WINDOW = 512
NSA_FORCE = 1e4
NSA_TQ = 256
NSA_TK = 1024
NSA_TK_WIN = 256
NSA_SLOPES = tuple(2.0 ** (-8.0 * (h + 1) / N_HEADS) for h in range(N_HEADS))
SEL_OFF = -float(2 ** 30)


def _dot3_nt(a, b):
    ah, al = _split2(a)
    bh, bl = _split2(b)
    d = lambda x, y: lax.dot_general(x, y, (((1,), (1,)), ((), ())), preferred_element_type=F32)
    return d(ah, bh) + d(ah, bl) + d(al, bh)


def _nsa_compress_kernel(kv_ref, pos_ref, w1_ref, w2_ref, o_ref):
    nbp = o_ref.shape[0]
    outs = []
    for c in range(2):
        acc_a = jnp.zeros((nbp, 256), F32)
        acc_b = jnp.zeros((nbp, 256), F32)
        for r in range(NSA_STRIDE):
            xr = kv_ref[pl.ds(r, nbp, stride=NSA_STRIDE), :][:, HEAD_DIM * c:HEAD_DIM * (c + 1)]
            acc_a = acc_a + _dot(xr + pos_ref[c, r:r + 1, :], w1_ref[c, 64 * r:64 * (r + 1), :])
            acc_b = acc_b + _dot(xr + pos_ref[c, 16 + r:17 + r, :], w1_ref[c, 1024 + 64 * r:1024 + 64 * (r + 1), :])
        hdn = _gelu_tanh(acc_a + pltpu.roll(acc_b, nbp - 1, axis=0))
        outs.append(_dot(hdn, w2_ref[c]))
    o_ref[...] = jnp.concatenate(outs, axis=1)


def _nsa_cmp_kernel(q_ref, kv_ref, ov_ref, ocmp_ref, qaug_ref, used_ref):
    tq = NSA_TQ
    nbp = kv_ref.shape[0]
    q0 = pl.program_id(1) * tq
    kv = kv_ref[...]
    kc, vc = kv[:, :HEAD_DIM], kv[:, HEAD_DIM:]
    pos = q0 + lax.broadcasted_iota(I32, (tq, nbp), 0)
    blk_end = lax.broadcasted_iota(I32, (tq, nbp), 1) * NSA_STRIDE + (2 * NSA_STRIDE - 1)
    dist = pos - blk_end
    mask = dist >= 0
    distf = dist.astype(F32)
    q = q_ref[...] * (HEAD_DIM ** -0.5)
    lane = lax.broadcasted_iota(I32, (tq, 128), 1)
    psum = jnp.zeros((tq, nbp), F32)
    for h in range(N_HEADS):
        qh = q[:, HEAD_DIM * h:HEAD_DIM * (h + 1)]
        s = jnp.where(mask, _dot3_nt(qh, kc) - NSA_SLOPES[h] * distf, -jnp.inf)
        m = jnp.max(s, axis=1, keepdims=True)
        m = jnp.where(m == -jnp.inf, 0.0, m)
        e = jnp.where(mask, jnp.exp(s - m), 0.0)
        p = e / jnp.maximum(jnp.sum(e, axis=1, keepdims=True), 1e-30)
        ocmp_ref[h] = _dot(p, vc)
        psum = psum + p
        qt = q[:, 128 * (h // 2):128 * (h // 2 + 1)]
        keep = (lane < HEAD_DIM) if h % 2 == 0 else (lane >= HEAD_DIM)
        qaug_ref[h, :, 0:128] = jnp.where(keep, qt, 0.0).astype(BF16)
    imp_t = _dot_exact_rhs(psum, ov_ref[...]).T
    j = lax.broadcasted_iota(I32, (128, tq), 0)
    pos_t = q0 + lax.broadcasted_iota(I32, (128, tq), 1)
    cur = pos_t // NSA_SEL_BLOCK
    forced = (j == 0) | (j == cur) | (j == cur - 1)
    valid = j * NSA_SEL_BLOCK <= pos_t
    x = jnp.where(valid, jnp.where(forced, NSA_FORCE, imp_t), -jnp.inf)
    jf = j.astype(F32)
    picked = jnp.zeros((128, tq), F32)
    for _ in range(NSA_TOPN):
        mx = jnp.max(x, axis=0, keepdims=True)
        first = jnp.min(jnp.where(x == mx, jf, 128.0), axis=0, keepdims=True)
        hit = jf == first
        picked = jnp.where(hit, 1.0, picked)
        x = jnp.where(hit, -jnp.inf, x)
    used = jnp.where(valid & (picked > 0.0), 1.0, 0.0)
    bias = jnp.where(used > 0.0, 0.0, SEL_OFF).T
    for h in range(N_HEADS):
        qaug_ref[h, :, 128:256] = bias.astype(BF16)
    used_ref[...] = jnp.broadcast_to(jnp.max(used.T, axis=0, keepdims=True), (8, 128))


def _nsa_flash_kernel(need_ref, qaug_ref, kv_ref, o_ref, kaug_s, v_s, *, window):
    tq = NSA_TQ
    s_len = kv_ref.shape[0]
    tk = min(NSA_TK_WIN if window else NSA_TK, s_len)
    kdim = 128 if window else 256
    qi = pl.program_id(1)
    q0 = qi * tq

    @pl.when(qi == 0)
    def _():
        def build(i, c):
            r0 = pl.multiple_of(i * tk, tk)
            kv = kv_ref[pl.ds(r0, tk), :]
            k = kv[:, :HEAD_DIM]
            parts = [k, k]
            if not window:
                blk = (r0 + lax.broadcasted_iota(I32, (tk, 128), 0)) // NSA_SEL_BLOCK
                parts.append(jnp.where(blk == lax.broadcasted_iota(I32, (tk, 128), 1), 1.0, 0.0))
            kaug_s[pl.ds(r0, tk), :] = jnp.concatenate(parts, axis=1).astype(BF16)
            v_s[pl.ds(r0, tk), :] = kv[:, HEAD_DIM:].astype(BF16)
            return c
        lax.fori_loop(0, s_len // tk, build, 0)

    q = qaug_ref[...][:, :, :kdim].reshape(N_HEADS * tq, kdim)
    row = lax.broadcasted_iota(I32, (N_HEADS * tq, tk), 0)
    t_q = q0 + (row & (tq - 1))
    head = lax.broadcasted_iota(I32, (N_HEADS * tq, 1), 0) // tq
    slope = jnp.where(head == 0, NSA_SLOPES[0], jnp.where(head == 1, NSA_SLOPES[1],
                      jnp.where(head == 2, NSA_SLOPES[2], NSA_SLOPES[3])))
    col = lax.broadcasted_iota(I32, (N_HEADS * tq, tk), 1)

    n_kt = s_len // tk
    need_base = (pl.program_id(0) * pl.num_programs(1) + qi) * n_kt

    def body(kt, carry):
        if window:
            return tile(kt, carry)
        return lax.cond(need_ref[need_base + kt] != 0, lambda c: tile(kt, c), lambda c: c, carry)

    def tile(kt, carry):
        m, l, acc = carry
        k0 = pl.multiple_of(kt * tk, tk)
        s = lax.dot_general(q, kaug_s[pl.ds(k0, tk), :], (((1,), (1,)), ((), ())), preferred_element_type=F32)
        t_k = k0 + col
        s = s + slope * (t_k - q0).astype(F32)
        ok = t_k <= t_q
        if window:
            ok = ok & (t_q - t_k < NSA_WINDOW)
        m_new = jnp.maximum(m, jnp.max(jnp.where(ok, s, NEG_BIG), axis=1, keepdims=True))
        p = jnp.where(ok, jnp.exp(s - m_new), 0.0)
        alpha = jnp.exp(m - m_new)
        l = alpha * l + jnp.sum(p, axis=1, keepdims=True)
        acc = alpha * acc + jnp.dot(p.astype(BF16), v_s[pl.ds(k0, tk), :], preferred_element_type=F32)
        return m_new, l, acc

    lo = jnp.maximum(q0 - NSA_WINDOW, 0) // tk if window else 0
    hi = (q0 + tq - 1) // tk + 1
    init = (jnp.full((N_HEADS * tq, 1), NEG_BIG, F32), jnp.zeros((N_HEADS * tq, 1), F32),
            jnp.zeros((N_HEADS * tq, HEAD_DIM), F32))
    m, l, acc = lax.fori_loop(lo, hi, body, init)
    o_ref[...] = (acc / l).reshape(N_HEADS, tq, HEAD_DIM)


def _nsa_overlap(s):
    nbp, nsel = s // NSA_STRIDE, s // NSA_SEL_BLOCK
    cs = np.arange(nbp) * NSA_STRIDE
    ss = np.arange(128) * NSA_SEL_BLOCK
    ov = np.minimum(cs[:, None] + 2 * NSA_STRIDE, ss[None, :] + NSA_SEL_BLOCK) - np.maximum(cs[:, None], ss[None, :])
    ov = np.clip(ov, 0, None) / (2 * NSA_STRIDE)
    ov[nbp - 1:, :] = 0.0
    ov[:, nsel:] = 0.0
    return jnp.asarray(ov, BF16)


def _nsa_branches(cols, cmp_pos, cmp_w1, cmp_w2):
    bsz, s, _ = cols.shape
    nbp = s // NSA_STRIDE
    tq = NSA_TQ
    kvcmp = pl.pallas_call(
        _nsa_compress_kernel,
        grid=(bsz,),
        in_specs=[pl.BlockSpec((None, s, 128), lambda b: (b, 0, COL["nsa_kvc"] // 128)),
                  pl.BlockSpec((2, 32, 64), lambda b: (0, 0, 0)),
                  pl.BlockSpec((2, 2048, 256), lambda b: (0, 0, 0)),
                  pl.BlockSpec((2, 256, 64), lambda b: (0, 0, 0))],
        out_specs=pl.BlockSpec((None, nbp, 128), lambda b: (b, 0, 0)),
        out_shape=jax.ShapeDtypeStruct((bsz, nbp, 128), F32),
        compiler_params=_cparams(("arbitrary",), 48),
    )(cols, cmp_pos, cmp_w1.astype(BF16), cmp_w2)
    head_out = lambda dt, w: (pl.BlockSpec((None, N_HEADS, tq, w), lambda b, i: (b, 0, i, 0)),
                              jax.ShapeDtypeStruct((bsz, N_HEADS, s, w), dt))
    (oc_spec, oc_shape), (qa_spec, qa_shape) = head_out(F32, HEAD_DIM), head_out(BF16, 256)
    nqt = s // tq
    o_cmp, qaug, used = pl.pallas_call(
        _nsa_cmp_kernel,
        grid=(bsz, nqt),
        in_specs=[pl.BlockSpec((None, tq, 256), lambda b, i: (b, i, COL["nsa_q"] // 256)),
                  pl.BlockSpec((None, nbp, 128), lambda b, i: (b, 0, 0)),
                  pl.BlockSpec((nbp, 128), lambda b, i: (0, 0))],
        out_specs=[oc_spec, qa_spec, pl.BlockSpec((None, None, 8, 128), lambda b, i: (b, i, 0, 0))],
        out_shape=[oc_shape, qa_shape, jax.ShapeDtypeStruct((bsz, nqt, 8, 128), F32)],
        compiler_params=_cparams(("arbitrary", "arbitrary"), 48),
    )(cols, kvcmp, _nsa_overlap(s))
    tk = min(NSA_TK, s)
    bpt = tk // NSA_SEL_BLOCK
    need = (used[:, :, 0, :(s // tk) * bpt].reshape(bsz, nqt, s // tk, bpt).max(axis=-1) > 0).astype(I32).reshape(-1)

    def flash(name, window):
        hmap = lambda b, i, need_ref: (b, 0, i, 0)
        return pl.pallas_call(
            functools.partial(_nsa_flash_kernel, window=window),
            grid_spec=pltpu.PrefetchScalarGridSpec(
                num_scalar_prefetch=1,
                grid=(bsz, nqt),
                in_specs=[pl.BlockSpec((None, N_HEADS, tq, 256), hmap),
                          pl.BlockSpec((None, s, 128), lambda b, i, need_ref: (b, 0, COL[name] // 128))],
                out_specs=pl.BlockSpec((None, N_HEADS, tq, HEAD_DIM), hmap),
                scratch_shapes=[pltpu.VMEM((s, 128 if window else 256), BF16), pltpu.VMEM((s, HEAD_DIM), BF16)]),
            out_shape=oc_shape,
            compiler_params=_cparams(("arbitrary", "arbitrary"), 48),
        )(need, qaug, cols)

    return o_cmp, flash("nsa_kvs", False), flash("nsa_kvw", True)


MIX_TILE = 256


def _mix_out_kernel(h_ref, ssm_ref, lru_ref, rw_ref, oc_ref, os_ref, ow_ref, gate_ref, w_ref, o_ref):
    acc = h_ref[...]
    acc = acc + _dot(ssm_ref[...], w_ref[0:256, :])
    acc = acc + _dot(lru_ref[...], w_ref[256:512, :])
    acc = acc + _dot(rw_ref[...], w_ref[512:768, :])
    g = _sigmoid(gate_ref[...])
    for h in range(N_HEADS):
        y = (g[:, 3 * h:3 * h + 1] * oc_ref[h] + g[:, 3 * h + 1:3 * h + 2] * os_ref[h]
             + g[:, 3 * h + 2:3 * h + 3] * ow_ref[h])
        acc = acc + _dot(y, w_ref[768 + HEAD_DIM * h:768 + HEAD_DIM * (h + 1), :])
    o_ref[...] = acc


def _mix_out(h, y_ssm, y_lru, y_rw, o_cmp, o_sel, o_win, cols, w_out):
    bsz, s, d = h.shape
    t = MIX_TILE
    tok = lambda w: pl.BlockSpec((None, t, w), lambda b, i: (b, i, 0))
    hd = pl.BlockSpec((None, N_HEADS, t, HEAD_DIM), lambda b, i: (b, 0, i, 0))
    return pl.pallas_call(
        _mix_out_kernel,
        grid=(bsz, s // t),
        in_specs=[tok(d), tok(256), tok(256), tok(256), hd, hd, hd,
                  pl.BlockSpec((None, t, 128), lambda b, i: (b, i, COL["nsa_gate"] // 128)),
                  pl.BlockSpec((4 * GROUP_WIDTH, d), lambda b, i: (0, 0))],
        out_specs=tok(d),
        out_shape=jax.ShapeDtypeStruct((bsz, s, d), F32),
        compiler_params=_cparams(("arbitrary", "arbitrary")),
    )(h, y_ssm, y_lru, y_rw, o_cmp, o_sel, o_win, cols, w_out.astype(BF16))


PEER_HEADS = 8
PEER_NKEYS = 128
PEER_TOPK = 16
PEER_TT = 256
PEER_TQ = 256
PEER_OUT_UNROLL = 128


PEER_ROW = 4


def _store_packed(x, o_ref):
    rows, d = x.shape
    b = pltpu.bitcast(x, I32)
    r = b + 0x7FFF + ((b >> 16) & 1)
    word = (r[:, d // 2:] & jnp.int32(-65536)) | lax.shift_right_logical(r[:, :d // 2], 16)
    for s in range(PEER_ROW):
        o_ref[pl.ds(s, rows, stride=PEER_ROW), :] = word[:, 128 * s:128 * (s + 1)]


def _norm_matmul2_kernel(x_ref, g_ref, w_ref, o_ref, xpk_ref, xn_ref):
    @pl.when(pl.program_id(1) == 0)
    def _():
        x = x_ref[...]
        y = x * lax.rsqrt(jnp.mean(x * x, axis=-1, keepdims=True) + NORM_EPS) * g_ref[...]
        _store_packed(y, xpk_ref)
        xn_ref[...] = y.astype(BF16)

    o_ref[...] = jnp.dot(xn_ref[...], w_ref[...], preferred_element_type=F32)


def _norm_matmul2(x, g, w, tm, tn):
    m, k = x.shape
    n = w.shape[1]
    return pl.pallas_call(
        _norm_matmul2_kernel,
        grid=(m // tm, n // tn),
        in_specs=[pl.BlockSpec((tm, k), lambda i, j: (i, 0)),
                  pl.BlockSpec((1, k), lambda i, j: (0, 0)),
                  pl.BlockSpec((k, tn), lambda i, j: (0, j))],
        out_specs=[pl.BlockSpec((tm, tn), lambda i, j: (i, j)),
                   pl.BlockSpec((tm * PEER_ROW, 128), lambda i, j: (i, 0))],
        out_shape=[jax.ShapeDtypeStruct((m, n), F32), jax.ShapeDtypeStruct((m * PEER_ROW, 128), I32)],
        scratch_shapes=[pltpu.VMEM((tm, k), BF16)],
        compiler_params=_cparams(("arbitrary", "arbitrary")),
    )(x, g, w)


def _top_rows(x, n, payload=None):
    r = x.shape[0]
    rid = lax.broadcasted_iota(I32, x.shape, 0).astype(F32)
    vals, ids, pays = [], [], []
    for _ in range(n):
        mx = jnp.max(x, axis=0, keepdims=True)
        first = jnp.min(jnp.where(x == mx, rid, float(r)), axis=0, keepdims=True)
        hit = rid == first
        vals.append(mx)
        ids.append(first)
        if payload is not None:
            pays.append(jnp.sum(jnp.where(hit, payload, 0.0), axis=0, keepdims=True))
        x = jnp.where(hit, -jnp.inf, x)
    cat = lambda v: jnp.concatenate(v, axis=0)
    return (cat(vals), cat(ids)) + ((cat(pays),) if payload is not None else ())


def _peer_route_kernel(q_ref, keys_ref, eidx_ref, gate_ref):
    k = PEER_TOPK
    tq = q_ref.shape[0]
    gates, eids = [], []
    for h in range(PEER_HEADS):
        s, i = [], []
        for c in range(2):
            qh = q_ref[:, 128 * (2 * h + c):128 * (2 * h + c + 1)]
            sv, si = _top_rows(_dot3_nt(keys_ref[h, c], qh), k)
            s.append(sv)
            i.append(si)
        cand, cidx = [], []
        for a in range(k):
            nb = k // (a + 1)
            cand.append(s[0][a:a + 1, :] + s[1][:nb])
            cidx.append(i[0][a:a + 1, :] * float(PEER_NKEYS) + i[1][:nb])
        n_c = sum(c.shape[0] for c in cand)
        pad = (-n_c) % 8
        cand.append(jnp.full((pad, tq), -jnp.inf, F32))
        cidx.append(jnp.zeros((pad, tq), F32))
        top, _, eidx = _top_rows(jnp.concatenate(cand, axis=0), k, payload=jnp.concatenate(cidx, axis=0))
        e = jnp.exp(top - top[0:1, :])
        gates.append(e / jnp.sum(e, axis=0, keepdims=True))
        eids.append(eidx)
    gate_ref[...] = jnp.concatenate(gates, axis=0).T
    eidx_ref[...] = (jnp.concatenate(eids, axis=0).T * float(PEER_ROW)).astype(I32)


def _peer_route(q, keys):
    t = q.shape[0]
    tq = min(PEER_TQ, t)
    nk = PEER_HEADS * PEER_TOPK
    return pl.pallas_call(
        _peer_route_kernel,
        grid=(t // tq,),
        in_specs=[pl.BlockSpec((tq, q.shape[1]), lambda i: (i, 0)),
                  pl.BlockSpec(keys.shape, lambda i: (0, 0, 0, 0))],
        out_specs=[pl.BlockSpec((tq, nk), lambda i: (i, 0)), pl.BlockSpec((tq, nk), lambda i: (i, 0))],
        out_shape=[jax.ShapeDtypeStruct((t, nk), I32), jax.ShapeDtypeStruct((t, nk), F32)],
        compiler_params=_cparams(("arbitrary",), 48),
    )(q, keys)


PACK_TILE = 512


def _pack_kernel(t_ref, o_ref):
    _store_packed(t_ref[...], o_ref)


def _pack_table(tab):
    e, d = tab.shape
    te = PACK_TILE
    return pl.pallas_call(
        _pack_kernel,
        grid=(e // te,),
        in_specs=[pl.BlockSpec((te, d), lambda i: (i, 0))],
        out_specs=pl.BlockSpec((te * PEER_ROW, 128), lambda i: (i, 0)),
        out_shape=jax.ShapeDtypeStruct((e * PEER_ROW, 128), I32),
        compiler_params=_cparams(("arbitrary",)),
    )(tab)


def _expert_halves(tbl_ref, r):
    word = tbl_ref[pl.ds(pl.multiple_of(r, PEER_ROW), PEER_ROW), :]
    return pltpu.bitcast(word << 16, F32), pltpu.bitcast(word & jnp.int32(-65536), F32)


_BITREV8 = (0, 4, 2, 6, 1, 5, 3, 7)


def _rowsum_pairs(cur):
    sub = lax.broadcasted_iota(I32, (8, 128), 0)
    for h in (2, 1):
        m = (sub & h) == 0
        cur = [jnp.where(m, x, y) + jnp.where(m, pltpu.roll(x, 8 - h, axis=0), pltpu.roll(y, h, axis=0))
               for x, y in zip(cur[0::2], cur[1::2])]
    return cur[0]


def _peer_act_kernel(idx_ref, x_ref, gate_ref, tbl_ref, o_ref, q_ref, s_ref):
    tt, nk = gate_ref.shape
    sub = lax.broadcasted_iota(I32, (8, 128), 0)

    @pl.when(pl.program_id(0) == 0)
    def _():
        q_ref[...] = jnp.zeros_like(q_ref)

    def finish(slot, t):
        s_ref[pl.ds(t, 1), :] = jnp.sum(q_ref[slot].T, axis=0, keepdims=True)

    def tok(t, c):
        finish((t + 1) % 2, jnp.maximum(t - 1, 0))
        xw = x_ref[pl.ds(pl.multiple_of(t * PEER_ROW, PEER_ROW), PEER_ROW), :]
        xb = pltpu.bitcast(jnp.concatenate([xw, xw], axis=0), BF16)
        slot = t % 2
        for g in range(nk // 8):
            fs = []
            for i in range(4):
                ra = pl.multiple_of(idx_ref[t * nk + 8 * g + _BITREV8[2 * i]], PEER_ROW)
                rb = pl.multiple_of(idx_ref[t * nk + 8 * g + _BITREV8[2 * i + 1]], PEER_ROW)
                pa = pltpu.bitcast(pltpu.bitcast(tbl_ref[pl.ds(ra + PEER_ROW, 8), :], BF16) * xb, I32)
                pb = pltpu.bitcast(pltpu.bitcast(tbl_ref[pl.ds(rb, 8), :], BF16) * xb, I32)
                prod = jnp.where(sub < PEER_ROW, pa, pb)
                fs.append(pltpu.bitcast(prod << 16, F32) + pltpu.bitcast(prod & jnp.int32(-65536), F32))
            q_ref[slot, 8 * g:8 * (g + 1), :] = _rowsum_pairs(fs)
        return c

    lax.fori_loop(0, tt, tok, 0)
    finish((tt - 1) % 2, tt - 1)
    o_ref[...] = gate_ref[...] * _gelu_tanh(s_ref[...])


def _peer_out_kernel(idx_ref, w_ref, tbl_ref, h_ref, o_ref):
    tt = h_ref.shape[0]
    nk = idx_ref.shape[0] // tt

    def tok(t, c):
        def grp(g, accs):
            accs = list(accs)
            k0 = t * nk + g * PEER_OUT_UNROLL
            for j in range(PEER_OUT_UNROLL):
                w = w_ref[k0 + j]
                vlo, vhi = _expert_halves(tbl_ref, idx_ref[k0 + j])
                a = 2 * (j % 2)
                accs[a] = accs[a] + vlo * w
                accs[a + 1] = accs[a + 1] + vhi * w
            return tuple(accs)

        z = jnp.zeros((PEER_ROW, 128), F32)
        l0, h0, l1, h1 = lax.fori_loop(0, nk // PEER_OUT_UNROLL, grp, (z, z, z, z))
        o_ref[t] = h_ref[t] + jnp.concatenate([l0 + l1, h0 + h1], axis=0)
        return c

    lax.fori_loop(0, tt, tok, 0)


def _peer_experts(eidx_t, gate_t, xn3, h3, tbl_u, tbl_v):
    t, nk = eidx_t.shape
    tt = min(PEER_TT, t)
    smem = lambda: pl.BlockSpec((tt * nk,), lambda i: (i,), memory_space=pltpu.SMEM)
    vm = lambda: pl.BlockSpec((tt, nk), lambda i: (i, 0))
    tok = lambda: pl.BlockSpec((tt, 8, 128), lambda i: (i, 0, 0))
    tbl = lambda a: pl.BlockSpec(a.shape, lambda i: (0, 0), pipeline_mode=pl.Buffered(1))
    eidx_flat = eidx_t.reshape(t * nk)
    w_t = pl.pallas_call(
        _peer_act_kernel,
        grid=(t // tt,),
        in_specs=[smem(), pl.BlockSpec((tt * PEER_ROW, 128), lambda i: (i, 0)), vm(), tbl(tbl_u)],
        out_specs=vm(),
        out_shape=jax.ShapeDtypeStruct((t, nk), F32),
        scratch_shapes=[pltpu.VMEM((2, nk, 128), F32), pltpu.VMEM((tt, nk), F32)],
        compiler_params=_cparams(("arbitrary",), 48),
    )(eidx_flat, xn3, gate_t, tbl_u)
    return pl.pallas_call(
        _peer_out_kernel,
        grid=(t // tt,),
        in_specs=[smem(), smem(), tbl(tbl_v), tok()],
        out_specs=tok(),
        out_shape=jax.ShapeDtypeStruct(h3.shape, F32),
        compiler_params=_cparams(("arbitrary",), 48),
    )(eidx_flat, w_t.reshape(t * nk), tbl_v, h3)


def _peer_block(h, g, wq, keys, u, v):
    t, d = h.shape
    q, xpk = _norm_matmul2(h, g.reshape(1, d), wq.astype(BF16), min(512, t), 1024)
    eidx_t, gate_t = _peer_route(q, keys)
    tbl_u = jnp.pad(_pack_table(u), ((PEER_ROW, PEER_ROW), (0, 0)))
    out = _peer_experts(eidx_t, gate_t, xpk, h.reshape(t, 8, 128), tbl_u, _pack_table(v))
    return out.reshape(t, d)


def _rmsnorm_kernel(x_ref, g_ref, o_ref):
    x = x_ref[...]
    o_ref[...] = x * lax.rsqrt(jnp.mean(x * x, axis=-1, keepdims=True) + NORM_EPS) * g_ref[...]


def _rmsnorm(x, g, tm=512):
    m, d = x.shape
    return pl.pallas_call(
        _rmsnorm_kernel,
        grid=(m // tm,),
        in_specs=[pl.BlockSpec((tm, d), lambda i: (i, 0)), pl.BlockSpec((1, d), lambda i: (0, 0))],
        out_specs=pl.BlockSpec((tm, d), lambda i: (i, 0)),
        out_shape=jax.ShapeDtypeStruct((m, d), F32),
        compiler_params=_cparams(("arbitrary",)),
    )(x, g.reshape(1, d))


def kernel(x, mix_norm, w_in, w_out, ssm_conv_w, ssm_conv_b, ssm_dt_bias, ssm_a_log, ssm_d, ssm_norm, lru_conv_w, lru_conv_b, lru_wa, lru_ba, lru_wi, lru_bi, lru_lambda, rwkv_mu, rwkv_w0, rwkv_w2, rwkv_a0, rwkv_a2, rwkv_g2, rwkv_kk, rwkv_ka, rwkv_rk, rwkv_ln_w, rwkv_ln_b, nsa_cmp_pos, nsa_cmp_w1, nsa_cmp_w2, ffn_norm, peer_wq, peer_keys, peer_u, peer_v, final_norm):
    bsz, s, d = x.shape
    t = bsz * s
    h = x
    for l in range(w_in.shape[0]):
        w_l = _arrange_cols(w_in[l]).astype(BF16)
        cols = _norm_matmul(h.reshape(t, d), mix_norm[l].reshape(1, d), w_l, 512, 640).reshape(bsz, s, NCOL)
        y_ssm = _ssd_mixer(cols, ssm_conv_w[l], ssm_conv_b[l], ssm_dt_bias[l], ssm_a_log[l], ssm_d[l], ssm_norm[l])
        y_lru = _lru_mixer(cols, lru_conv_w[l], lru_conv_b[l], lru_wa[l], lru_ba[l], lru_wi[l], lru_bi[l], lru_lambda[l])
        y_rw = _rwkv_mixer(cols, rwkv_mu[l], rwkv_w0[l], rwkv_w2[l], rwkv_a0[l], rwkv_a2[l], rwkv_g2[l],
                           rwkv_kk[l], rwkv_ka[l], rwkv_rk[l], rwkv_ln_w[l], rwkv_ln_b[l])
        o_cmp, o_sel, o_win = _nsa_branches(cols, nsa_cmp_pos[l], nsa_cmp_w1[l], nsa_cmp_w2[l])
        h = _mix_out(h, y_ssm, y_lru, y_rw, o_cmp, o_sel, o_win, cols, w_out[l])
        h = _peer_block(h.reshape(t, d), ffn_norm[l], peer_wq[l], peer_keys[l], peer_u[l], peer_v[l]).reshape(bsz, s, d)
    return _rmsnorm(h.reshape(t, d), final_norm).reshape(bsz, s, d)
```

```python
import functools
import math

import numpy as np
import jax
import jax.numpy as jnp
from jax import lax
from jax.experimental import pallas as pl
from jax.experimental.pallas import tpu as pltpu

F32 = jnp.float32
BF16 = jnp.bfloat16
I32 = jnp.int32

NORM_EPS = 1e-6
GROUP_WIDTH = 256
HEAD_DIM = 64
N_HEADS = 4
LRU_C = 8.0
NEG_BIG = -1e30


COL = dict(ssd_z=0, ssd_x=256, lru_gate=512, lru_x=768, rw_r=1024, rw_k=1280, rw_v=1536, rw_wa=1792, rw_g=1920,
           nsa_q=2048, nsa_kvc=2304, nsa_kvs=2432, nsa_kvw=2560, nsa_gate=2688, ssd_b=2816, ssd_c=2944, ssd_dt=3072)
NCOL = 3200


def _col_perm():
    src = np.full((NCOL,), -1, np.int64)

    def put(dst, s0, n):
        src[dst:dst + n] = np.arange(s0, s0 + n)

    put(COL["ssd_z"], 0, 256)
    put(COL["ssd_x"], 256, 256)
    put(COL["ssd_b"], 512, 128)
    put(COL["ssd_c"], 640, 128)
    put(COL["ssd_dt"], 768, 4)
    put(COL["lru_gate"], 772, 256)
    put(COL["lru_x"], 1028, 256)
    put(COL["rw_r"], 1284, 1024)
    put(COL["nsa_q"], 2308, 256)
    put(COL["nsa_kvc"], 2564, 128)
    put(COL["nsa_kvs"], 2692, 128)
    put(COL["nsa_kvw"], 2820, 128)
    put(COL["nsa_gate"], 2948, 12)
    return src


def _arrange_cols(a):
    src = _col_perm()
    out = jnp.take(a, jnp.asarray(np.maximum(src, 0)), axis=-1)
    return jnp.where(jnp.asarray(src >= 0), out, jnp.zeros((), a.dtype))


def _cparams(sem, vmem_mb=None):
    kw = dict(dimension_semantics=sem)
    if vmem_mb is not None:
        kw["vmem_limit_bytes"] = vmem_mb * 1024 * 1024
    return pltpu.CompilerParams(**kw)


def _dot(a, b):
    return jnp.dot(a.astype(BF16), b.astype(BF16), preferred_element_type=F32)


def _dot_nt(a, b):
    return lax.dot_general(a.astype(BF16), b.astype(BF16), (((1,), (1,)), ((), ())), preferred_element_type=F32)


def _dot_tn(a, b):
    return lax.dot_general(a.astype(BF16), b.astype(BF16), (((0,), (0,)), ((), ())), preferred_element_type=F32)


def _split3(x):
    h = x.astype(BF16)
    r = x - h.astype(F32)
    m = r.astype(BF16)
    l = (r - m.astype(F32)).astype(BF16)
    return h, m, l


def _split2(x):
    h = x.astype(BF16)
    return h, (x - h.astype(F32)).astype(BF16)


def _dot_exact_rhs(a, b_exact):
    h, m, l = _split3(a)
    d = lambda t: jnp.dot(t, b_exact, preferred_element_type=F32)
    return d(h) + d(m) + d(l)


def _dot_exact_lhs(a_exact, b):
    h, m, l = _split3(b)
    d = lambda t: jnp.dot(a_exact, t, preferred_element_type=F32)
    return d(h) + d(m) + d(l)


def _sigmoid(x):
    return 1.0 / (1.0 + jnp.exp(-x))


def _silu(x):
    return x * _sigmoid(x)


def _gelu_tanh(x):
    return 0.5 * x * (1.0 + jnp.tanh(math.sqrt(2.0 / math.pi) * (x + 0.044715 * (x * x * x))))


def _softplus(x):
    return jnp.maximum(x, 0.0) + jnp.log1p(jnp.exp(-jnp.abs(x)))


def _conv4(cur, tail, w, b):
    c = cur.shape[1]
    row8 = lax.broadcasted_iota(I32, (8, c), 0)
    acc = cur * w[3:4, :] + b
    for k in (1, 2, 3):
        cr = pltpu.roll(cur, k, axis=0)
        tr = pltpu.roll(tail, k, axis=0)
        head = jnp.where(row8 < k, tr, cr[:8])
        sh = jnp.concatenate([head, cr[8:]], axis=0)
        acc = acc + sh * w[3 - k:4 - k, :]
    return acc


def _norm_matmul_kernel(x_ref, g_ref, w_ref, o_ref, xn_ref):
    @pl.when(pl.program_id(1) == 0)
    def _():
        x = x_ref[...]
        y = x * lax.rsqrt(jnp.mean(x * x, axis=-1, keepdims=True) + NORM_EPS) * g_ref[...]
        xn_ref[...] = y.astype(BF16)

    o_ref[...] = jnp.dot(xn_ref[...], w_ref[...], preferred_element_type=F32)


def _norm_matmul(x, g, w, tm, tn):
    m, k = x.shape
    n = w.shape[1]
    return pl.pallas_call(
        _norm_matmul_kernel,
        grid=(m // tm, n // tn),
        in_specs=[pl.BlockSpec((tm, k), lambda i, j: (i, 0)),
                  pl.BlockSpec((1, k), lambda i, j: (0, 0)),
                  pl.BlockSpec((k, tn), lambda i, j: (0, j))],
        out_specs=pl.BlockSpec((tm, tn), lambda i, j: (i, j)),
        out_shape=jax.ShapeDtypeStruct((m, n), F32),
        scratch_shapes=[pltpu.VMEM((tm, k), BF16)],
        compiler_params=_cparams(("arbitrary", "arbitrary")),
    )(x, g, w)


SSD_CHUNK = 128


def _ssd_kernel(z_ref, x_ref, b_ref, c_ref, dt_ref, cwx_ref, cbx_ref, cwb_ref, cbb_ref, cwc_ref, cbc_ref,
                dtb_ref, alog_ref, d_ref, nw_ref, o_ref, tx_ref, tb_ref, tc_ref, st_ref):
    t = SSD_CHUNK

    @pl.when(pl.program_id(1) == 0)
    def _():
        tx_ref[...] = jnp.zeros_like(tx_ref)
        tb_ref[...] = jnp.zeros_like(tb_ref)
        tc_ref[...] = jnp.zeros_like(tc_ref)
        st_ref[...] = jnp.zeros_like(st_ref)

    xr, br, cr = x_ref[...], b_ref[...], c_ref[...]
    xs = _silu(_conv4(xr, tx_ref[...], cwx_ref[...], cbx_ref[...]))
    bm = _silu(_conv4(br, tb_ref[...], cwb_ref[...], cbb_ref[...]))
    cm = _silu(_conv4(cr, tc_ref[...], cwc_ref[...], cbc_ref[...]))
    tx_ref[...] = xr[t - 8:]
    tb_ref[...] = br[t - 8:]
    tc_ref[...] = cr[t - 8:]

    dt = _softplus(dt_ref[...] + dtb_ref[...])
    a = dt * (-jnp.exp(alog_ref[...]))
    ri = lax.broadcasted_iota(I32, (t, t), 0)
    ci = lax.broadcasted_iota(I32, (t, t), 1)
    tril = ri >= ci
    acs = _dot_exact_lhs(jnp.where(tril, 1.0, 0.0).astype(BF16), a)
    acs_t = acs.T
    d_row = d_ref[...]
    ys = []
    for h in range(N_HEADS):
        g = h // 2
        lo, hi = HEAD_DIM * h, HEAD_DIM * (h + 1)
        b_g = bm[:, HEAD_DIM * g:HEAD_DIM * (g + 1)]
        c_g = cm[:, HEAD_DIM * g:HEAD_DIM * (g + 1)]
        x_h = xs[:, lo:hi]
        acs_c = acs[:, h:h + 1]
        lmat = jnp.exp(jnp.where(tril, acs_c - acs_t[h:h + 1, :], NEG_BIG))
        scores = _dot_nt(c_g, b_g) * lmat
        xdt = x_h * dt[:, h:h + 1]
        st = st_ref[h]
        y = _dot(scores, xdt) + _dot(c_g, st) * jnp.exp(acs_c) + x_h * d_row[:, lo:hi]
        a_last = acs[t - 1:t, h:h + 1]
        bw = b_g * jnp.exp(a_last - acs_c)
        st_ref[h] = st * jnp.exp(a_last) + _dot_tn(bw, xdt)
        ys.append(y)
    y = jnp.concatenate(ys, axis=1) * _silu(z_ref[...])
    outs = []
    for g in range(2):
        yg = y[:, 128 * g:128 * (g + 1)]
        outs.append(yg * lax.rsqrt(jnp.mean(yg * yg, axis=-1, keepdims=True) + NORM_EPS))
    o_ref[...] = jnp.concatenate(outs, axis=1) * nw_ref[...]


def _ssd_mixer(cols, conv_w, conv_b, dt_bias, a_log, d_skip, norm_w):
    bsz, s, _ = cols.shape
    t = SSD_CHUNK
    row = lambda v, n: jnp.pad(v.reshape(1, -1), ((0, 0), (0, n - v.size)))
    full = lambda shp: pl.BlockSpec(shp, lambda b, c: (0,) * len(shp))
    colspec = lambda w, name: pl.BlockSpec((None, t, w), lambda b, c: (b, c, COL[name] // w))
    return pl.pallas_call(
        _ssd_kernel,
        grid=(bsz, s // t),
        in_specs=[colspec(256, "ssd_z"), colspec(256, "ssd_x"), colspec(128, "ssd_b"), colspec(128, "ssd_c"),
                  colspec(128, "ssd_dt"),
                  full((4, 256)), full((1, 256)), full((4, 128)), full((1, 128)), full((4, 128)), full((1, 128)),
                  full((1, 128)), full((1, 128)), full((1, 256)), full((1, 256))],
        out_specs=pl.BlockSpec((None, t, 256), lambda b, c: (b, c, 0)),
        out_shape=jax.ShapeDtypeStruct((bsz, s, 256), F32),
        scratch_shapes=[pltpu.VMEM((8, 256), F32), pltpu.VMEM((8, 128), F32), pltpu.VMEM((8, 128), F32),
                        pltpu.VMEM((N_HEADS, HEAD_DIM, HEAD_DIM), F32)],
        compiler_params=_cparams(("arbitrary", "arbitrary")),
    )(cols, cols, cols, cols, cols,
      conv_w[:, :256], conv_b[:256].reshape(1, 256), conv_w[:, 256:384], conv_b[256:384].reshape(1, 128),
      conv_w[:, 384:512], conv_b[384:512].reshape(1, 128),
      row(dt_bias, 128), row(a_log, 128), jnp.repeat(d_skip, HEAD_DIM).reshape(1, 256), norm_w.reshape(1, 256))


LRU_TILE = 256


def _lru_kernel(gate_ref, xr_ref, cw_ref, cb_ref, wa_ref, ba_ref, wi_ref, bi_ref, lam_ref, o_ref, tail_ref, h_ref):
    t = LRU_TILE

    @pl.when(pl.program_id(1) == 0)
    def _():
        tail_ref[...] = jnp.zeros_like(tail_ref)
        h_ref[...] = jnp.zeros_like(h_ref)

    xr = xr_ref[...]
    xc = _conv4(xr, tail_ref[...], cw_ref[...], cb_ref[...])
    tail_ref[...] = xr[t - 8:]
    r = _sigmoid(_dot(xc, wa_ref[...]) + ba_ref[...])
    i = _sigmoid(_dot(xc, wi_ref[...]) + bi_ref[...])
    log_a = LRU_C * r * (-_softplus(-lam_ref[...]))
    a = jnp.exp(log_a)
    u = jnp.sqrt(-jnp.tanh(log_a) * (a * a + 1.0)) * (i * xc)
    rows = lax.broadcasted_iota(I32, (t, GROUP_WIDTH), 0)
    d = 1
    while d < t:
        a_sh = pltpu.roll(a, d, axis=0)
        u_sh = pltpu.roll(u, d, axis=0)
        m = rows >= d
        u = jnp.where(m, a * u_sh + u, u)
        a = jnp.where(m, a * a_sh, a)
        d *= 2
    h = a * h_ref[7:8, :] + u
    h_ref[...] = h[t - 8:]
    o_ref[...] = h * _gelu_tanh(gate_ref[...])


def _lru_mixer(cols, conv_w, conv_b, wa, ba, wi, bi, lam):
    bsz, s, _ = cols.shape
    t = LRU_TILE
    cb = COL["lru_gate"] // 256
    bd = lambda w: jax.scipy.linalg.block_diag(*[w[k] for k in range(w.shape[0])]).astype(BF16)
    full = lambda shp: pl.BlockSpec(shp, lambda b, c: (0,) * len(shp))
    return pl.pallas_call(
        _lru_kernel,
        grid=(bsz, s // t),
        in_specs=[pl.BlockSpec((None, t, 256), lambda b, c: (b, c, cb)),
                  pl.BlockSpec((None, t, 256), lambda b, c: (b, c, cb + 1)),
                  full((4, 256)), full((1, 256)), full((256, 256)), full((1, 256)), full((256, 256)), full((1, 256)),
                  full((1, 256))],
        out_specs=pl.BlockSpec((None, t, 256), lambda b, c: (b, c, 0)),
        out_shape=jax.ShapeDtypeStruct((bsz, s, 256), F32),
        scratch_shapes=[pltpu.VMEM((8, 256), F32), pltpu.VMEM((8, 256), F32)],
        compiler_params=_cparams(("arbitrary", "arbitrary")),
    )(cols, cols, conv_w, conv_b.reshape(1, 256), bd(wa), ba.reshape(1, 256), bd(wi), bi.reshape(1, 256),
      lam.reshape(1, 256))


RW_TILE = 128
RW_GROUP = 4
RW_READOUT_PASSES = 1
RWKV_GN_EPS = 64e-5


def _segsum_bc(x):
    r = x.shape[0]
    m = lax.broadcasted_iota(I32, (r, 128), 1) < HEAD_DIM
    outs = []
    for j in range(2):
        xj = x[:, 128 * j:128 * (j + 1)]
        lo = jnp.sum(jnp.where(m, xj, 0.0), axis=1, keepdims=True)
        hi = jnp.sum(jnp.where(m, 0.0, xj), axis=1, keepdims=True)
        outs.append(jnp.where(m, lo, hi))
    return jnp.concatenate(outs, axis=1)


def _rwkv_kernel(c_ref, mu_ref, w0_ref, w2_ref, a0_ref, a2_ref, g2_ref, kkp_ref, ka_ref, rk_ref, lnw_ref, lnb_ref, e_ref,
                 o_ref, prev_ref, s_ref, r_s, w_s, k_s, v_s, kk_s, b_s, g_s, bon_s, y_s):
    nb = c_ref.shape[0]
    t = RW_TILE

    @pl.when(pl.program_id(0) == 0)
    def _():
        prev_ref[...] = jnp.zeros_like(prev_ref)
        s_ref[...] = jnp.zeros_like(s_ref)
        r_s[...] = jnp.zeros_like(r_s)

    row = lax.broadcasted_iota(I32, (t, 1024), 0)
    for b in range(nb):
        c = c_ref[b]
        pr = jnp.where(row == 0, prev_ref[b, 7:8, :], pltpu.roll(c, 1, axis=0))
        prev_ref[b] = c[t - 8:]
        c = c + (pr - c) * mu_ref[...]
        r, k, v = c[:, 0:256], c[:, 256:512], c[:, 512:768]
        wd, ad, gd = c[:, 768:832], c[:, 832:896], c[:, 896:1024]
        w_log = -_softplus(-(w0_ref[...] + _dot(jnp.tanh(wd), w2_ref[...]))) - 0.5
        a = _sigmoid(a0_ref[...] + _dot(ad, a2_ref[...]))
        kk = k * kkp_ref[...]
        kk = kk / jnp.maximum(jnp.sqrt(_segsum_bc(kk * kk)), 1e-12)
        k2 = k * (1.0 + (a - 1.0) * ka_ref[...])
        r_s[b, 8:, :] = r
        w_s[b] = jnp.exp(-jnp.exp(w_log))
        k_s[b] = k2
        v_s[b] = v
        kk_s[b] = kk
        b_s[b] = kk * a
        g_s[b] = _dot(_sigmoid(gd), g2_ref[...])
        bon_s[b] = _segsum_bc(r * k2 * rk_ref[...]) * v

    gb = RW_GROUP if nb % RW_GROUP == 0 else nb
    nr = gb * HEAD_DIM
    eye4 = jnp.where(lax.broadcasted_iota(I32, (nr, 256), 0) % HEAD_DIM
                     == lax.broadcasted_iota(I32, (nr, 256), 1) % HEAD_DIM, 1.0, 0.0)

    def seg(*xs, passes):
        parts = []
        for x, n in zip(xs, passes):
            parts.extend(_split2(x) if n == 2 else (x.astype(BF16),))
        out = jnp.dot(jnp.concatenate(parts, axis=0), e_ref[...], preferred_element_type=F32)
        res, r0 = [], 0
        for n in passes:
            res.append(out[r0:r0 + nr] + out[r0 + nr:r0 + 2 * nr] if n == 2 else out[r0:r0 + nr])
            r0 += n * nr
        return res

    def rows(ref, b0, i):
        return jnp.concatenate([jnp.broadcast_to(ref[b0 + b, pl.ds(i, 1), :], (HEAD_DIM, 256)) for b in range(gb)],
                               axis=0)

    def put_y(ybc, b0, i):
        yd = ybc * eye4
        for b in range(gb):
            y_s[b0 + b, pl.ds(i, 1), :] = jnp.sum(yd[b * HEAD_DIM:(b + 1) * HEAD_DIM], axis=0, keepdims=True)

    def step(tt, carry):
        for b0 in range(0, nb, gb):
            sl = slice(b0 * HEAD_DIM, (b0 + gb) * HEAD_DIM)
            s = s_ref[sl, :]
            vbc = _segsum_bc(eye4 * rows(v_s, b0, tt))
            sa, ybc = seg(s * rows(kk_s, b0, tt), s * rows(r_s, b0, tt + 7), passes=(2, RW_READOUT_PASSES))
            put_y(ybc, b0, tt + 7)
            s_ref[sl, :] = s * rows(w_s, b0, tt) - sa * rows(b_s, b0, tt) + vbc * rows(k_s, b0, tt)
        return carry

    lax.fori_loop(0, t, step, 0)
    for b0 in range(0, nb, gb):
        ybc, = seg(s_ref[b0 * HEAD_DIM:(b0 + gb) * HEAD_DIM, :] * rows(r_s, b0, t + 7), passes=(RW_READOUT_PASSES,))
        put_y(ybc, b0, t + 7)

    for b in range(nb):
        y = y_s[b, 8:, :]
        d = y - _segsum_bc(y) * (1.0 / HEAD_DIM)
        var = _segsum_bc(d * d) * (1.0 / HEAD_DIM)
        y = d * lax.rsqrt(var + RWKV_GN_EPS) * lnw_ref[...] + lnb_ref[...] + bon_s[b]
        o_ref[b] = y * g_s[b]


def _rwkv_mixer(cols, mu, w0, w2, a0, a2, g2, k_k, k_a, r_k, ln_w, ln_b):
    bsz, s, _ = cols.shape
    t = RW_TILE
    full = lambda shp: pl.BlockSpec(shp, lambda c: (0,) * len(shp))
    r1 = lambda v: v.reshape(1, -1)
    tile = lambda: pltpu.VMEM((bsz, t, 256), F32)
    return pl.pallas_call(
        _rwkv_kernel,
        grid=(s // t,),
        in_specs=[pl.BlockSpec((bsz, t, 1024), lambda c: (0, c, COL["rw_r"] // 1024)),
                  full((1, 1024)), full((1, 256)), full((64, 256)), full((1, 256)), full((64, 256)), full((128, 256)),
                  full((1, 256)), full((1, 256)), full((1, 256)), full((1, 256)), full((1, 256)), full((256, 256))],
        out_specs=pl.BlockSpec((bsz, t, 256), lambda c: (0, c, 0)),
        out_shape=jax.ShapeDtypeStruct((bsz, s, 256), F32),
        scratch_shapes=([pltpu.VMEM((bsz, 8, 1024), F32), pltpu.VMEM((bsz * HEAD_DIM, 256), F32),
                         pltpu.VMEM((bsz, t + 8, 256), F32)] + [tile() for _ in range(7)]
                        + [pltpu.VMEM((bsz, t + 8, 256), F32)]),
        compiler_params=_cparams(("arbitrary",)),
    )(cols, r1(mu), r1(w0), w2, r1(a0), a2, g2, r1(k_k), r1(k_a), r1(r_k), r1(ln_w), r1(ln_b),
      jnp.asarray(np.kron(np.eye(N_HEADS), np.ones((HEAD_DIM, HEAD_DIM))), BF16))


NSA_STRIDE = 16
NSA_SEL_BLOCK = 64
NSA_TOPN = 16
NSA_WINDOW = 512
NSA_FORCE = 1e4
NSA_TQ = 256
NSA_TK = 1024
NSA_TK_WIN = 256
NSA_SLOPES = tuple(2.0 ** (-8.0 * (h + 1) / N_HEADS) for h in range(N_HEADS))
SEL_OFF = -float(2 ** 30)


def _dot3_nt(a, b):
    ah, al = _split2(a)
    bh, bl = _split2(b)
    d = lambda x, y: lax.dot_general(x, y, (((1,), (1,)), ((), ())), preferred_element_type=F32)
    return d(ah, bh) + d(ah, bl) + d(al, bh)


def _nsa_compress_kernel(kv_ref, pos_ref, w1_ref, w2_ref, o_ref):
    nbp = o_ref.shape[0]
    outs = []
    for c in range(2):
        acc_a = jnp.zeros((nbp, 256), F32)
        acc_b = jnp.zeros((nbp, 256), F32)
        for r in range(NSA_STRIDE):
            xr = kv_ref[pl.ds(r, nbp, stride=NSA_STRIDE), :][:, HEAD_DIM * c:HEAD_DIM * (c + 1)]
            acc_a = acc_a + _dot(xr + pos_ref[c, r:r + 1, :], w1_ref[c, 64 * r:64 * (r + 1), :])
            acc_b = acc_b + _dot(xr + pos_ref[c, 16 + r:17 + r, :], w1_ref[c, 1024 + 64 * r:1024 + 64 * (r + 1), :])
        hdn = _gelu_tanh(acc_a + pltpu.roll(acc_b, nbp - 1, axis=0))
        outs.append(_dot(hdn, w2_ref[c]))
    o_ref[...] = jnp.concatenate(outs, axis=1)


def _nsa_cmp_kernel(q_ref, kv_ref, ov_ref, ocmp_ref, qaug_ref, used_ref):
    tq = NSA_TQ
    nbp = kv_ref.shape[0]
    q0 = pl.program_id(1) * tq
    kv = kv_ref[...]
    kc, vc = kv[:, :HEAD_DIM], kv[:, HEAD_DIM:]
    pos = q0 + lax.broadcasted_iota(I32, (tq, nbp), 0)
    blk_end = lax.broadcasted_iota(I32, (tq, nbp), 1) * NSA_STRIDE + (2 * NSA_STRIDE - 1)
    dist = pos - blk_end
    mask = dist >= 0
    distf = dist.astype(F32)
    q = q_ref[...] * (HEAD_DIM ** -0.5)
    lane = lax.broadcasted_iota(I32, (tq, 128), 1)
    psum = jnp.zeros((tq, nbp), F32)
    for h in range(N_HEADS):
        qh = q[:, HEAD_DIM * h:HEAD_DIM * (h + 1)]
        s = jnp.where(mask, _dot3_nt(qh, kc) - NSA_SLOPES[h] * distf, -jnp.inf)
        m = jnp.max(s, axis=1, keepdims=True)
        m = jnp.where(m == -jnp.inf, 0.0, m)
        e = jnp.where(mask, jnp.exp(s - m), 0.0)
        p = e / jnp.maximum(jnp.sum(e, axis=1, keepdims=True), 1e-30)
        ocmp_ref[h] = _dot(p, vc)
        psum = psum + p
        qt = q[:, 128 * (h // 2):128 * (h // 2 + 1)]
        keep = (lane < HEAD_DIM) if h % 2 == 0 else (lane >= HEAD_DIM)
        qaug_ref[h, :, 0:128] = jnp.where(keep, qt, 0.0).astype(BF16)
    imp_t = _dot_exact_rhs(psum, ov_ref[...]).T
    j = lax.broadcasted_iota(I32, (128, tq), 0)
    pos_t = q0 + lax.broadcasted_iota(I32, (128, tq), 1)
    cur = pos_t // NSA_SEL_BLOCK
    forced = (j == 0) | (j == cur) | (j == cur - 1)
    valid = j * NSA_SEL_BLOCK <= pos_t
    x = jnp.where(valid, jnp.where(forced, NSA_FORCE, imp_t), -jnp.inf)
    jf = j.astype(F32)
    picked = jnp.zeros((128, tq), F32)
    for _ in range(NSA_TOPN):
        mx = jnp.max(x, axis=0, keepdims=True)
        first = jnp.min(jnp.where(x == mx, jf, 128.0), axis=0, keepdims=True)
        hit = jf == first
        picked = jnp.where(hit, 1.0, picked)
        x = jnp.where(hit, -jnp.inf, x)
    used = jnp.where(valid & (picked > 0.0), 1.0, 0.0)
    bias = jnp.where(used > 0.0, 0.0, SEL_OFF).T
    for h in range(N_HEADS):
        qaug_ref[h, :, 128:256] = bias.astype(BF16)
    used_ref[...] = jnp.broadcast_to(jnp.max(used.T, axis=0, keepdims=True), (8, 128))


def _nsa_flash_kernel(need_ref, qaug_ref, kv_ref, o_ref, kaug_s, v_s, *, window):
    tq = NSA_TQ
    s_len = kv_ref.shape[0]
    tk = min(NSA_TK_WIN if window else NSA_TK, s_len)
    kdim = 128 if window else 256
    qi = pl.program_id(1)
    q0 = qi * tq

    @pl.when(qi == 0)
    def _():
        def build(i, c):
            r0 = pl.multiple_of(i * tk, tk)
            kv = kv_ref[pl.ds(r0, tk), :]
            k = kv[:, :HEAD_DIM]
            parts = [k, k]
            if not window:
                blk = (r0 + lax.broadcasted_iota(I32, (tk, 128), 0)) // NSA_SEL_BLOCK
                parts.append(jnp.where(blk == lax.broadcasted_iota(I32, (tk, 128), 1), 1.0, 0.0))
            kaug_s[pl.ds(r0, tk), :] = jnp.concatenate(parts, axis=1).astype(BF16)
            v_s[pl.ds(r0, tk), :] = kv[:, HEAD_DIM:].astype(BF16)
            return c
        lax.fori_loop(0, s_len // tk, build, 0)

    q = qaug_ref[...][:, :, :kdim].reshape(N_HEADS * tq, kdim)
    row = lax.broadcasted_iota(I32, (N_HEADS * tq, tk), 0)
    t_q = q0 + (row & (tq - 1))
    head = lax.broadcasted_iota(I32, (N_HEADS * tq, 1), 0) // tq
    slope = jnp.where(head == 0, NSA_SLOPES[0], jnp.where(head == 1, NSA_SLOPES[1],
                      jnp.where(head == 2, NSA_SLOPES[2], NSA_SLOPES[3])))
    col = lax.broadcasted_iota(I32, (N_HEADS * tq, tk), 1)

    n_kt = s_len // tk
    need_base = (pl.program_id(0) * pl.num_programs(1) + qi) * n_kt

    def body(kt, carry):
        if window:
            return tile(kt, carry)
        return lax.cond(need_ref[need_base + kt] != 0, lambda c: tile(kt, c), lambda c: c, carry)

    def tile(kt, carry):
        m, l, acc = carry
        k0 = pl.multiple_of(kt * tk, tk)
        s = lax.dot_general(q, kaug_s[pl.ds(k0, tk), :], (((1,), (1,)), ((), ())), preferred_element_type=F32)
        t_k = k0 + col
        s = s + slope * (t_k - q0).astype(F32)
        ok = t_k <= t_q
        if window:
            ok = ok & (t_q - t_k < NSA_WINDOW)
        m_new = jnp.maximum(m, jnp.max(jnp.where(ok, s, NEG_BIG), axis=1, keepdims=True))
        p = jnp.where(ok, jnp.exp(s - m_new), 0.0)
        alpha = jnp.exp(m - m_new)
        l = alpha * l + jnp.sum(p, axis=1, keepdims=True)
        acc = alpha * acc + jnp.dot(p.astype(BF16), v_s[pl.ds(k0, tk), :], preferred_element_type=F32)
        return m_new, l, acc

    lo = jnp.maximum(q0 - NSA_WINDOW, 0) // tk if window else 0
    hi = (q0 + tq - 1) // tk + 1
    init = (jnp.full((N_HEADS * tq, 1), NEG_BIG, F32), jnp.zeros((N_HEADS * tq, 1), F32),
            jnp.zeros((N_HEADS * tq, HEAD_DIM), F32))
    m, l, acc = lax.fori_loop(lo, hi, body, init)
    o_ref[...] = (acc / l).reshape(N_HEADS, tq, HEAD_DIM)


def _nsa_overlap(s):
    nbp, nsel = s // NSA_STRIDE, s // NSA_SEL_BLOCK
    cs = np.arange(nbp) * NSA_STRIDE
    ss = np.arange(128) * NSA_SEL_BLOCK
    ov = np.minimum(cs[:, None] + 2 * NSA_STRIDE, ss[None, :] + NSA_SEL_BLOCK) - np.maximum(cs[:, None], ss[None, :])
    ov = np.clip(ov, 0, None) / (2 * NSA_STRIDE)
    ov[nbp - 1:, :] = 0.0
    ov[:, nsel:] = 0.0
    return jnp.asarray(ov, BF16)


def _nsa_branches(cols, cmp_pos, cmp_w1, cmp_w2):
    bsz, s, _ = cols.shape
    nbp = s // NSA_STRIDE
    tq = NSA_TQ
    kvcmp = pl.pallas_call(
        _nsa_compress_kernel,
        grid=(bsz,),
        in_specs=[pl.BlockSpec((None, s, 128), lambda b: (b, 0, COL["nsa_kvc"] // 128)),
                  pl.BlockSpec((2, 32, 64), lambda b: (0, 0, 0)),
                  pl.BlockSpec((2, 2048, 256), lambda b: (0, 0, 0)),
                  pl.BlockSpec((2, 256, 64), lambda b: (0, 0, 0))],
        out_specs=pl.BlockSpec((None, nbp, 128), lambda b: (b, 0, 0)),
        out_shape=jax.ShapeDtypeStruct((bsz, nbp, 128), F32),
        compiler_params=_cparams(("arbitrary",), 48),
    )(cols, cmp_pos, cmp_w1.astype(BF16), cmp_w2)
    head_out = lambda dt, w: (pl.BlockSpec((None, N_HEADS, tq, w), lambda b, i: (b, 0, i, 0)),
                              jax.ShapeDtypeStruct((bsz, N_HEADS, s, w), dt))
    (oc_spec, oc_shape), (qa_spec, qa_shape) = head_out(F32, HEAD_DIM), head_out(BF16, 256)
    nqt = s // tq
    o_cmp, qaug, used = pl.pallas_call(
        _nsa_cmp_kernel,
        grid=(bsz, nqt),
        in_specs=[pl.BlockSpec((None, tq, 256), lambda b, i: (b, i, COL["nsa_q"] // 256)),
                  pl.BlockSpec((None, nbp, 128), lambda b, i: (b, 0, 0)),
                  pl.BlockSpec((nbp, 128), lambda b, i: (0, 0))],
        out_specs=[oc_spec, qa_spec, pl.BlockSpec((None, None, 8, 128), lambda b, i: (b, i, 0, 0))],
        out_shape=[oc_shape, qa_shape, jax.ShapeDtypeStruct((bsz, nqt, 8, 128), F32)],
        compiler_params=_cparams(("parallel", "parallel"), 48),
    )(cols, kvcmp, _nsa_overlap(s))
    tk = min(NSA_TK, s)
    bpt = tk // NSA_SEL_BLOCK
    need = (used[:, :, 0, :(s // tk) * bpt].reshape(bsz, nqt, s // tk, bpt).max(axis=-1) > 0).astype(I32).reshape(-1)

    def flash(name, window):
        hmap = lambda b, i, need_ref: (b, 0, i, 0)
        return pl.pallas_call(
            functools.partial(_nsa_flash_kernel, window=window),
            grid_spec=pltpu.PrefetchScalarGridSpec(
                num_scalar_prefetch=1,
                grid=(bsz, nqt),
                in_specs=[pl.BlockSpec((None, N_HEADS, tq, 256), hmap),
                          pl.BlockSpec((None, s, 128), lambda b, i, need_ref: (b, 0, COL[name] // 128))],
                out_specs=pl.BlockSpec((None, N_HEADS, tq, HEAD_DIM), hmap),
                scratch_shapes=[pltpu.VMEM((s, 128 if window else 256), BF16), pltpu.VMEM((s, HEAD_DIM), BF16)]),
            out_shape=oc_shape,
            compiler_params=_cparams(("arbitrary", "arbitrary"), 48),
        )(need, qaug, cols)

    return o_cmp, flash("nsa_kvs", False), flash("nsa_kvw", True)


MIX_TILE = 256


def _mix_out_kernel(h_ref, ssm_ref, lru_ref, rw_ref, oc_ref, os_ref, ow_ref, gate_ref, w_ref, o_ref):
    acc = h_ref[...]
    acc = acc + _dot(ssm_ref[...], w_ref[0:256, :])
    acc = acc + _dot(lru_ref[...], w_ref[256:512, :])
    acc = acc + _dot(rw_ref[...], w_ref[512:768, :])
    g = _sigmoid(gate_ref[...])
    for h in range(N_HEADS):
        y = (g[:, 3 * h:3 * h + 1] * oc_ref[h] + g[:, 3 * h + 1:3 * h + 2] * os_ref[h]
             + g[:, 3 * h + 2:3 * h + 3] * ow_ref[h])
        acc = acc + _dot(y, w_ref[768 + HEAD_DIM * h:768 + HEAD_DIM * (h + 1), :])
    o_ref[...] = acc


def _mix_out(h, y_ssm, y_lru, y_rw, o_cmp, o_sel, o_win, cols, w_out):
    bsz, s, d = h.shape
    t = MIX_TILE
    tok = lambda w: pl.BlockSpec((None, t, w), lambda b, i: (b, i, 0))
    hd = pl.BlockSpec((None, N_HEADS, t, HEAD_DIM), lambda b, i: (b, 0, i, 0))
    return pl.pallas_call(
        _mix_out_kernel,
        grid=(bsz, s // t),
        in_specs=[tok(d), tok(256), tok(256), tok(256), hd, hd, hd,
                  pl.BlockSpec((None, t, 128), lambda b, i: (b, i, COL["nsa_gate"] // 128)),
                  pl.BlockSpec((4 * GROUP_WIDTH, d), lambda b, i: (0, 0))],
        out_specs=tok(d),
        out_shape=jax.ShapeDtypeStruct((bsz, s, d), F32),
        compiler_params=_cparams(("parallel", "parallel")),
    )(h, y_ssm, y_lru, y_rw, o_cmp, o_sel, o_win, cols, w_out.astype(BF16))


PEER_HEADS = 8
PEER_NKEYS = 128
PEER_TOPK = 16
PEER_TT = 256
PEER_TQ = 256
PEER_OUT_UNROLL = 128


PEER_ROW = 4


def _store_packed(x, o_ref):
    rows, d = x.shape
    b = pltpu.bitcast(x, I32)
    r = b + 0x7FFF + ((b >> 16) & 1)
    word = (r[:, d // 2:] & jnp.int32(-65536)) | lax.shift_right_logical(r[:, :d // 2], 16)
    for s in range(PEER_ROW):
        o_ref[pl.ds(s, rows, stride=PEER_ROW), :] = word[:, 128 * s:128 * (s + 1)]


def _norm_matmul2_kernel(x_ref, g_ref, w_ref, o_ref, xpk_ref, xn_ref):
    @pl.when(pl.program_id(1) == 0)
    def _():
        x = x_ref[...]
        y = x * lax.rsqrt(jnp.mean(x * x, axis=-1, keepdims=True) + NORM_EPS) * g_ref[...]
        _store_packed(y, xpk_ref)
        xn_ref[...] = y.astype(BF16)

    o_ref[...] = jnp.dot(xn_ref[...], w_ref[...], preferred_element_type=F32)


def _norm_matmul2(x, g, w, tm, tn):
    m, k = x.shape
    n = w.shape[1]
    return pl.pallas_call(
        _norm_matmul2_kernel,
        grid=(m // tm, n // tn),
        in_specs=[pl.BlockSpec((tm, k), lambda i, j: (i, 0)),
                  pl.BlockSpec((1, k), lambda i, j: (0, 0)),
                  pl.BlockSpec((k, tn), lambda i, j: (0, j))],
        out_specs=[pl.BlockSpec((tm, tn), lambda i, j: (i, j)),
                   pl.BlockSpec((tm * PEER_ROW, 128), lambda i, j: (i, 0))],
        out_shape=[jax.ShapeDtypeStruct((m, n), F32), jax.ShapeDtypeStruct((m * PEER_ROW, 128), I32)],
        scratch_shapes=[pltpu.VMEM((tm, k), BF16)],
        compiler_params=_cparams(("arbitrary", "arbitrary")),
    )(x, g, w)


def _top_rows(x, n, payload=None):
    r = x.shape[0]
    rid = lax.broadcasted_iota(I32, x.shape, 0).astype(F32)
    vals, ids, pays = [], [], []
    for _ in range(n):
        mx = jnp.max(x, axis=0, keepdims=True)
        first = jnp.min(jnp.where(x == mx, rid, float(r)), axis=0, keepdims=True)
        hit = rid == first
        vals.append(mx)
        ids.append(first)
        if payload is not None:
            pays.append(jnp.sum(jnp.where(hit, payload, 0.0), axis=0, keepdims=True))
        x = jnp.where(hit, -jnp.inf, x)
    cat = lambda v: jnp.concatenate(v, axis=0)
    return (cat(vals), cat(ids)) + ((cat(pays),) if payload is not None else ())


def _peer_route_kernel(q_ref, keys_ref, eidx_ref, gate_ref):
    k = PEER_TOPK
    tq = q_ref.shape[0]
    gates, eids = [], []
    for h in range(PEER_HEADS):
        s, i = [], []
        for c in range(2):
            qh = q_ref[:, 128 * (2 * h + c):128 * (2 * h + c + 1)]
            sv, si = _top_rows(_dot3_nt(keys_ref[h, c], qh), k)
            s.append(sv)
            i.append(si)
        cand, cidx = [], []
        for a in range(k):
            nb = k // (a + 1)
            cand.append(s[0][a:a + 1, :] + s[1][:nb])
            cidx.append(i[0][a:a + 1, :] * float(PEER_NKEYS) + i[1][:nb])
        n_c = sum(c.shape[0] for c in cand)
        pad = (-n_c) % 8
        cand.append(jnp.full((pad, tq), -jnp.inf, F32))
        cidx.append(jnp.zeros((pad, tq), F32))
        top, _, eidx = _top_rows(jnp.concatenate(cand, axis=0), k, payload=jnp.concatenate(cidx, axis=0))
        e = jnp.exp(top - top[0:1, :])
        gates.append(e / jnp.sum(e, axis=0, keepdims=True))
        eids.append(eidx)
    gate_ref[...] = jnp.concatenate(gates, axis=0).T
    eidx_ref[...] = (jnp.concatenate(eids, axis=0).T * float(PEER_ROW)).astype(I32)


def _peer_route(q, keys):
    t = q.shape[0]
    tq = min(PEER_TQ, t)
    nk = PEER_HEADS * PEER_TOPK
    return pl.pallas_call(
        _peer_route_kernel,
        grid=(t // tq,),
        in_specs=[pl.BlockSpec((tq, q.shape[1]), lambda i: (i, 0)),
                  pl.BlockSpec(keys.shape, lambda i: (0, 0, 0, 0))],
        out_specs=[pl.BlockSpec((tq, nk), lambda i: (i, 0)), pl.BlockSpec((tq, nk), lambda i: (i, 0))],
        out_shape=[jax.ShapeDtypeStruct((t, nk), I32), jax.ShapeDtypeStruct((t, nk), F32)],
        compiler_params=_cparams(("parallel",), 48),
    )(q, keys)


PACK_TILE = 512


def _pack_kernel(t_ref, o_ref):
    _store_packed(t_ref[...], o_ref)


def _pack_table(tab):
    e, d = tab.shape
    te = PACK_TILE
    return pl.pallas_call(
        _pack_kernel,
        grid=(e // te,),
        in_specs=[pl.BlockSpec((te, d), lambda i: (i, 0))],
        out_specs=pl.BlockSpec((te * PEER_ROW, 128), lambda i: (i, 0)),
        out_shape=jax.ShapeDtypeStruct((e * PEER_ROW, 128), I32),
        compiler_params=_cparams(("parallel",)),
    )(tab)


def _expert_halves(tbl_ref, r):
    word = tbl_ref[pl.ds(pl.multiple_of(r, PEER_ROW), PEER_ROW), :]
    return pltpu.bitcast(word << 16, F32), pltpu.bitcast(word & jnp.int32(-65536), F32)


_BITREV8 = (0, 4, 2, 6, 1, 5, 3, 7)


def _rowsum_pairs(cur):
    sub = lax.broadcasted_iota(I32, (8, 128), 0)
    for h in (2, 1):
        m = (sub & h) == 0
        cur = [jnp.where(m, x, y) + jnp.where(m, pltpu.roll(x, 8 - h, axis=0), pltpu.roll(y, h, axis=0))
               for x, y in zip(cur[0::2], cur[1::2])]
    return cur[0]


def _peer_act_kernel(idx_ref, x_ref, gate_ref, tbl_ref, o_ref, q_ref, s_ref):
    tt, nk = gate_ref.shape
    sub = lax.broadcasted_iota(I32, (8, 128), 0)

    @pl.when(pl.program_id(0) == 0)
    def _():
        q_ref[...] = jnp.zeros_like(q_ref)

    def finish(slot, t):
        s_ref[pl.ds(t, 1), :] = jnp.sum(q_ref[slot].T, axis=0, keepdims=True)

    def tok(t, c):
        finish((t + 1) % 2, jnp.maximum(t - 1, 0))
        xw = x_ref[pl.ds(pl.multiple_of(t * PEER_ROW, PEER_ROW), PEER_ROW), :]
        xb = pltpu.bitcast(jnp.concatenate([xw, xw], axis=0), BF16)
        slot = t % 2
        for g in range(nk // 8):
            fs = []
            for i in range(4):
                ra = pl.multiple_of(idx_ref[t * nk + 8 * g + _BITREV8[2 * i]], PEER_ROW)
                rb = pl.multiple_of(idx_ref[t * nk + 8 * g + _BITREV8[2 * i + 1]], PEER_ROW)
                pa = pltpu.bitcast(pltpu.bitcast(tbl_ref[pl.ds(ra + PEER_ROW, 8), :], BF16) * xb, I32)
                pb = pltpu.bitcast(pltpu.bitcast(tbl_ref[pl.ds(rb, 8), :], BF16) * xb, I32)
                prod = jnp.where(sub < PEER_ROW, pa, pb)
                fs.append(pltpu.bitcast(prod << 16, F32) + pltpu.bitcast(prod & jnp.int32(-65536), F32))
            q_ref[slot, 8 * g:8 * (g + 1), :] = _rowsum_pairs(fs)
        return c

    lax.fori_loop(0, tt, tok, 0)
    finish((tt - 1) % 2, tt - 1)
    o_ref[...] = gate_ref[...] * _gelu_tanh(s_ref[...])


def _peer_out_kernel(idx_ref, w_ref, tbl_ref, h_ref, o_ref):
    tt = h_ref.shape[0]
    nk = idx_ref.shape[0] // tt

    def tok(t, c):
        def grp(g, accs):
            accs = list(accs)
            k0 = t * nk + g * PEER_OUT_UNROLL
            for j in range(PEER_OUT_UNROLL):
                w = w_ref[k0 + j]
                vlo, vhi = _expert_halves(tbl_ref, idx_ref[k0 + j])
                a = 2 * (j % 2)
                accs[a] = accs[a] + vlo * w
                accs[a + 1] = accs[a + 1] + vhi * w
            return tuple(accs)

        z = jnp.zeros((PEER_ROW, 128), F32)
        l0, h0, l1, h1 = lax.fori_loop(0, nk // PEER_OUT_UNROLL, grp, (z, z, z, z))
        o_ref[t] = h_ref[t] + jnp.concatenate([l0 + l1, h0 + h1], axis=0)
        return c

    lax.fori_loop(0, tt, tok, 0)


def _peer_experts(eidx_t, gate_t, xn3, h3, tbl_u, tbl_v):
    t, nk = eidx_t.shape
    tt = min(PEER_TT, t)
    smem = lambda: pl.BlockSpec((tt * nk,), lambda i: (i,), memory_space=pltpu.SMEM)
    vm = lambda: pl.BlockSpec((tt, nk), lambda i: (i, 0))
    tok = lambda: pl.BlockSpec((tt, 8, 128), lambda i: (i, 0, 0))
    tbl = lambda a: pl.BlockSpec(a.shape, lambda i: (0, 0), pipeline_mode=pl.Buffered(1))
    eidx_flat = eidx_t.reshape(t * nk)
    w_t = pl.pallas_call(
        _peer_act_kernel,
        grid=(t // tt,),
        in_specs=[smem(), pl.BlockSpec((tt * PEER_ROW, 128), lambda i: (i, 0)), vm(), tbl(tbl_u)],
        out_specs=vm(),
        out_shape=jax.ShapeDtypeStruct((t, nk), F32),
        scratch_shapes=[pltpu.VMEM((2, nk, 128), F32), pltpu.VMEM((tt, nk), F32)],
        compiler_params=_cparams(("arbitrary",), 48),
    )(eidx_flat, xn3, gate_t, tbl_u)
    return pl.pallas_call(
        _peer_out_kernel,
        grid=(t // tt,),
        in_specs=[smem(), smem(), tbl(tbl_v), tok()],
        out_specs=tok(),
        out_shape=jax.ShapeDtypeStruct(h3.shape, F32),
        compiler_params=_cparams(("arbitrary",), 48),
    )(eidx_flat, w_t.reshape(t * nk), tbl_v, h3)


def _peer_block(h, g, wq, keys, u, v):
    t, d = h.shape
    q, xpk = _norm_matmul2(h, g.reshape(1, d), wq.astype(BF16), min(512, t), 1024)
    eidx_t, gate_t = _peer_route(q, keys)
    tbl_u = jnp.pad(_pack_table(u), ((PEER_ROW, PEER_ROW), (0, 0)))
    out = _peer_experts(eidx_t, gate_t, xpk, h.reshape(t, 8, 128), tbl_u, _pack_table(v))
    return out.reshape(t, d)


def _rmsnorm_kernel(x_ref, g_ref, o_ref):
    x = x_ref[...]
    o_ref[...] = x * lax.rsqrt(jnp.mean(x * x, axis=-1, keepdims=True) + NORM_EPS) * g_ref[...]


def _rmsnorm(x, g, tm=512):
    m, d = x.shape
    return pl.pallas_call(
        _rmsnorm_kernel,
        grid=(m // tm,),
        in_specs=[pl.BlockSpec((tm, d), lambda i: (i, 0)), pl.BlockSpec((1, d), lambda i: (0, 0))],
        out_specs=pl.BlockSpec((tm, d), lambda i: (i, 0)),
        out_shape=jax.ShapeDtypeStruct((m, d), F32),
        compiler_params=_cparams(("parallel",)),
    )(x, g.reshape(1, d))


def kernel(x, mix_norm, w_in, w_out, ssm_conv_w, ssm_conv_b, ssm_dt_bias, ssm_a_log, ssm_d, ssm_norm, lru_conv_w, lru_conv_b, lru_wa, lru_ba, lru_wi, lru_bi, lru_lambda, rwkv_mu, rwkv_w0, rwkv_w2, rwkv_a0, rwkv_a2, rwkv_g2, rwkv_kk, rwkv_ka, rwkv_rk, rwkv_ln_w, rwkv_ln_b, nsa_cmp_pos, nsa_cmp_w1, nsa_cmp_w2, ffn_norm, peer_wq, peer_keys, peer_u, peer_v, final_norm):
    bsz, s, d = x.shape
    t = bsz * s
    h = x
    for l in range(w_in.shape[0]):
        w_l = _arrange_cols(w_in[l]).astype(BF16)
        cols = _norm_matmul(h.reshape(t, d), mix_norm[l].reshape(1, d), w_l, 512, 640).reshape(bsz, s, NCOL)
        y_ssm = _ssd_mixer(cols, ssm_conv_w[l], ssm_conv_b[l], ssm_dt_bias[l], ssm_a_log[l], ssm_d[l], ssm_norm[l])
        y_lru = _lru_mixer(cols, lru_conv_w[l], lru_conv_b[l], lru_wa[l], lru_ba[l], lru_wi[l], lru_bi[l], lru_lambda[l])
        y_rw = _rwkv_mixer(cols, rwkv_mu[l], rwkv_w0[l], rwkv_w2[l], rwkv_a0[l], rwkv_a2[l], rwkv_g2[l],
                           rwkv_kk[l], rwkv_ka[l], rwkv_rk[l], rwkv_ln_w[l], rwkv_ln_b[l])
        o_cmp, o_sel, o_win = _nsa_branches(cols, nsa_cmp_pos[l], nsa_cmp_w1[l], nsa_cmp_w2[l])
        h = _mix_out(h, y_ssm, y_lru, y_rw, o_cmp, o_sel, o_win, cols, w_out[l])
        h = _peer_block(h.reshape(t, d), ffn_norm[l], peer_wq[l], peer_keys[l], peer_u[l], peer_v[l]).reshape(bsz, s, d)
    return _rmsnorm(h.reshape(t, d), final_norm).reshape(bsz, s, d)
```
